```python
import math
import jax
import jax.numpy as jnp
from jax import lax
import numpy as np

D_MODEL = 1024
BATCH = 2
SEQ = 8192
DEPTH = 4
DEC_BATCH = 128
DEC_SEQ = 4
PAST_LEN = 8192
PAGE_SIZE = 128

N_EVEN = (DEPTH + 1) // 2
N_ODD = DEPTH // 2
EPS = 1e-6

SSM_D_INNER = D_MODEL
SSM_HEAD_DIM = 64
SSM_HEADS = SSM_D_INNER // SSM_HEAD_DIM
SSM_GROUPS = 4
SSM_STATE = 128
SSM_CONV = 4
SSM_CHUNK = 128
SSM_GN = SSM_GROUPS * SSM_STATE
SSM_CONV_CH = SSM_D_INNER + 2 * SSM_GN

MLA_HEADS = 16
Q_LORA = 256
KV_LORA = 256
QK_NOPE = 64
QK_ROPE = 32
QK_DIM = QK_NOPE + QK_ROPE
V_DIM = 64
MLA_WIDTH = MLA_HEADS * V_DIM
ROPE_THETA = 10000.0
Q_BLOCK = 128

EVEN_SPLITS = (SSM_D_INNER, SSM_CONV_CH, SSM_HEADS, Q_LORA, KV_LORA, QK_ROPE, MLA_WIDTH)
EVEN_IN = sum(EVEN_SPLITS)
EVEN_MIX = SSM_D_INNER + MLA_WIDTH

S5_WIDTH = D_MODEL
S5_GROUP_CH = 16
S5_GROUPS = S5_WIDTH // S5_GROUP_CH
S5_STATE = 64

kernel_name = 'hybrid_ssd_mla_s5_decode_step'

F32 = jnp.float32


def _split(x, sizes):
    offs = []
    acc = 0
    for sz in sizes[:-1]:
        acc += sz
        offs.append(acc)
    return jnp.split(x, offs, axis=-1)


def rmsnorm(x, w):
    xf = x.astype(F32)
    y = xf * lax.rsqrt(jnp.mean(xf * xf, axis=-1, keepdims=True) + EPS)
    return (y * w.astype(F32)).astype(x.dtype)


def rope(x, ang):
    half = QK_ROPE // 2
    xf = x.astype(F32)
    x1, x2 = xf[..., :half], xf[..., half:]
    c, s = jnp.cos(ang), jnp.sin(ang)
    return jnp.concatenate([x1 * c - x2 * s, x1 * s + x2 * c], axis=-1).astype(x.dtype)


def rope_angles(pos):
    half = QK_ROPE // 2
    inv_freq = jnp.power(ROPE_THETA, -jnp.arange(half, dtype=F32) / half)
    return pos[:, None] * inv_freq[None, :]


def causal_dwconv(x, buf, w, b):
    xp = jnp.concatenate([buf.astype(x.dtype), x], axis=1)
    y = lax.conv_general_dilated(xp, w[:, None, :].astype(x.dtype), window_strides=(1,), padding='VALID',
                                 dimension_numbers=('NWC', 'WIO', 'NWC'), feature_group_count=x.shape[-1])
    return y + b.astype(x.dtype), xp[:, xp.shape[1] - (SSM_CONV - 1):]


def segsum_exp(a_cs):
    L = a_cs.shape[-1]
    diff = a_cs[..., :, None] - a_cs[..., None, :]
    mask = jnp.tril(jnp.ones((L, L), dtype=bool))
    return jnp.exp(jnp.where(mask, diff, -jnp.inf))


def ssd_scan(x, dt, A, Bm, Cm, h0):
    b, s, h, p = x.shape
    g, n = Bm.shape[2], Bm.shape[3]
    e = h // g
    L = SSM_CHUNK if s % SSM_CHUNK == 0 else s
    c = s // L
    xf = (x.astype(F32) * dt[..., None]).reshape(b, c, L, g, e, p)
    Bc = Bm.astype(F32).reshape(b, c, L, g, n)
    Cc = Cm.astype(F32).reshape(b, c, L, g, n)
    dA = jnp.moveaxis((dt * A).reshape(b, c, L, g, e), 2, -1)
    a_cs = jnp.cumsum(dA, axis=-1)
    cb = jnp.einsum('bclgn,bcsgn->bcgls', Cc, Bc)
    w = cb[:, :, :, None] * segsum_exp(a_cs)
    y_diag = jnp.einsum('bcgels,bcsgep->bclgep', w, xf)
    dte = jnp.moveaxis(jnp.exp(a_cs[..., -1:] - a_cs), -1, 2)
    states = jnp.einsum('bclgn,bclgep->bcgepn', Bc, xf * dte[..., None])
    states = jnp.concatenate([h0.astype(F32).reshape(b, g, e, p, n)[:, None], states], axis=1)
    chunk_dA = jnp.moveaxis(a_cs[..., -1], 1, -1)
    chunk_cs = jnp.cumsum(jnp.pad(chunk_dA, ((0, 0), (0, 0), (0, 0), (1, 0))), axis=-1)
    decay_chunk = segsum_exp(chunk_cs)
    new_states = jnp.einsum('bgezc,bcgepn->bzgepn', decay_chunk, states)
    prev_states = new_states[:, :-1]
    h_final = new_states[:, -1].reshape(b, h, p, n)
    decay_in = jnp.moveaxis(jnp.exp(a_cs), -1, 2)
    y_off = jnp.einsum('bclgn,bcgepn->bclgep', Cc, prev_states) * decay_in[..., None]
    return (y_diag + y_off).reshape(b, s, h, p), h_final


def mla_kv(ckv, kpe, w_kvb, k_norm_w):
    kv = (ckv @ w_kvb).reshape(ckv.shape[0], ckv.shape[1], MLA_HEADS, QK_NOPE + V_DIM)
    k_nope, v = kv[..., :QK_NOPE], kv[..., QK_NOPE:]
    k_pe = jnp.broadcast_to(kpe[:, :, None, :], k_nope.shape[:-1] + (QK_ROPE,)).astype(k_nope.dtype)
    k = rmsnorm(jnp.concatenate([k_nope, k_pe], axis=-1), k_norm_w)
    return k, v


def attend(q, k, v, q_pos, k_pos):
    s = jnp.einsum('bqhd,bkhd->bhqk', q, k, preferred_element_type=F32) * (QK_DIM ** -0.5)
    s = jnp.where(k_pos[None, :] <= q_pos[:, None], s, -jnp.inf)
    p = jax.nn.softmax(s, axis=-1)
    return jnp.einsum('bhqk,bkhd->bqhd', p.astype(v.dtype), v, preferred_element_type=F32).astype(v.dtype)


def prompt_attention(q, ckv, kpe, w_kvb, k_norm_w):
    k, v = mla_kv(ckv, kpe, w_kvb, k_norm_w)
    b, s = q.shape[0], q.shape[1]
    qb = Q_BLOCK if s % Q_BLOCK == 0 else s
    nb = s // qb
    pos = jnp.arange(s, dtype=jnp.int32)
    q_blocks = jnp.moveaxis(q.reshape(b, nb, qb, MLA_HEADS, QK_DIM), 1, 0)
    pos_blocks = pos.reshape(nb, qb)
    o = lax.map(lambda a: attend(a[0], k, v, a[1], pos), (q_blocks, pos_blocks))
    return jnp.moveaxis(o, 0, 1).reshape(b, s, MLA_HEADS, V_DIM)


def sample_attention(q, ckv, kpe, pool_ckv, pool_kpe, page_table, w_kvb, k_norm_w):
    t = q.shape[1]
    k_pos = jnp.arange(PAST_LEN + t, dtype=jnp.int32)
    q_pos = PAST_LEN + jnp.arange(t, dtype=jnp.int32)

    def one(a):
        pt, q1, c1, p1 = a
        c_all = jnp.concatenate([pool_ckv[pt].reshape(-1, KV_LORA).astype(c1.dtype), c1], axis=0)
        p_all = jnp.concatenate([pool_kpe[pt].reshape(-1, QK_ROPE).astype(p1.dtype), p1], axis=0)
        k, v = mla_kv(c_all[None], p_all[None], w_kvb, k_norm_w)
        return attend(q1[None], k, v, q_pos, k_pos)[0]

    return lax.map(one, (page_table, q, ckv, kpe))


def even_mixer(xn, pos, ssm_h0, conv_buf, attn_fn, w_in, conv_w, conv_b, dt_bias, a_log, d_ssm, ssm_norm_w,
               q_a_norm_w, w_qb, kv_a_norm_w, w_kvb, q_norm_w, k_norm_w, w_out):
    b, s, _ = xn.shape
    z, xbc, dt_raw, q_lat, kv_lat, k_pe, g_mla = _split(xn @ w_in, EVEN_SPLITS)
    xbc, conv_new = causal_dwconv(xbc, conv_buf, conv_w, conv_b)
    xbc = jax.nn.silu(xbc)
    xs, Bm, Cm = _split(xbc, (SSM_D_INNER, SSM_GN, SSM_GN))
    xs = xs.reshape(b, s, SSM_HEADS, SSM_HEAD_DIM)
    dt = jax.nn.softplus(dt_raw.astype(F32) + dt_bias.astype(F32))
    A = -jnp.exp(a_log.astype(F32))
    y, h_new = ssd_scan(xs, dt, A, Bm.reshape(b, s, SSM_GROUPS, SSM_STATE),
                        Cm.reshape(b, s, SSM_GROUPS, SSM_STATE), ssm_h0)
    y = y + d_ssm.astype(F32)[:, None] * xs.astype(F32)
    y = y.reshape(b, s, SSM_D_INNER) * jax.nn.silu(z.astype(F32))
    y = rmsnorm(y.reshape(b, s, SSM_GROUPS, SSM_D_INNER // SSM_GROUPS),
                ssm_norm_w.reshape(SSM_GROUPS, -1)).reshape(b, s, SSM_D_INNER).astype(xn.dtype)
    ang = rope_angles(pos)
    q = (rmsnorm(q_lat, q_a_norm_w) @ w_qb).reshape(b, s, MLA_HEADS, QK_DIM)
    q = jnp.concatenate([q[..., :QK_NOPE], rope(q[..., QK_NOPE:], ang[:, None, :])], axis=-1)
    q = rmsnorm(q, q_norm_w)
    ckv = rmsnorm(kv_lat, kv_a_norm_w)
    kpe = rope(k_pe, ang)
    o = attn_fn(q, ckv, kpe).reshape(b, s, MLA_WIDTH) * jax.nn.silu(g_mla)
    out = jnp.concatenate([y, o.astype(xn.dtype)], axis=-1) @ w_out
    return out, h_new, conv_new, ckv, kpe


def s5_scan(u, h0_re, h0_im, a_re, a_im, b_re, b_im, c_re, c_im, d, log_step):
    bsz, s, _ = u.shape
    uf = u.astype(F32).reshape(bsz, s, S5_GROUPS, S5_GROUP_CH)
    ar, ai = a_re.astype(F32), a_im.astype(F32)
    step = jnp.exp(log_step.astype(F32))[:, None]
    mag = jnp.exp(ar * step)
    ab_re, ab_im = mag * jnp.cos(ai * step), mag * jnp.sin(ai * step)
    den = ar * ar + ai * ai
    nr, ni = ab_re - 1.0, ab_im
    f_re = (nr * ar + ni * ai) / den
    f_im = (ni * ar - nr * ai) / den
    br, bi = b_re.astype(F32), b_im.astype(F32)
    bb_re = f_re[..., None] * br - f_im[..., None] * bi
    bb_im = f_re[..., None] * bi + f_im[..., None] * br
    bu_re = jnp.einsum('bsgc,gnc->sbgn', uf, bb_re)
    bu_im = jnp.einsum('bsgc,gnc->sbgn', uf, bb_im)
    pa_re = jnp.broadcast_to(ab_re[None, None], (s, 1, S5_GROUPS, S5_STATE))
    pa_im = jnp.broadcast_to(ab_im[None, None], (s, 1, S5_GROUPS, S5_STATE))

    def combine(l, r):
        a1r, a1i, b1r, b1i = l
        a2r, a2i, b2r, b2i = r
        return (a2r * a1r - a2i * a1i, a2r * a1i + a2i * a1r,
                a2r * b1r - a2i * b1i + b2r, a2r * b1i + a2i * b1r + b2i)

    pr, pi_, xr, xi = lax.associative_scan(combine, (pa_re, pa_im, bu_re, bu_im), axis=0)
    h0r, h0i = h0_re.astype(F32)[None], h0_im.astype(F32)[None]
    xr2 = xr + pr * h0r - pi_ * h0i
    xi2 = xi + pr * h0i + pi_ * h0r
    y = (jnp.einsum('sbgn,gcn->bsgc', xr2, c_re.astype(F32))
         - jnp.einsum('sbgn,gcn->bsgc', xi2, c_im.astype(F32)))
    y = y.reshape(bsz, s, S5_WIDTH) + d.astype(F32) * u.astype(F32)
    return y, xr2[-1], xi2[-1]


def odd_mixer(xn, h0_re, h0_im, w_in, a_re, a_im, b_re, b_im, c_re, c_im, d, log_step, w_glu, b_glu, w_out):
    u, z = _split(xn @ w_in, (S5_WIDTH, S5_WIDTH))
    y, hr, hi = s5_scan(u, h0_re, h0_im, a_re, a_im, b_re, b_im, c_re, c_im, d, log_step)
    y = jax.nn.gelu(y.astype(xn.dtype))
    y = y * jax.nn.sigmoid(y @ w_glu + b_glu)
    y = y * jax.nn.silu(z)
    return y @ w_out, hr, hi


def setup_inputs(seed: int = 0) -> dict:
    key = jax.random.key(seed)
    ks = iter(jax.random.split(key, 48))

    def nrm(shape, scale):
        return scale * jax.random.normal(next(ks), shape, F32)

    def gain(shape):
        return 1.0 + nrm(shape, 0.01)

    n_pages = PAST_LEN // PAGE_SIZE
    n_used = DEC_BATCH * n_pages
    n_pool = n_used + n_used // 4
    x_prompt = nrm((BATCH, SEQ, D_MODEL), 1.0)
    x_sample = nrm((DEC_BATCH, DEC_SEQ, D_MODEL), 1.0)
    cache_ckv = nrm((N_EVEN, n_pool, PAGE_SIZE, KV_LORA), 1.0)
    cache_kpe = nrm((N_EVEN, n_pool, PAGE_SIZE, QK_ROPE), 1.0)
    page_table = jax.random.permutation(next(ks), n_pool)[:n_used].reshape(DEC_BATCH, n_pages).astype(jnp.int32)
    state_ssm = nrm((N_EVEN, DEC_BATCH, SSM_HEADS, SSM_HEAD_DIM, SSM_STATE), 0.1)
    state_conv = nrm((N_EVEN, DEC_BATCH, SSM_CONV - 1, SSM_CONV_CH), 1.0)
    state_s5_re = nrm((N_ODD, DEC_BATCH, S5_GROUPS, S5_STATE), 0.5)
    state_s5_im = nrm((N_ODD, DEC_BATCH, S5_GROUPS, S5_STATE), 0.5)
    dt0 = jnp.exp(jax.random.uniform(next(ks), (N_EVEN, SSM_HEADS), F32, math.log(1e-3), math.log(1e-1)))
    dt_bias = dt0 + jnp.log(-jnp.expm1(-dt0))
    a_log = jnp.log(jax.random.uniform(next(ks), (N_EVEN, SSM_HEADS), F32, 1.0, 16.0))
    s5_a_re = -0.5 + nrm((N_ODD, S5_GROUPS, S5_STATE), 0.01)
    s5_a_im = math.pi * jnp.arange(S5_STATE, dtype=F32) + nrm((N_ODD, S5_GROUPS, S5_STATE), 0.01)
    s5_log_step = jax.random.uniform(next(ks), (N_ODD, S5_GROUPS), F32, math.log(1e-3), math.log(1e-1))
    return {
        'x_prompt': x_prompt,
        'x_sample': x_sample,
        'cache_ckv': cache_ckv,
        'cache_kpe': cache_kpe,
        'page_table': page_table,
        'state_ssm': state_ssm,
        'state_conv': state_conv,
        'state_s5_re': state_s5_re,
        'state_s5_im': state_s5_im,
        'norm_w': gain((DEPTH, D_MODEL)),
        'w_in_even': nrm((N_EVEN, D_MODEL, EVEN_IN), D_MODEL ** -0.5),
        'conv_w': nrm((N_EVEN, SSM_CONV, SSM_CONV_CH), SSM_CONV ** -0.5),
        'conv_b': nrm((N_EVEN, SSM_CONV_CH), 0.01),
        'dt_bias': dt_bias,
        'a_log': a_log,
        'd_ssm': gain((N_EVEN, SSM_HEADS)),
        'ssm_norm_w': gain((N_EVEN, SSM_D_INNER)),
        'q_a_norm_w': gain((N_EVEN, Q_LORA)),
        'w_qb': nrm((N_EVEN, Q_LORA, MLA_HEADS * QK_DIM), Q_LORA ** -0.5),
        'kv_a_norm_w': gain((N_EVEN, KV_LORA)),
        'w_kvb': nrm((N_EVEN, KV_LORA, MLA_HEADS * (QK_NOPE + V_DIM)), KV_LORA ** -0.5),
        'q_norm_w': gain((N_EVEN, QK_DIM)),
        'k_norm_w': gain((N_EVEN, QK_DIM)),
        'w_out_even': nrm((N_EVEN, EVEN_MIX, D_MODEL), EVEN_MIX ** -0.5),
        'w_in_odd': nrm((N_ODD, D_MODEL, 2 * S5_WIDTH), D_MODEL ** -0.5),
        's5_a_re': s5_a_re,
        's5_a_im': s5_a_im,
        's5_b_re': nrm((N_ODD, S5_GROUPS, S5_STATE, S5_GROUP_CH), (2.0 * S5_GROUP_CH) ** -0.5),
        's5_b_im': nrm((N_ODD, S5_GROUPS, S5_STATE, S5_GROUP_CH), (2.0 * S5_GROUP_CH) ** -0.5),
        's5_c_re': nrm((N_ODD, S5_GROUPS, S5_GROUP_CH, S5_STATE), S5_STATE ** -0.5),
        's5_c_im': nrm((N_ODD, S5_GROUPS, S5_GROUP_CH, S5_STATE), S5_STATE ** -0.5),
        's5_d': nrm((N_ODD, S5_WIDTH), 1.0),
        's5_log_step': s5_log_step,
        'w_glu': nrm((N_ODD, S5_WIDTH, S5_WIDTH), S5_WIDTH ** -0.5),
        'b_glu': nrm((N_ODD, S5_WIDTH), 0.01),
        'w_out_odd': nrm((N_ODD, S5_WIDTH, D_MODEL), S5_WIDTH ** -0.5),
    }


def reference(x_prompt, x_sample, cache_ckv, cache_kpe, page_table, state_ssm, state_conv, state_s5_re,
              state_s5_im, norm_w, w_in_even, conv_w, conv_b, dt_bias, a_log, d_ssm, ssm_norm_w, q_a_norm_w,
              w_qb, kv_a_norm_w, w_kvb, q_norm_w, k_norm_w, w_out_even, w_in_odd, s5_a_re, s5_a_im, s5_b_re,
              s5_b_im, s5_c_re, s5_c_im, s5_d, s5_log_step, w_glu, b_glu, w_out_odd):
    bp, sp = x_prompt.shape[0], x_prompt.shape[1]
    bs, ss = x_sample.shape[0], x_sample.shape[1]
    pos_p = jnp.arange(sp, dtype=F32)
    pos_s = PAST_LEN + jnp.arange(ss, dtype=F32)
    hp, hs = x_prompt, x_sample
    ckv_p_l, kpe_p_l, ckv_s_l, kpe_s_l = [], [], [], []
    ssm_p_l, ssm_s_l, conv_p_l, conv_s_l = [], [], [], []
    s5r_p_l, s5i_p_l, s5r_s_l, s5i_s_l = [], [], [], []
    for i in range(DEPTH):
        j = i // 2
        xn_p = rmsnorm(hp, norm_w[i])
        xn_s = rmsnorm(hs, norm_w[i])
        if i % 2 == 0:
            lw = (w_in_even[j], conv_w[j], conv_b[j], dt_bias[j], a_log[j], d_ssm[j], ssm_norm_w[j],
                  q_a_norm_w[j], w_qb[j], kv_a_norm_w[j], w_kvb[j], q_norm_w[j], k_norm_w[j], w_out_even[j])
            wkvb, knw = w_kvb[j], k_norm_w[j]
            pool_c, pool_k = cache_ckv[j], cache_kpe[j]
            p_fn = lambda q, c, k: prompt_attention(q, c, k, wkvb, knw)
            s_fn = lambda q, c, k: sample_attention(q, c, k, pool_c, pool_k, page_table, wkvb, knw)
            h0_p = jnp.zeros((bp, SSM_HEADS, SSM_HEAD_DIM, SSM_STATE), F32)
            buf_p = jnp.zeros((bp, SSM_CONV - 1, SSM_CONV_CH), x_prompt.dtype)
            out_p, ssm_p, conv_p, ckv_p, kpe_p = even_mixer(xn_p, pos_p, h0_p, buf_p, p_fn, *lw)
            out_s, ssm_s, conv_s, ckv_s, kpe_s = even_mixer(xn_s, pos_s, state_ssm[j], state_conv[j], s_fn, *lw)
            ckv_p_l.append(ckv_p)
            kpe_p_l.append(kpe_p)
            ckv_s_l.append(ckv_s)
            kpe_s_l.append(kpe_s)
            ssm_p_l.append(ssm_p)
            ssm_s_l.append(ssm_s)
            conv_p_l.append(conv_p)
            conv_s_l.append(conv_s)
        else:
            lw = (w_in_odd[j], s5_a_re[j], s5_a_im[j], s5_b_re[j], s5_b_im[j], s5_c_re[j], s5_c_im[j],
                  s5_d[j], s5_log_step[j], w_glu[j], b_glu[j], w_out_odd[j])
            z0 = jnp.zeros((bp, S5_GROUPS, S5_STATE), F32)
            out_p, r_p, i_p = odd_mixer(xn_p, z0, z0, *lw)
            out_s, r_s, i_s = odd_mixer(xn_s, state_s5_re[j], state_s5_im[j], *lw)
            s5r_p_l.append(r_p)
            s5i_p_l.append(i_p)
            s5r_s_l.append(r_s)
            s5i_s_l.append(i_s)
        hp = hp + out_p
        hs = hs + out_s
    return (hp, hs,
            jnp.stack(ckv_p_l), jnp.stack(kpe_p_l), jnp.stack(ckv_s_l), jnp.stack(kpe_s_l),
            jnp.stack(ssm_p_l), jnp.stack(ssm_s_l), jnp.stack(conv_p_l), jnp.stack(conv_s_l),
            jnp.stack(s5r_p_l), jnp.stack(s5i_p_l), jnp.stack(s5r_s_l), jnp.stack(s5i_s_l))
```

```python
import functools
import math

import jax
import jax.numpy as jnp
from jax import lax
from jax.experimental import pallas as pl
from jax.experimental.pallas import tpu as pltpu

F32 = jnp.float32
BF16 = jnp.bfloat16
EPS = 1e-6

D_MODEL = 1024
LANES = 128
SUBLANES = 8
VMEM_LIMIT = 48 * 1024 * 1024

SSM_D_INNER = 1024
SSM_HEAD_DIM = 64
SSM_HEADS = 16
SSM_GROUPS = 4
SSM_STATE = 128
SSM_CONV = 4
SSM_CHUNK = 128
SSM_GN = SSM_GROUPS * SSM_STATE
SSM_CONV_CH = SSM_D_INNER + 2 * SSM_GN

MLA_HEADS = 16
Q_LORA = 256
KV_LORA = 256
QK_NOPE = 64
QK_ROPE = 32
QK_DIM = QK_NOPE + QK_ROPE
V_DIM = 64
MLA_WIDTH = MLA_HEADS * V_DIM
ROPE_THETA = 10000.0
PAGE = 128

S5_WIDTH = 1024
S5_GROUP_CH = 16
S5_GROUPS = 64
S5_STATE = 64
S5_SLABS = S5_WIDTH // LANES
S5_SLAB_STATE = (LANES // S5_GROUP_CH) * S5_STATE

C_XBC, C_Z, C_G, C_QL, C_KVL, C_KPE, C_KPESW, C_DT, N_EVEN_PAD = 0, 2048, 3072, 4096, 4352, 4608, 4736, 4864, 5120


def _cparams(*sem):
    return pltpu.CompilerParams(dimension_semantics=sem, vmem_limit_bytes=VMEM_LIMIT)


def _silu(x):
    return x * (1.0 / (1.0 + jnp.exp(-x)))


def _dot(a, b):
    return jnp.dot(a, b, preferred_element_type=F32)


def _dot_nt(a, b):
    return lax.dot_general(a, b, (((1,), (1,)), ((), ())), preferred_element_type=F32)


def _split3(v):
    hi = v.astype(BF16)
    r1 = v - hi.astype(F32)
    mid = r1.astype(BF16)
    lo = (r1 - mid.astype(F32)).astype(BF16)
    return hi, mid, lo


def _dot_exact_lhs(m_bf16, v):
    hi, mid, lo = _split3(v)
    return _dot(m_bf16, hi) + _dot(m_bf16, mid) + _dot(m_bf16, lo)


def _dot_exact_rhs(v, m_bf16):
    hi, mid, lo = _split3(v)
    return _dot(hi, m_bf16) + _dot(mid, m_bf16) + _dot(lo, m_bf16)


def _inproj_body(x_ref, nw_ref, w_ref, o_ref, xn_ref):
    @pl.when(pl.program_id(1) == 0)
    def _():
        x = x_ref[...]
        ms = jnp.mean(x * x, axis=-1, keepdims=True)
        xn_ref[...] = (x * lax.rsqrt(ms + EPS) * nw_ref[...]).astype(BF16)

    o_ref[...] = _dot(xn_ref[...], w_ref[...])


def _inproj(x, norm_w, w, tm=512, tn=1024):
    m, k = x.shape
    n = w.shape[1]
    tm = min(tm, m)
    return pl.pallas_call(
        _inproj_body,
        grid=(m // tm, n // tn),
        in_specs=[
            pl.BlockSpec((tm, k), lambda i, j: (i, 0)),
            pl.BlockSpec((1, k), lambda i, j: (0, 0)),
            pl.BlockSpec((k, tn), lambda i, j: (0, j)),
        ],
        out_specs=pl.BlockSpec((tm, tn), lambda i, j: (i, j)),
        out_shape=jax.ShapeDtypeStruct((m, n), F32),
        scratch_shapes=[pltpu.VMEM((tm, k), BF16)],
        compiler_params=_cparams("parallel", "arbitrary"),
        name="inproj",
    )(x, norm_w.reshape(1, k), w)


def _ssd_body(xbc_ref, z_ref, dt_ref, h0_ref, convw_ref, convb_ref, dtb_ref, alog_ref, dexp_ref, nw_ref, e_ref,
              y_ref, hout_ref,
              xp_ref, ysc_ref, xwt_ref, b_ref, c_ref, cse_ref, tott_ref,
              *, L, P, n_hist, n_real, nseq, nchunks, carry):
    blk = pl.program_id(0)
    s = pl.program_id(1)
    chunk = blk % nchunks
    HIST = SUBLANES

    @pl.when(s == 0)
    def _intra():
        if carry:
            @pl.when(chunk == 0)
            def _():
                xp_ref[0:HIST, :] = jnp.zeros((HIST, SSM_CONV_CH), F32)
        else:
            xp_ref[0:HIST, :] = jnp.zeros((HIST, SSM_CONV_CH), F32)
        xp_ref[HIST:HIST + L, :] = xbc_ref[...]
        conv = convb_ref[...] + convw_ref[3:4, :] * xp_ref[HIST:HIST + L, :]
        for k in range(SSM_CONV - 1):
            off = HIST - (SSM_CONV - 1) + k
            conv = conv + convw_ref[k:k + 1, :] * xp_ref[off:off + L, :]
        if carry:
            xp_ref[0:HIST, :] = xp_ref[L:L + HIST, :]
        xc = _silu(conv)
        xs = xc[:, :SSM_D_INNER]
        b_ref[...] = xc[:, SSM_D_INNER:SSM_D_INNER + SSM_GN].astype(BF16)
        c_ref[...] = xc[:, SSM_D_INNER + SSM_GN:].astype(BF16)

        raw = dt_ref[...] + dtb_ref[...]
        dt = jnp.maximum(raw, 0.0) + jnp.log1p(jnp.exp(-jnp.abs(raw)))
        ri = lax.broadcasted_iota(jnp.int32, (L, L), 0)
        ci = lax.broadcasted_iota(jnp.int32, (L, L), 1)
        if P < L:
            rp = lax.broadcasted_iota(jnp.int32, (L, LANES), 0) % P
            dt = jnp.where((rp >= n_hist) & (rp < n_hist + n_real), dt, 0.0)
            same = (ri // P) == (ci // P)
            causal = same & (ci <= ri)
        else:
            same = ri >= 0
            causal = ci <= ri
        a_neg = -jnp.exp(alog_ref[...])
        da = dt * a_neg
        m_cum = jnp.where(causal, 1.0, 0.0).astype(BF16)
        m_tot = jnp.where(same, 1.0, 0.0).astype(BF16)
        cs = _dot_exact_lhs(m_cum, da)
        tot = _dot_exact_lhs(m_tot, da)
        cst = cs.T
        e = e_ref[...]
        dt_e = _dot_exact_rhs(dt, e)
        cs_e = _dot_exact_rhs(cs, e)
        tot_e = _dot_exact_rhs(tot, e)
        cse_ref[...] = cs_e
        tott_ref[...] = tot_e.T
        xdt = xs * dt_e
        xw = xdt * jnp.exp(tot_e - cs_e)
        xwt_ref[...] = xw.T.astype(BF16)
        xdt16 = xdt.astype(BF16)
        lane = lax.broadcasted_iota(jnp.int32, (L, LANES), 1)
        for g in range(SSM_GROUPS):
            cb = _dot_nt(c_ref[:, g * SSM_STATE:(g + 1) * SSM_STATE], b_ref[:, g * SSM_STATE:(g + 1) * SSM_STATE])
            for pr in range(2):
                h0i = 4 * g + 2 * pr
                col = h0i // 2
                xpair = xdt16[:, col * LANES:(col + 1) * LANES]
                ys = []
                for hh in (h0i, h0i + 1):
                    dec = jnp.exp(jnp.where(causal, cs[:, hh:hh + 1] - cst[hh:hh + 1, :], -1e30))
                    ys.append(_dot((cb * dec).astype(BF16), xpair))
                ypair = jnp.where(lane < SSM_HEAD_DIM, ys[0], ys[1])
                ysc_ref[:, col * LANES:(col + 1) * LANES] = (
                    ypair + dexp_ref[:, col * LANES:(col + 1) * LANES] * xs[:, col * LANES:(col + 1) * LANES])

    if carry:
        @pl.when(chunk == 0)
        def _():
            hout_ref[...] = h0_ref[...]
    else:
        hout_ref[...] = h0_ref[...]

    if nseq > 1:
        rmask = (lax.broadcasted_iota(jnp.int32, (L, LANES), 0) // P) == s
        cmask = (lax.broadcasted_iota(jnp.int32, (LANES, L), 1) // P) == s
        onehot = lax.broadcasted_iota(jnp.int32, (LANES, L), 1) == s * P
    for col in range(SSM_HEADS // 2):
        g = col // 2
        sp = hout_ref[col * LANES:(col + 1) * LANES, :]
        yoff = _dot_nt(c_ref[:, g * SSM_STATE:(g + 1) * SSM_STATE], sp.astype(BF16))
        yoff = yoff * jnp.exp(cse_ref[:, col * LANES:(col + 1) * LANES])
        xwt = xwt_ref[col * LANES:(col + 1) * LANES, :]
        tott = tott_ref[col * LANES:(col + 1) * LANES, :]
        if nseq > 1:
            yoff = jnp.where(rmask, yoff, 0.0)
            xwt = jnp.where(cmask, xwt, jnp.zeros_like(xwt))
            deccol = jnp.exp(jnp.sum(jnp.where(onehot, tott, 0.0), axis=1, keepdims=True))
        else:
            deccol = jnp.exp(tott[:, 0:1])
        ysc_ref[:, col * LANES:(col + 1) * LANES] += yoff
        hout_ref[col * LANES:(col + 1) * LANES, :] = sp * deccol + _dot(xwt, b_ref[:, g * SSM_STATE:(g + 1) * SSM_STATE])

    @pl.when(s == nseq - 1)
    def _epilogue():
        gw = SSM_D_INNER // SSM_GROUPS
        for g in range(SSM_GROUPS):
            y = ysc_ref[:, g * gw:(g + 1) * gw] * _silu(z_ref[:, g * gw:(g + 1) * gw])
            ms = jnp.mean(y * y, axis=-1, keepdims=True)
            y_ref[:, g * gw:(g + 1) * gw] = y * lax.rsqrt(ms + EPS) * nw_ref[:, g * gw:(g + 1) * gw]


def _ssd(xbc_src, z_src, dt_src, h0, lw, *, nblk, nchunks, nseq, P, n_hist, n_real, carry,
         xbc_col, z_col, dt_col):
    L = SSM_CHUNK
    nstate = h0.shape[0]
    if carry:
        state_idx = lambda b, s: (b // nchunks, 0, 0)
    else:
        state_idx = lambda b, s: (b * nseq + s, 0, 0)
    full = lambda shape: pl.BlockSpec(shape, lambda b, s: (0,) * len(shape))
    body = functools.partial(_ssd_body, L=L, P=P, n_hist=n_hist, n_real=n_real, nseq=nseq, nchunks=nchunks,
                             carry=carry)
    return pl.pallas_call(
        body,
        grid=(nblk, nseq),
        in_specs=[
            pl.BlockSpec((L, SSM_CONV_CH), lambda b, s: (b, xbc_col)),
            pl.BlockSpec((L, SSM_D_INNER), lambda b, s: (b, z_col)),
            pl.BlockSpec((L, LANES), lambda b, s: (b, dt_col)),
            pl.BlockSpec((None, SSM_D_INNER, SSM_STATE), state_idx),
            full((SSM_CONV, SSM_CONV_CH)),
            full((1, SSM_CONV_CH)),
            full((1, LANES)),
            full((1, LANES)),
            full((1, SSM_D_INNER)),
            full((1, SSM_D_INNER)),
            full((LANES, SSM_D_INNER)),
        ],
        out_specs=[
            pl.BlockSpec((L, SSM_D_INNER), lambda b, s: (b, 0)),
            pl.BlockSpec((None, SSM_D_INNER, SSM_STATE), state_idx),
        ],
        out_shape=[
            jax.ShapeDtypeStruct((nblk * L, SSM_D_INNER), F32),
            jax.ShapeDtypeStruct((nstate, SSM_D_INNER, SSM_STATE), F32),
        ],
        scratch_shapes=[
            pltpu.VMEM((L + SUBLANES, SSM_CONV_CH), F32),
            pltpu.VMEM((L, SSM_D_INNER), F32),
            pltpu.VMEM((SSM_D_INNER, L), BF16),
            pltpu.VMEM((L, SSM_GN), BF16),
            pltpu.VMEM((L, SSM_GN), BF16),
            pltpu.VMEM((L, SSM_D_INNER), F32),
            pltpu.VMEM((SSM_D_INNER, L), F32),
        ],
        compiler_params=_cparams("arbitrary", "arbitrary"),
        name="ssd",
    )(xbc_src, z_src, dt_src, h0, lw["conv_w"], lw["conv_b"], lw["dt_bias"], lw["a_log"], lw["d_exp"],
      lw["ssm_norm_w"], lw["head_expand"])


def _mla_proj_body(ql_ref, kvl_ref, kpe_ref, kpesw_ref, cos_ref, sin_ref, qanw_ref, wq_ref, wqsw_ref, qnw_ref,
                   kvanw_ref, wk_ref, wv_ref, knw_ref,
                   q_ref, ckv_ref, kper_ref, *kv_refs, with_kv):
    cos = cos_ref[...]
    sin = sin_ref[...]
    ql = ql_ref[...]
    qn = (ql * lax.rsqrt(jnp.mean(ql * ql, axis=-1, keepdims=True) + EPS) * qanw_ref[...]).astype(BF16)
    q0 = _dot(qn, wq_ref[...])
    q1 = _dot(qn, wqsw_ref[...])
    scale = QK_DIM ** -0.5
    for h in range(MLA_HEADS):
        sl = slice(h * LANES, (h + 1) * LANES)
        qh = q0[:, sl] * cos + q1[:, sl] * sin
        ms = jnp.sum(qh * qh, axis=-1, keepdims=True) * (1.0 / QK_DIM)
        q_ref[:, sl] = (qh * lax.rsqrt(ms + EPS) * (qnw_ref[...] * scale)).astype(BF16)
    kvl = kvl_ref[...]
    ckv = kvl * lax.rsqrt(jnp.mean(kvl * kvl, axis=-1, keepdims=True) + EPS) * kvanw_ref[...]
    ckv_ref[...] = ckv
    kper = kpe_ref[...] * cos + kpesw_ref[...] * sin
    kper_ref[...] = kper
    if with_kv:
        k_ref, v_ref = kv_refs
        c16 = ckv.astype(BF16)
        kn = _dot(c16, wk_ref[...])
        for h in range(MLA_HEADS):
            sl = slice(h * LANES, (h + 1) * LANES)
            kh = kn[:, sl] + kper
            ms = jnp.sum(kh * kh, axis=-1, keepdims=True) * (1.0 / QK_DIM)
            k_ref[:, sl] = (kh * lax.rsqrt(ms + EPS) * knw_ref[...]).astype(BF16)
        v_ref[...] = _dot(c16, wv_ref[...]).astype(BF16)


def _mla_proj(proj, cos_t, sin_t, lw, *, with_kv, tm=256):
    m = proj.shape[0]
    tm = min(tm, m)
    hp = MLA_HEADS * LANES
    full = lambda shape: pl.BlockSpec(shape, lambda i: (0,) * len(shape))
    row = lambda w, c: pl.BlockSpec((tm, w), lambda i: (i, c))
    out_specs = [row(hp, 0), row(KV_LORA, 0), row(LANES, 0)]
    out_shape = [jax.ShapeDtypeStruct((m, hp), BF16), jax.ShapeDtypeStruct((m, KV_LORA), F32),
                 jax.ShapeDtypeStruct((m, LANES), F32)]
    if with_kv:
        out_specs += [row(hp, 0), row(MLA_WIDTH, 0)]
        out_shape += [jax.ShapeDtypeStruct((m, hp), BF16), jax.ShapeDtypeStruct((m, MLA_WIDTH), BF16)]
    return pl.pallas_call(
        functools.partial(_mla_proj_body, with_kv=with_kv),
        grid=(m // tm,),
        in_specs=[
            row(Q_LORA, C_QL // Q_LORA), row(KV_LORA, C_KVL // KV_LORA), row(LANES, C_KPE // LANES),
            row(LANES, C_KPESW // LANES), row(LANES, 0), row(LANES, 0),
            full((1, Q_LORA)), full((Q_LORA, hp)), full((Q_LORA, hp)), full((1, LANES)),
            full((1, KV_LORA)), full((KV_LORA, hp)), full((KV_LORA, MLA_WIDTH)), full((1, LANES)),
        ],
        out_specs=out_specs,
        out_shape=out_shape,
        compiler_params=_cparams("parallel"),
        name="mla_proj",
    )(proj, proj, proj, proj, cos_t, sin_t, lw["q_a_norm_w"], lw["w_q"], lw["w_qsw"], lw["q_norm_pat"],
      lw["kv_a_norm_w"], lw["w_k"], lw["w_v"], lw["k_norm_pat"])


def _flash_body(q_ref, k_ref, v_ref, o_ref, *, tq):
    qi = pl.program_id(2)
    lane = lax.broadcasted_iota(jnp.int32, (tq, LANES), 1)
    ri = lax.broadcasted_iota(jnp.int32, (tq, tq), 0)
    ci = lax.broadcasted_iota(jnp.int32, (tq, tq), 1)

    def step(q, j, koff, carry, masked):
        m, l, acc = carry
        k = k_ref[pl.ds(koff, tq), j * LANES:(j + 1) * LANES]
        v = v_ref[pl.ds(koff, tq), :]
        sc = _dot_nt(q, k)
        if masked:
            sc = jnp.where(ci <= ri, sc, -jnp.inf)
        m_new = jnp.maximum(m, jnp.max(sc, axis=-1, keepdims=True))
        p = jnp.exp(sc - m_new)
        alpha = jnp.exp(m - m_new)
        l = alpha * l + jnp.sum(p, axis=-1, keepdims=True)
        acc = alpha * acc + _dot(p.astype(BF16), v)
        return m_new, l, acc

    outs = []
    for j in range(2):
        q = q_ref[:, j * LANES:(j + 1) * LANES]
        init = (jnp.full((tq, 1), -jnp.inf, F32), jnp.zeros((tq, 1), F32), jnp.zeros((tq, LANES), F32))
        carry = lax.fori_loop(
            0, qi, lambda ki, c: step(q, j, pl.multiple_of(ki * tq, tq), c, False), init)
        m, l, acc = step(q, j, pl.multiple_of(qi * tq, tq), carry, True)
        outs.append(acc / l)
    o_ref[...] = jnp.where(lane < V_DIM, outs[0], outs[1])


def _flash(q, k, v, *, batch, seq, tq=256):
    tq = min(tq, seq)
    nq = seq // tq
    return pl.pallas_call(
        functools.partial(_flash_body, tq=tq),
        grid=(batch, MLA_HEADS // 2, nq),
        in_specs=[
            pl.BlockSpec((tq, 2 * LANES), lambda b, h, i: (b * nq + i, h)),
            pl.BlockSpec((seq, 2 * LANES), lambda b, h, i: (b, h)),
            pl.BlockSpec((seq, LANES), lambda b, h, i: (b, h)),
        ],
        out_specs=pl.BlockSpec((tq, LANES), lambda b, h, i: (b * nq + i, h)),
        out_shape=jax.ShapeDtypeStruct((batch * seq, MLA_WIDTH), F32),
        compiler_params=_cparams("parallel", "parallel", "arbitrary"),
        name="flash",
    )(q, k, v)


def _qabs_body(q_ref, knw_ref, wabs_ref, epe_ref, qabs_ref, qpe_ref):
    qh = (q_ref[...].astype(F32) * knw_ref[...]).astype(BF16)
    qabs_ref[...] = _dot(qh, wabs_ref[...]).astype(BF16)
    qpe_ref[...] = _dot(qh, epe_ref[...]).astype(BF16)


def _qabs(q, lw):
    m = q.shape[0]
    return pl.pallas_call(
        _qabs_body,
        grid=(MLA_HEADS,),
        in_specs=[
            pl.BlockSpec((m, LANES), lambda h: (0, h)),
            pl.BlockSpec((1, LANES), lambda h: (0, 0)),
            pl.BlockSpec((None, LANES, KV_LORA), lambda h: (h, 0, 0)),
            pl.BlockSpec((LANES, LANES), lambda h: (0, 0)),
        ],
        out_specs=[pl.BlockSpec((m, KV_LORA), lambda h: (0, h)), pl.BlockSpec((m, LANES), lambda h: (0, h))],
        out_shape=[jax.ShapeDtypeStruct((m, MLA_HEADS * KV_LORA), BF16),
                   jax.ShapeDtypeStruct((m, MLA_HEADS * LANES), BF16)],
        compiler_params=_cparams("parallel"),
        name="qabs",
    )(q, lw["k_norm_pat"], lw["w_abs"], lw["e_pe"])


def _decode_body(pt_ref, *refs, pg, ngroups, nq):
    del pt_ref
    c_refs = refs[:pg]
    p_refs = refs[pg:2 * pg]
    (cnew_ref, pnew_ref, wt_ref, qabs_ref, qpe_ref, wv_ref, o_ref, m_ref, l_ref, acc_ref, kpad_ref) = refs[2 * pg:]
    g = pl.program_id(1)
    rows = nq * MLA_HEADS

    @pl.when(g == 0)
    def _():
        m_ref[...] = jnp.full((rows, 1), -jnp.inf, F32)
        l_ref[...] = jnp.zeros((rows, 1), F32)
        acc_ref[...] = jnp.zeros((rows, KV_LORA), F32)
        kpad_ref[...] = jnp.zeros(kpad_ref.shape, F32)

    ones = jnp.ones((SUBLANES, LANES), BF16)

    def process(c, t, mask):
        c16 = c.astype(BF16)
        kt = _dot_nt(wt_ref[...], c16)
        ssq = jnp.sum((kt * kt).reshape(MLA_HEADS, QK_NOPE, t), axis=1)
        kp = kpad_ref[0:t, :]
        kp2 = kp * kp
        kp2h = kp2.astype(BF16)
        kp2l = (kp2 - kp2h.astype(F32)).astype(BF16)
        ssq_pe = (_dot_nt(ones, kp2h) + _dot_nt(ones, kp2l))[0:1, :]
        r = lax.rsqrt((ssq + ssq_pe) * (1.0 / QK_DIM) + EPS)
        st = _dot_nt(qabs_ref[...], c16) + _dot_nt(qpe_ref[...], kp.astype(BF16))
        st = (st.reshape(nq, MLA_HEADS, t) * r[None]).reshape(rows, t)
        if mask is not None:
            st = jnp.where(mask, st, -jnp.inf)
        m_old = m_ref[...]
        m_new = jnp.maximum(m_old, jnp.max(st, axis=-1, keepdims=True))
        p = jnp.exp(st - m_new)
        alpha = jnp.exp(m_old - m_new)
        l_ref[...] = alpha * l_ref[...] + jnp.sum(p, axis=-1, keepdims=True)
        acc_ref[...] = alpha * acc_ref[...] + _dot(p.astype(BF16), c16)
        m_ref[...] = m_new

    @pl.when(g < ngroups)
    def _():
        for k in range(0, pg, 2):
            kpad_ref[0:PAGE, 0:QK_ROPE] = p_refs[k][...]
            kpad_ref[PAGE:2 * PAGE, 0:QK_ROPE] = p_refs[k + 1][...]
            c = jnp.concatenate([c_refs[k][...], c_refs[k + 1][...]], axis=0)
            process(c, 2 * PAGE, None)

    @pl.when(g == ngroups)
    def _():
        kpad_ref[0:PAGE, 0:QK_ROPE] = pnew_ref[...]
        qrow = lax.broadcasted_iota(jnp.int32, (rows, PAGE), 0) // MLA_HEADS
        tok = lax.broadcasted_iota(jnp.int32, (rows, PAGE), 1)
        process(cnew_ref[...], PAGE, tok <= qrow)
        olat = (acc_ref[...] / l_ref[...]).astype(BF16)
        of = _dot(olat, wv_ref[...])
        rh = lax.broadcasted_iota(jnp.int32, (rows, MLA_WIDTH), 0) % MLA_HEADS
        ch = lax.broadcasted_iota(jnp.int32, (rows, MLA_WIDTH), 1) // V_DIM
        of = jnp.where(rh == ch, of, 0.0)
        o_ref[...] = jnp.sum(of.reshape(nq, MLA_HEADS, MLA_WIDTH), axis=1)


def _decode_attn(page_table, pool_c, pool_p, cnew, pnew, qabs, qpe, lw, *, pg=8):
    nb, npages = page_table.shape
    nq = qabs.shape[1] // MLA_HEADS
    rows = nq * MLA_HEADS
    ngroups = npages // pg
    last = ngroups - 1

    def page_spec(width, k):
        return pl.BlockSpec((None, PAGE, width),
                            lambda b, g, pt: (pt[b, jnp.minimum(g, last) * pg + k], 0, 0))

    per_seq = lambda shape: pl.BlockSpec((None,) + shape, lambda b, g, pt: (b, 0, 0))
    full = lambda shape: pl.BlockSpec(shape, lambda b, g, pt: (0,) * len(shape))
    in_specs = ([page_spec(KV_LORA, k) for k in range(pg)] + [page_spec(QK_ROPE, k) for k in range(pg)]
                + [per_seq((PAGE, KV_LORA)), per_seq((PAGE, QK_ROPE)), full((MLA_HEADS * QK_NOPE, KV_LORA)),
                   per_seq((rows, KV_LORA)), per_seq((rows, LANES)), full((KV_LORA, MLA_WIDTH))])
    grid_spec = pltpu.PrefetchScalarGridSpec(
        num_scalar_prefetch=1,
        grid=(nb, ngroups + 1),
        in_specs=in_specs,
        out_specs=per_seq((nq, MLA_WIDTH)),
        scratch_shapes=[pltpu.VMEM((rows, 1), F32), pltpu.VMEM((rows, 1), F32), pltpu.VMEM((rows, KV_LORA), F32),
                        pltpu.VMEM((2 * PAGE, LANES), F32)],
    )
    return pl.pallas_call(
        functools.partial(_decode_body, pg=pg, ngroups=ngroups, nq=nq),
        grid_spec=grid_spec,
        out_shape=jax.ShapeDtypeStruct((nb, nq, MLA_WIDTH), F32),
        compiler_params=_cparams("parallel", "arbitrary"),
        name="decode_attn",
    )(page_table, *([pool_c] * pg), *([pool_p] * pg), cnew, pnew, lw["w_nope_t"], qabs, qpe, lw["w_v"])


def _outproj_even_body(y_ref, o_ref, g_ref, h_ref, wy_ref, wo_ref, out_ref):
    og = (o_ref[...] * _silu(g_ref[...])).astype(BF16)
    out_ref[...] = h_ref[...] + _dot(y_ref[...].astype(BF16), wy_ref[...]) + _dot(og, wo_ref[...])


def _outproj_even(y, o, proj, h, w_out, tm=512):
    m = h.shape[0]
    tm = min(tm, m)
    row = lambda c: pl.BlockSpec((tm, D_MODEL), lambda i: (i, c))
    return pl.pallas_call(
        _outproj_even_body,
        grid=(m // tm,),
        in_specs=[row(0), row(0), row(C_G // D_MODEL), row(0),
                  pl.BlockSpec((SSM_D_INNER, D_MODEL), lambda i: (0, 0)),
                  pl.BlockSpec((MLA_WIDTH, D_MODEL), lambda i: (1, 0))],
        out_specs=row(0),
        out_shape=jax.ShapeDtypeStruct((m, D_MODEL), F32),
        compiler_params=_cparams("parallel"),
        name="outproj_even",
    )(y, o, proj, h, w_out, w_out)


def _gelu(x):
    return 0.5 * x * (1.0 + jnp.tanh(math.sqrt(2.0 / math.pi) * (x + 0.044715 * (x * x * x))))


def _s5_seq_body(u_ref, bre_ref, bim_ref, cre_ref, cim_ref, d_ref, are_ref, aim_ref, pre_ref, pim_ref,
                 y_ref, hre_ref, him_ref, xre_ref, xim_ref, car_ref, cai_ref, *, lc):
    c = pl.program_id(1)
    nt = lc // SUBLANES
    ns = S5_SLAB_STATE

    @pl.when(c == 0)
    def _():
        car_ref[...] = jnp.zeros(car_ref.shape, F32)
        cai_ref[...] = jnp.zeros(cai_ref.shape, F32)

    for j in range(S5_SLABS):
        uj = u_ref[:, j * LANES:(j + 1) * LANES].astype(BF16)
        re = _dot(uj, bre_ref[j])
        im = _dot(uj, bim_ref[j])
        for si in range(3):
            sh = 1 << si
            sre = pltpu.roll(re, sh, 0).reshape(nt, SUBLANES, ns)
            sim = pltpu.roll(im, sh, 0).reshape(nt, SUBLANES, ns)
            ar = are_ref[j, si][None]
            ai = aim_ref[j, si][None]
            re3 = re.reshape(nt, SUBLANES, ns) + ar * sre - ai * sim
            im3 = im.reshape(nt, SUBLANES, ns) + ar * sim + ai * sre
            re = re3.reshape(lc, ns)
            im = im3.reshape(lc, ns)
        xre_ref[:, j * ns:(j + 1) * ns] = re
        xim_ref[:, j * ns:(j + 1) * ns] = im

    def tile(t, carry):
        cr, ci = carry
        off = pl.multiple_of(t * SUBLANES, SUBLANES)
        pr = pre_ref[...]
        pi = pim_ref[...]
        xr = xre_ref[pl.ds(off, SUBLANES), :] + pr * cr - pi * ci
        xi = xim_ref[pl.ds(off, SUBLANES), :] + pr * ci + pi * cr
        xre_ref[pl.ds(off, SUBLANES), :] = xr
        xim_ref[pl.ds(off, SUBLANES), :] = xi
        return (jnp.broadcast_to(xr[SUBLANES - 1:SUBLANES, :], xr.shape),
                jnp.broadcast_to(xi[SUBLANES - 1:SUBLANES, :], xi.shape))

    cr, ci = lax.fori_loop(0, nt, tile, (car_ref[...], cai_ref[...]))
    car_ref[...] = cr
    cai_ref[...] = ci
    hre_ref[...] = cr[0:1, :]
    him_ref[...] = ci[0:1, :]

    for j in range(S5_SLABS):
        y = (_dot(xre_ref[:, j * ns:(j + 1) * ns].astype(BF16), cre_ref[j])
             + _dot(xim_ref[:, j * ns:(j + 1) * ns].astype(BF16), cim_ref[j]))
        u = u_ref[:, j * LANES:(j + 1) * LANES]
        y_ref[:, j * LANES:(j + 1) * LANES] = _gelu(y + d_ref[:, j * LANES:(j + 1) * LANES] * u)


def _s5_seq(proj, lw, *, batch, seq, lc=256):
    lc = min(lc, seq)
    nc = seq // lc
    nst = S5_GROUPS * S5_STATE
    full = lambda shape: pl.BlockSpec(shape, lambda b, c: (0,) * len(shape))
    return pl.pallas_call(
        functools.partial(_s5_seq_body, lc=lc),
        grid=(batch, nc),
        in_specs=[
            pl.BlockSpec((lc, S5_WIDTH), lambda b, c: (b * nc + c, 0)),
            full((S5_SLABS, LANES, S5_SLAB_STATE)), full((S5_SLABS, LANES, S5_SLAB_STATE)),
            full((S5_SLABS, S5_SLAB_STATE, LANES)), full((S5_SLABS, S5_SLAB_STATE, LANES)),
            full((1, S5_WIDTH)),
            full((S5_SLABS, 3, SUBLANES, S5_SLAB_STATE)), full((S5_SLABS, 3, SUBLANES, S5_SLAB_STATE)),
            full((SUBLANES, nst)), full((SUBLANES, nst)),
        ],
        out_specs=[
            pl.BlockSpec((lc, S5_WIDTH), lambda b, c: (b * nc + c, 0)),
            pl.BlockSpec((None, 1, nst), lambda b, c: (b, 0, 0)),
            pl.BlockSpec((None, 1, nst), lambda b, c: (b, 0, 0)),
        ],
        out_shape=[
            jax.ShapeDtypeStruct((batch * seq, S5_WIDTH), F32),
            jax.ShapeDtypeStruct((batch, 1, nst), F32),
            jax.ShapeDtypeStruct((batch, 1, nst), F32),
        ],
        scratch_shapes=[pltpu.VMEM((lc, nst), F32), pltpu.VMEM((lc, nst), F32),
                        pltpu.VMEM((SUBLANES, nst), F32), pltpu.VMEM((SUBLANES, nst), F32)],
        compiler_params=_cparams("parallel", "arbitrary"),
        name="s5_seq",
    )(proj, lw["b_re"], lw["b_im"], lw["c_re"], lw["c_im"], lw["d"], lw["a_re_hs"], lw["a_im_hs"],
      lw["p_re"], lw["p_im"])


def _s5_step_body(u_ref, h0re_ref, h0im_ref, bre_ref, bim_ref, brel_ref, biml_ref, cre_ref, cim_ref, d_ref,
                  are_ref, aim_ref, y_ref, hre_ref, him_ref, *, nt):
    xr = h0re_ref[...]
    xi = h0im_ref[...]
    ar = are_ref[0]
    ai = aim_ref[0]
    for t in range(nt):
        u = u_ref[t]
        uh = u.astype(BF16)
        ul = (u - uh.astype(F32)).astype(BF16)
        bur = _dot(uh, bre_ref[0]) + _dot(ul, bre_ref[0]) + _dot(uh, brel_ref[0])
        bui = _dot(uh, bim_ref[0]) + _dot(ul, bim_ref[0]) + _dot(uh, biml_ref[0])
        xr, xi = ar * xr - ai * xi + bur, ar * xi + ai * xr + bui
        y = _dot(xr.astype(BF16), cre_ref[0]) + _dot(xi.astype(BF16), cim_ref[0])
        y_ref[t] = _gelu(y + d_ref[...] * u)
    hre_ref[...] = xr
    him_ref[...] = xi


def _s5_step(u_t, h0re, h0im, lw):
    nt, nb, _ = u_t.shape
    ns = S5_SLAB_STATE
    slab3 = lambda a, b: pl.BlockSpec((1, a, b), lambda j: (j, 0, 0))
    return pl.pallas_call(
        functools.partial(_s5_step_body, nt=nt),
        grid=(S5_SLABS,),
        in_specs=[
            pl.BlockSpec((nt, nb, LANES), lambda j: (0, 0, j)),
            pl.BlockSpec((nb, ns), lambda j: (0, j)), pl.BlockSpec((nb, ns), lambda j: (0, j)),
            slab3(LANES, ns), slab3(LANES, ns), slab3(LANES, ns), slab3(LANES, ns),
            slab3(ns, LANES), slab3(ns, LANES),
            pl.BlockSpec((1, LANES), lambda j: (0, j)),
            slab3(1, ns), slab3(1, ns),
        ],
        out_specs=[
            pl.BlockSpec((nt, nb, LANES), lambda j: (0, 0, j)),
            pl.BlockSpec((nb, ns), lambda j: (0, j)), pl.BlockSpec((nb, ns), lambda j: (0, j)),
        ],
        out_shape=[
            jax.ShapeDtypeStruct((nt, nb, S5_WIDTH), F32),
            jax.ShapeDtypeStruct((nb, S5_GROUPS * S5_STATE), F32),
            jax.ShapeDtypeStruct((nb, S5_GROUPS * S5_STATE), F32),
        ],
        compiler_params=_cparams("parallel"),
        name="s5_step",
    )(u_t, h0re, h0im, lw["b_re"], lw["b_im"], lw["b_re_lo"], lw["b_im_lo"], lw["c_re"], lw["c_im"], lw["d"],
      lw["a_re1"], lw["a_im1"])


def _odd_tail_body(y_ref, z_ref, h_ref, wg_ref, bg_ref, wo_ref, out_ref):
    y = y_ref[...]
    gl = _dot(y.astype(BF16), wg_ref[...]) + bg_ref[...]
    y = y * (1.0 / (1.0 + jnp.exp(-gl)))
    y = y * _silu(z_ref[...])
    out_ref[...] = h_ref[...] + _dot(y.astype(BF16), wo_ref[...])


def _odd_tail(y, proj, h, lw, tm=512):
    m = h.shape[0]
    tm = min(tm, m)
    row = lambda c: pl.BlockSpec((tm, D_MODEL), lambda i: (i, c))
    full = lambda shape: pl.BlockSpec(shape, lambda i: (0,) * len(shape))
    return pl.pallas_call(
        _odd_tail_body,
        grid=(m // tm,),
        in_specs=[row(0), row(1), row(0), full((S5_WIDTH, S5_WIDTH)), full((1, S5_WIDTH)),
                  full((S5_WIDTH, D_MODEL))],
        out_specs=row(0),
        out_shape=jax.ShapeDtypeStruct((m, D_MODEL), F32),
        compiler_params=_cparams("parallel"),
        name="odd_tail",
    )(y, proj, h, lw["w_glu"], lw["b_glu"], lw["w_out"])


def _rot_half_cols(w):
    half = QK_ROPE // 2
    return jnp.concatenate([-w[..., half:], w[..., :half]], axis=-1)


def _head_pad(x_nope, x_rope):
    z = jnp.zeros(x_nope.shape[:-1] + (LANES - QK_DIM,), x_nope.dtype)
    out = jnp.concatenate([x_nope, x_rope, z], axis=-1)
    return out.reshape(out.shape[:-2] + (MLA_HEADS * LANES,))


def _even_weights(w_in, conv_w, conv_b, dt_bias, a_log, d_ssm, ssm_norm_w, q_a_norm_w, w_qb, kv_a_norm_w, w_kvb,
                  q_norm_w, k_norm_w, w_out):
    k = w_in.shape[0]
    o = 0
    parts = {}
    for name, sz in (("z", SSM_D_INNER), ("xbc", SSM_CONV_CH), ("dt", SSM_HEADS), ("ql", Q_LORA), ("kvl", KV_LORA),
                     ("kpe", QK_ROPE), ("g", MLA_WIDTH)):
        parts[name] = w_in[:, o:o + sz]
        o += sz
    zc = lambda n: jnp.zeros((k, n), F32)
    kpe_blk = jnp.concatenate([zc(QK_NOPE), parts["kpe"], zc(LANES - QK_DIM)], axis=1)
    kpesw_blk = jnp.concatenate([zc(QK_NOPE), _rot_half_cols(parts["kpe"]), zc(LANES - QK_DIM)], axis=1)
    dt_blk = jnp.concatenate([parts["dt"], zc(LANES - SSM_HEADS)], axis=1)
    w_in_p = jnp.concatenate([parts["xbc"], parts["z"], parts["g"], parts["ql"], parts["kvl"], kpe_blk, kpesw_blk,
                              dt_blk, zc(N_EVEN_PAD - C_DT - LANES)], axis=1).astype(BF16)
    wq = w_qb.reshape(Q_LORA, MLA_HEADS, QK_DIM)
    zq = jnp.zeros((Q_LORA, MLA_HEADS, QK_NOPE), F32)
    wkv = w_kvb.reshape(KV_LORA, MLA_HEADS, QK_NOPE + V_DIM)
    w_nope = wkv[..., :QK_NOPE]
    pad1 = lambda v, n: jnp.concatenate([v, jnp.zeros((n - v.shape[0],), F32)]).reshape(1, n)
    norm_pat = lambda w: jnp.concatenate([w, jnp.zeros((LANES - QK_DIM,), F32)]).reshape(1, LANES)
    w_abs = jnp.concatenate([jnp.transpose(w_nope, (1, 2, 0)),
                             jnp.zeros((MLA_HEADS, LANES - QK_NOPE, KV_LORA), F32)], axis=1)
    e_pe = jnp.zeros((LANES, LANES), F32).at[QK_NOPE + jnp.arange(QK_ROPE), jnp.arange(QK_ROPE)].set(1.0)
    head_expand = (jnp.arange(LANES)[:, None] == (jnp.arange(SSM_D_INNER)[None, :] // SSM_HEAD_DIM))
    return {
        "w_in": w_in_p,
        "conv_w": conv_w, "conv_b": conv_b.reshape(1, -1),
        "dt_bias": pad1(dt_bias, LANES), "a_log": pad1(a_log, LANES),
        "d_exp": jnp.repeat(d_ssm, SSM_HEAD_DIM).reshape(1, -1),
        "ssm_norm_w": ssm_norm_w.reshape(1, -1),
        "head_expand": head_expand.astype(BF16),
        "q_a_norm_w": q_a_norm_w.reshape(1, -1),
        "w_q": _head_pad(wq[..., :QK_NOPE], wq[..., QK_NOPE:]).astype(BF16),
        "w_qsw": _head_pad(zq, _rot_half_cols(wq[..., QK_NOPE:])).astype(BF16),
        "q_norm_pat": norm_pat(q_norm_w),
        "kv_a_norm_w": kv_a_norm_w.reshape(1, -1),
        "w_k": _head_pad(w_nope, jnp.zeros((KV_LORA, MLA_HEADS, QK_ROPE), F32)).astype(BF16),
        "w_v": wkv[..., QK_NOPE:].reshape(KV_LORA, MLA_WIDTH).astype(BF16),
        "k_norm_pat": norm_pat(k_norm_w),
        "w_abs": w_abs.astype(BF16),
        "e_pe": e_pe.astype(BF16),
        "w_nope_t": jnp.transpose(w_nope, (1, 2, 0)).reshape(MLA_HEADS * QK_NOPE, KV_LORA).astype(BF16),
        "w_out": w_out.astype(BF16),
    }


def _rope_tables(pos):
    half = QK_ROPE // 2
    inv_freq = jnp.power(ROPE_THETA, -jnp.arange(half, dtype=F32) / half)
    ang = pos[:, None] * inv_freq[None, :]
    c, s = jnp.cos(ang), jnp.sin(ang)
    n = pos.shape[0]
    cos_t = jnp.concatenate([jnp.ones((n, QK_NOPE), F32), c, c, jnp.ones((n, LANES - QK_DIM), F32)], axis=1)
    sin_t = jnp.concatenate([jnp.zeros((n, QK_NOPE), F32), s, s, jnp.zeros((n, LANES - QK_DIM), F32)], axis=1)
    return cos_t, sin_t


def _odd_weights(w_in, a_re, a_im, b_re, b_im, c_re, c_im, d, log_step, w_glu, b_glu, w_out):
    ar, ai = a_re.astype(F32), a_im.astype(F32)
    step = jnp.exp(log_step.astype(F32))[:, None]
    mag = jnp.exp(ar * step)
    ab_re, ab_im = mag * jnp.cos(ai * step), mag * jnp.sin(ai * step)
    den = ar * ar + ai * ai
    nr, ni = ab_re - 1.0, ab_im
    f_re = (nr * ar + ni * ai) / den
    f_im = (ni * ar - nr * ai) / den
    bb_re = f_re[..., None] * b_re - f_im[..., None] * b_im
    bb_im = f_re[..., None] * b_im + f_im[..., None] * b_re
    gl = LANES // S5_GROUP_CH

    def b_blocks(bb):
        x = jnp.transpose(bb.reshape(S5_SLABS, gl, S5_STATE, S5_GROUP_CH), (0, 1, 3, 2))
        eye = jnp.eye(gl, dtype=F32)
        return jnp.einsum("sgcn,gh->sgchn", x, eye).reshape(S5_SLABS, LANES, S5_SLAB_STATE)

    def c_blocks(cc):
        x = jnp.transpose(cc.reshape(S5_SLABS, gl, S5_GROUP_CH, S5_STATE), (0, 1, 3, 2))
        eye = jnp.eye(gl, dtype=F32)
        return jnp.einsum("sgnc,gh->sgnhc", x, eye).reshape(S5_SLABS, S5_SLAB_STATE, LANES)

    def powers(n):
        pr, pi = jnp.ones_like(ab_re), jnp.zeros_like(ab_im)
        out = []
        for _ in range(n):
            pr, pi = pr * ab_re - pi * ab_im, pr * ab_im + pi * ab_re
            out.append((pr, pi))
        return out

    pw = powers(SUBLANES)
    slab = lambda x: x.reshape(S5_SLABS, S5_SLAB_STATE)
    rows = jnp.arange(SUBLANES)[None, :, None]
    hs_re = jnp.stack([jnp.where(rows >= sh, slab(pw[sh - 1][0])[:, None, :], 0.0) for sh in (1, 2, 4)], axis=1)
    hs_im = jnp.stack([jnp.where(rows >= sh, slab(pw[sh - 1][1])[:, None, :], 0.0) for sh in (1, 2, 4)], axis=1)
    p_re = jnp.stack([p[0].reshape(-1) for p in pw], axis=0)
    p_im = jnp.stack([p[1].reshape(-1) for p in pw], axis=0)
    bre_f, bim_f = b_blocks(bb_re), b_blocks(bb_im)
    bre16, bim16 = bre_f.astype(BF16), bim_f.astype(BF16)
    return {
        "w_in": w_in.astype(BF16),
        "b_re": bre16, "b_im": bim16,
        "b_re_lo": (bre_f - bre16.astype(F32)).astype(BF16), "b_im_lo": (bim_f - bim16.astype(F32)).astype(BF16),
        "c_re": c_blocks(c_re.astype(F32)).astype(BF16), "c_im": c_blocks(-c_im.astype(F32)).astype(BF16),
        "d": d.reshape(1, -1),
        "a_re_hs": hs_re, "a_im_hs": hs_im, "p_re": p_re, "p_im": p_im,
        "a_re1": slab(ab_re).reshape(S5_SLABS, 1, S5_SLAB_STATE),
        "a_im1": slab(ab_im).reshape(S5_SLABS, 1, S5_SLAB_STATE),
        "w_glu": w_glu.astype(BF16), "b_glu": b_glu.reshape(1, -1), "w_out": w_out.astype(BF16),
    }


SAMPLE_ROWS = 8


def _even_layer(hp, hs, norm_w, lw, tabs, state_ssm, state_conv, pool_c, pool_p, page_table, bp, sp, bs, ss):
    cos_p, sin_p, cos_s, sin_s = tabs
    hist = SSM_CONV - 1
    proj_p = _inproj(hp, norm_w, lw["w_in"])
    proj_s = _inproj(hs, norm_w, lw["w_in"])
    nchunks = sp // SSM_CHUNK
    zero_state = jnp.zeros((bp, SSM_D_INNER, SSM_STATE), F32)
    y_p, ssm_p = _ssd(proj_p, proj_p, proj_p, zero_state, lw, nblk=bp * nchunks, nchunks=nchunks, nseq=1,
                      P=SSM_CHUNK, n_hist=0, n_real=SSM_CHUNK, carry=True,
                      xbc_col=0, z_col=C_Z // SSM_D_INNER, dt_col=C_DT // LANES)
    conv_p = proj_p[:, C_XBC:C_XBC + SSM_CONV_CH].reshape(bp, sp, SSM_CONV_CH)[:, sp - hist:]
    xbc_s = proj_s[:, C_XBC:C_XBC + SSM_CONV_CH].reshape(bs, ss, SSM_CONV_CH)
    xbc_full = jnp.concatenate([state_conv, xbc_s], axis=1)
    pad_rows = SAMPLE_ROWS - hist - ss
    pad3 = lambda a: jnp.pad(a, ((0, 0), (hist, pad_rows), (0, 0)))
    xbc_pad = jnp.pad(xbc_full, ((0, 0), (0, pad_rows), (0, 0))).reshape(bs * SAMPLE_ROWS, SSM_CONV_CH)
    z_pad = pad3(proj_s[:, C_Z:C_Z + SSM_D_INNER].reshape(bs, ss, -1)).reshape(bs * SAMPLE_ROWS, -1)
    dt_pad = pad3(proj_s[:, C_DT:C_DT + LANES].reshape(bs, ss, -1)).reshape(bs * SAMPLE_ROWS, -1)
    nseq = SSM_CHUNK // SAMPLE_ROWS
    y_s_pad, ssm_s = _ssd(xbc_pad, z_pad, dt_pad, state_ssm.reshape(bs, SSM_D_INNER, SSM_STATE), lw,
                          nblk=bs // nseq, nchunks=1, nseq=nseq, P=SAMPLE_ROWS, n_hist=hist, n_real=ss,
                          carry=False, xbc_col=0, z_col=0, dt_col=0)
    y_s = y_s_pad.reshape(bs, SAMPLE_ROWS, SSM_D_INNER)[:, hist:hist + ss].reshape(bs * ss, SSM_D_INNER)
    conv_s = xbc_full[:, ss:]
    q_p, ckv_p, kper_p, k_p, v_p = _mla_proj(proj_p, cos_p, sin_p, lw, with_kv=True)
    q_s, ckv_s, kper_s = _mla_proj(proj_s, cos_s, sin_s, lw, with_kv=False)
    o_p = _flash(q_p, k_p, v_p, batch=bp, seq=sp)
    kpe_p = kper_p[:, QK_NOPE:QK_DIM]
    kpe_s = kper_s[:, QK_NOPE:QK_DIM]
    qabs, qpe = _qabs(q_s, lw)
    qabs = qabs.reshape(bs, ss * MLA_HEADS, KV_LORA)
    qpe = qpe.reshape(bs, ss * MLA_HEADS, LANES)
    cnew = jnp.pad(ckv_s.reshape(bs, ss, KV_LORA), ((0, 0), (0, PAGE - ss), (0, 0)))
    pnew = jnp.pad(kpe_s.reshape(bs, ss, QK_ROPE), ((0, 0), (0, PAGE - ss), (0, 0)))
    o_s = _decode_attn(page_table, pool_c, pool_p, cnew, pnew, qabs, qpe, lw).reshape(bs * ss, MLA_WIDTH)
    hp = _outproj_even(y_p, o_p, proj_p, hp, lw["w_out"])
    hs = _outproj_even(y_s, o_s, proj_s, hs, lw["w_out"])
    outs = (ckv_p.reshape(bp, sp, KV_LORA), kpe_p.reshape(bp, sp, QK_ROPE), ckv_s.reshape(bs, ss, KV_LORA),
            kpe_s.reshape(bs, ss, QK_ROPE), ssm_p.reshape(bp, SSM_HEADS, SSM_HEAD_DIM, SSM_STATE),
            ssm_s.reshape(bs, SSM_HEADS, SSM_HEAD_DIM, SSM_STATE), conv_p, conv_s)
    return hp, hs, outs


def _odd_layer(hp, hs, norm_w, lw, s5_re, s5_im, bp, sp, bs, ss):
    proj_p = _inproj(hp, norm_w, lw["w_in"])
    proj_s = _inproj(hs, norm_w, lw["w_in"])
    y_p, r_p, i_p = _s5_seq(proj_p, lw, batch=bp, seq=sp)
    u_t = jnp.transpose(proj_s[:, :S5_WIDTH].reshape(bs, ss, S5_WIDTH), (1, 0, 2))
    y_t, r_s, i_s = _s5_step(u_t, s5_re.reshape(bs, -1), s5_im.reshape(bs, -1), lw)
    y_s = jnp.transpose(y_t, (1, 0, 2)).reshape(bs * ss, S5_WIDTH)
    hp = _odd_tail(y_p, proj_p, hp, lw)
    hs = _odd_tail(y_s, proj_s, hs, lw)
    shp = lambda a, b: a.reshape(b, S5_GROUPS, S5_STATE)
    return hp, hs, (shp(r_p, bp), shp(i_p, bp), shp(r_s, bs), shp(i_s, bs))


def kernel(x_prompt, x_sample, cache_ckv, cache_kpe, page_table, state_ssm, state_conv, state_s5_re, state_s5_im, norm_w, w_in_even, conv_w, conv_b, dt_bias, a_log, d_ssm, ssm_norm_w, q_a_norm_w, w_qb, kv_a_norm_w, w_kvb, q_norm_w, k_norm_w, w_out_even, w_in_odd, s5_a_re, s5_a_im, s5_b_re, s5_b_im, s5_c_re, s5_c_im, s5_d, s5_log_step, w_glu, b_glu, w_out_odd):
    bp, sp, _ = x_prompt.shape
    bs, ss, _ = x_sample.shape
    past_len = page_table.shape[1] * PAGE
    depth = norm_w.shape[0]
    pos_p = jnp.tile(jnp.arange(sp, dtype=F32), bp)
    pos_s = jnp.tile(past_len + jnp.arange(ss, dtype=F32), bs)
    tabs = _rope_tables(pos_p) + _rope_tables(pos_s)
    hp = x_prompt.reshape(bp * sp, D_MODEL)
    hs = x_sample.reshape(bs * ss, D_MODEL)
    even_out, odd_out = [], []
    for i in range(depth):
        j = i // 2
        if i % 2 == 0:
            lw = _even_weights(w_in_even[j], conv_w[j], conv_b[j], dt_bias[j], a_log[j], d_ssm[j], ssm_norm_w[j],
                               q_a_norm_w[j], w_qb[j], kv_a_norm_w[j], w_kvb[j], q_norm_w[j], k_norm_w[j],
                               w_out_even[j])
            hp, hs, o = _even_layer(hp, hs, norm_w[i], lw, tabs, state_ssm[j], state_conv[j], cache_ckv[j],
                                    cache_kpe[j], page_table, bp, sp, bs, ss)
            even_out.append(o)
        else:
            lw = _odd_weights(w_in_odd[j], s5_a_re[j], s5_a_im[j], s5_b_re[j], s5_b_im[j], s5_c_re[j], s5_c_im[j],
                              s5_d[j], s5_log_step[j], w_glu[j], b_glu[j], w_out_odd[j])
            hp, hs, o = _odd_layer(hp, hs, norm_w[i], lw, state_s5_re[j], state_s5_im[j], bp, sp, bs, ss)
            odd_out.append(o)
    ev = [jnp.stack([o[k] for o in even_out]) for k in range(8)]
    od = [jnp.stack([o[k] for o in odd_out]) for k in range(4)]
    return (hp.reshape(bp, sp, D_MODEL), hs.reshape(bs, ss, D_MODEL),
            ev[0], ev[1], ev[2], ev[3], ev[4], ev[5], ev[6], ev[7], od[0], od[1], od[2], od[3])
```

```python
import functools
import math

import jax
import jax.numpy as jnp
from jax import lax
from jax.experimental import pallas as pl
from jax.experimental.pallas import tpu as pltpu

F32 = jnp.float32
BF16 = jnp.bfloat16
EPS = 1e-6

D_MODEL = 1024
LANES = 128
SUBLANES = 8
VMEM_LIMIT = 48 * 1024 * 1024

SSM_D_INNER = 1024
SSM_HEAD_DIM = 64
SSM_HEADS = 16
SSM_GROUPS = 4
SSM_STATE = 128
SSM_CONV = 4
SSM_CHUNK = 128
SSM_GN = SSM_GROUPS * SSM_STATE
SSM_CONV_CH = SSM_D_INNER + 2 * SSM_GN

MLA_HEADS = 16
Q_LORA = 256
KV_LORA = 256
QK_NOPE = 64
QK_ROPE = 32
QK_DIM = QK_NOPE + QK_ROPE
V_DIM = 64
MLA_WIDTH = MLA_HEADS * V_DIM
ROPE_THETA = 10000.0
PAGE = 128
FLASH_TK = 512

S5_WIDTH = 1024
S5_GROUP_CH = 16
S5_GROUPS = 64
S5_STATE = 64
S5_SLABS = S5_WIDTH // LANES
S5_SLAB_STATE = (LANES // S5_GROUP_CH) * S5_STATE

C_XBC, C_Z, C_G, C_QL, C_KVL, C_KPE, C_KPESW, C_DT, N_EVEN_PAD = 0, 2048, 3072, 4096, 4352, 4608, 4736, 4864, 5120


def _cparams(*sem):
    return pltpu.CompilerParams(dimension_semantics=sem, vmem_limit_bytes=VMEM_LIMIT)


def _silu(x):
    return x * (1.0 / (1.0 + jnp.exp(-x)))


def _dot(a, b):
    return jnp.dot(a, b, preferred_element_type=F32)


def _dot_nt(a, b):
    return lax.dot_general(a, b, (((1,), (1,)), ((), ())), preferred_element_type=F32)


def _split3(v):
    hi = v.astype(BF16)
    r1 = v - hi.astype(F32)
    mid = r1.astype(BF16)
    lo = (r1 - mid.astype(F32)).astype(BF16)
    return hi, mid, lo


def _dot_exact_lhs(m_bf16, v):
    hi, mid, lo = _split3(v)
    return _dot(m_bf16, hi) + _dot(m_bf16, mid) + _dot(m_bf16, lo)


def _dot_exact_rhs(v, m_bf16):
    hi, mid, lo = _split3(v)
    return _dot(hi, m_bf16) + _dot(mid, m_bf16) + _dot(lo, m_bf16)


def _inproj_body(x_ref, nw_ref, w_ref, o_ref, xn_ref):
    @pl.when(pl.program_id(1) == 0)
    def _():
        x = x_ref[...]
        ms = jnp.mean(x * x, axis=-1, keepdims=True)
        xn_ref[...] = (x * lax.rsqrt(ms + EPS) * nw_ref[...]).astype(BF16)

    o_ref[...] = _dot(xn_ref[...], w_ref[...])


def _inproj(x, norm_w, w, tm=512, tn=1024):
    m, k = x.shape
    n = w.shape[1]
    tm = min(tm, m)
    return pl.pallas_call(
        _inproj_body,
        grid=(m // tm, n // tn),
        in_specs=[
            pl.BlockSpec((tm, k), lambda i, j: (i, 0)),
            pl.BlockSpec((1, k), lambda i, j: (0, 0)),
            pl.BlockSpec((k, tn), lambda i, j: (0, j)),
        ],
        out_specs=pl.BlockSpec((tm, tn), lambda i, j: (i, j)),
        out_shape=jax.ShapeDtypeStruct((m, n), F32),
        scratch_shapes=[pltpu.VMEM((tm, k), BF16)],
        compiler_params=_cparams("parallel", "arbitrary"),
        name="inproj",
    )(x, norm_w.reshape(1, k), w)


def _ssd_body(xbc_ref, z_ref, dt_ref, h0_ref, convw_ref, convb_ref, dtb_ref, alog_ref, dexp_ref, nw_ref, e_ref,
              y_ref, hout_ref,
              xp_ref, ysc_ref, xwt_ref, b_ref, c_ref, cse_ref, tott_ref,
              *, L, P, n_hist, n_real, nseq, nchunks, carry):
    blk = pl.program_id(0)
    s = pl.program_id(1)
    chunk = blk % nchunks
    HIST = SUBLANES

    @pl.when(s == 0)
    def _intra():
        if carry:
            @pl.when(chunk == 0)
            def _():
                xp_ref[0:HIST, :] = jnp.zeros((HIST, SSM_CONV_CH), F32)
        else:
            xp_ref[0:HIST, :] = jnp.zeros((HIST, SSM_CONV_CH), F32)
        xp_ref[HIST:HIST + L, :] = xbc_ref[...]
        conv = convb_ref[...] + convw_ref[3:4, :] * xp_ref[HIST:HIST + L, :]
        for k in range(SSM_CONV - 1):
            off = HIST - (SSM_CONV - 1) + k
            conv = conv + convw_ref[k:k + 1, :] * xp_ref[off:off + L, :]
        if carry:
            xp_ref[0:HIST, :] = xp_ref[L:L + HIST, :]
        xc = _silu(conv)
        xs = xc[:, :SSM_D_INNER]
        b_ref[...] = xc[:, SSM_D_INNER:SSM_D_INNER + SSM_GN].astype(BF16)
        c_ref[...] = xc[:, SSM_D_INNER + SSM_GN:].astype(BF16)

        raw = dt_ref[...] + dtb_ref[...]
        dt = jnp.maximum(raw, 0.0) + jnp.log1p(jnp.exp(-jnp.abs(raw)))
        ri = lax.broadcasted_iota(jnp.int32, (L, L), 0)
        ci = lax.broadcasted_iota(jnp.int32, (L, L), 1)
        if P < L:
            rp = lax.broadcasted_iota(jnp.int32, (L, LANES), 0) % P
            dt = jnp.where((rp >= n_hist) & (rp < n_hist + n_real), dt, 0.0)
            same = (ri // P) == (ci // P)
            causal = same & (ci <= ri)
        else:
            same = ri >= 0
            causal = ci <= ri
        a_neg = -jnp.exp(alog_ref[...])
        da = dt * a_neg
        m_cum = jnp.where(causal, 1.0, 0.0).astype(BF16)
        m_tot = jnp.where(same, 1.0, 0.0).astype(BF16)
        cs = _dot_exact_lhs(m_cum, da)
        tot = _dot_exact_lhs(m_tot, da)
        cst = cs.T
        e = e_ref[...]
        dt_e = _dot_exact_rhs(dt, e)
        cs_e = _dot_exact_rhs(cs, e)
        tot_e = _dot_exact_rhs(tot, e)
        cse_ref[...] = cs_e
        tott_ref[...] = tot_e.T
        xdt = xs * dt_e
        xw = xdt * jnp.exp(tot_e - cs_e)
        xwt_ref[...] = xw.T.astype(BF16)
        xdt16 = xdt.astype(BF16)
        lane = lax.broadcasted_iota(jnp.int32, (L, LANES), 1)
        for g in range(SSM_GROUPS):
            cb = _dot_nt(c_ref[:, g * SSM_STATE:(g + 1) * SSM_STATE], b_ref[:, g * SSM_STATE:(g + 1) * SSM_STATE])
            for pr in range(2):
                h0i = 4 * g + 2 * pr
                col = h0i // 2
                xpair = xdt16[:, col * LANES:(col + 1) * LANES]
                ys = []
                for hh in (h0i, h0i + 1):
                    dec = jnp.exp(jnp.where(causal, cs[:, hh:hh + 1] - cst[hh:hh + 1, :], -1e30))
                    ys.append(_dot((cb * dec).astype(BF16), xpair))
                ypair = jnp.where(lane < SSM_HEAD_DIM, ys[0], ys[1])
                ysc_ref[:, col * LANES:(col + 1) * LANES] = (
                    ypair + dexp_ref[:, col * LANES:(col + 1) * LANES] * xs[:, col * LANES:(col + 1) * LANES])

    if carry:
        @pl.when(chunk == 0)
        def _():
            hout_ref[...] = h0_ref[...]
    else:
        hout_ref[...] = h0_ref[...]

    if nseq > 1:
        rmask = (lax.broadcasted_iota(jnp.int32, (L, LANES), 0) // P) == s
        cmask = (lax.broadcasted_iota(jnp.int32, (LANES, L), 1) // P) == s
        onehot = lax.broadcasted_iota(jnp.int32, (LANES, L), 1) == s * P
    for col in range(SSM_HEADS // 2):
        g = col // 2
        sp = hout_ref[col * LANES:(col + 1) * LANES, :]
        yoff = _dot_nt(c_ref[:, g * SSM_STATE:(g + 1) * SSM_STATE], sp.astype(BF16))
        yoff = yoff * jnp.exp(cse_ref[:, col * LANES:(col + 1) * LANES])
        xwt = xwt_ref[col * LANES:(col + 1) * LANES, :]
        tott = tott_ref[col * LANES:(col + 1) * LANES, :]
        if nseq > 1:
            yoff = jnp.where(rmask, yoff, 0.0)
            xwt = jnp.where(cmask, xwt, jnp.zeros_like(xwt))
            deccol = jnp.exp(jnp.sum(jnp.where(onehot, tott, 0.0), axis=1, keepdims=True))
        else:
            deccol = jnp.exp(tott[:, 0:1])
        ysc_ref[:, col * LANES:(col + 1) * LANES] += yoff
        hout_ref[col * LANES:(col + 1) * LANES, :] = sp * deccol + _dot(xwt, b_ref[:, g * SSM_STATE:(g + 1) * SSM_STATE])

    @pl.when(s == nseq - 1)
    def _epilogue():
        gw = SSM_D_INNER // SSM_GROUPS
        for g in range(SSM_GROUPS):
            y = ysc_ref[:, g * gw:(g + 1) * gw] * _silu(z_ref[:, g * gw:(g + 1) * gw])
            ms = jnp.mean(y * y, axis=-1, keepdims=True)
            y_ref[:, g * gw:(g + 1) * gw] = y * lax.rsqrt(ms + EPS) * nw_ref[:, g * gw:(g + 1) * gw]


def _ssd(xbc_src, z_src, dt_src, h0, lw, *, nblk, nchunks, nseq, P, n_hist, n_real, carry,
         xbc_col, z_col, dt_col):
    L = SSM_CHUNK
    nstate = h0.shape[0]
    if carry:
        state_idx = lambda b, s: (b // nchunks, 0, 0)
    else:
        state_idx = lambda b, s: (b * nseq + s, 0, 0)
    full = lambda shape: pl.BlockSpec(shape, lambda b, s: (0,) * len(shape))
    body = functools.partial(_ssd_body, L=L, P=P, n_hist=n_hist, n_real=n_real, nseq=nseq, nchunks=nchunks,
                             carry=carry)
    return pl.pallas_call(
        body,
        grid=(nblk, nseq),
        in_specs=[
            pl.BlockSpec((L, SSM_CONV_CH), lambda b, s: (b, xbc_col)),
            pl.BlockSpec((L, SSM_D_INNER), lambda b, s: (b, z_col)),
            pl.BlockSpec((L, LANES), lambda b, s: (b, dt_col)),
            pl.BlockSpec((None, SSM_D_INNER, SSM_STATE), state_idx),
            full((SSM_CONV, SSM_CONV_CH)),
            full((1, SSM_CONV_CH)),
            full((1, LANES)),
            full((1, LANES)),
            full((1, SSM_D_INNER)),
            full((1, SSM_D_INNER)),
            full((LANES, SSM_D_INNER)),
        ],
        out_specs=[
            pl.BlockSpec((L, SSM_D_INNER), lambda b, s: (b, 0)),
            pl.BlockSpec((None, SSM_D_INNER, SSM_STATE), state_idx),
        ],
        out_shape=[
            jax.ShapeDtypeStruct((nblk * L, SSM_D_INNER), F32),
            jax.ShapeDtypeStruct((nstate, SSM_D_INNER, SSM_STATE), F32),
        ],
        scratch_shapes=[
            pltpu.VMEM((L + SUBLANES, SSM_CONV_CH), F32),
            pltpu.VMEM((L, SSM_D_INNER), F32),
            pltpu.VMEM((SSM_D_INNER, L), BF16),
            pltpu.VMEM((L, SSM_GN), BF16),
            pltpu.VMEM((L, SSM_GN), BF16),
            pltpu.VMEM((L, SSM_D_INNER), F32),
            pltpu.VMEM((SSM_D_INNER, L), F32),
        ],
        compiler_params=_cparams("arbitrary", "arbitrary"),
        name="ssd",
    )(xbc_src, z_src, dt_src, h0, lw["conv_w"], lw["conv_b"], lw["dt_bias"], lw["a_log"], lw["d_exp"],
      lw["ssm_norm_w"], lw["head_expand"])


def _mla_proj_body(ql_ref, kvl_ref, kpe_ref, kpesw_ref, cos_ref, sin_ref, qanw_ref, wq_ref, wqsw_ref, qnw_ref,
                   kvanw_ref, wk_ref, wv_ref, knw_ref,
                   q_ref, ckv_ref, kper_ref, *kv_refs, with_kv):
    cos = cos_ref[...]
    sin = sin_ref[...]
    ql = ql_ref[...]
    qn = (ql * lax.rsqrt(jnp.mean(ql * ql, axis=-1, keepdims=True) + EPS) * qanw_ref[...]).astype(BF16)
    q0 = _dot(qn, wq_ref[...])
    q1 = _dot(qn, wqsw_ref[...])
    scale = QK_DIM ** -0.5
    for h in range(MLA_HEADS):
        sl = slice(h * LANES, (h + 1) * LANES)
        qh = q0[:, sl] * cos + q1[:, sl] * sin
        ms = jnp.sum(qh * qh, axis=-1, keepdims=True) * (1.0 / QK_DIM)
        q_ref[:, sl] = (qh * lax.rsqrt(ms + EPS) * (qnw_ref[...] * scale)).astype(BF16)
    kvl = kvl_ref[...]
    ckv = kvl * lax.rsqrt(jnp.mean(kvl * kvl, axis=-1, keepdims=True) + EPS) * kvanw_ref[...]
    ckv_ref[...] = ckv
    kper = kpe_ref[...] * cos + kpesw_ref[...] * sin
    kper_ref[...] = kper
    if with_kv:
        k_ref, v_ref = kv_refs
        c16 = ckv.astype(BF16)
        kn = _dot(c16, wk_ref[...])
        for h in range(MLA_HEADS):
            sl = slice(h * LANES, (h + 1) * LANES)
            kh = kn[:, sl] + kper
            ms = jnp.sum(kh * kh, axis=-1, keepdims=True) * (1.0 / QK_DIM)
            k_ref[:, sl] = (kh * lax.rsqrt(ms + EPS) * knw_ref[...]).astype(BF16)
        v_ref[...] = _dot_nt(wv_ref[...], c16).astype(BF16)


def _mla_proj(proj, cos_t, sin_t, lw, *, with_kv, tm=256):
    m = proj.shape[0]
    tm = min(tm, m)
    hp = MLA_HEADS * LANES
    full = lambda shape: pl.BlockSpec(shape, lambda i: (0,) * len(shape))
    row = lambda w, c: pl.BlockSpec((tm, w), lambda i: (i, c))
    out_specs = [row(hp, 0), row(KV_LORA, 0), row(LANES, 0)]
    out_shape = [jax.ShapeDtypeStruct((m, hp), BF16), jax.ShapeDtypeStruct((m, KV_LORA), F32),
                 jax.ShapeDtypeStruct((m, LANES), F32)]
    if with_kv:
        out_specs += [row(hp, 0), pl.BlockSpec((None, MLA_WIDTH, tm), lambda i: (i, 0, 0))]
        out_shape += [jax.ShapeDtypeStruct((m, hp), BF16), jax.ShapeDtypeStruct((m // tm, MLA_WIDTH, tm), BF16)]
    return pl.pallas_call(
        functools.partial(_mla_proj_body, with_kv=with_kv),
        grid=(m // tm,),
        in_specs=[
            row(Q_LORA, C_QL // Q_LORA), row(KV_LORA, C_KVL // KV_LORA), row(LANES, C_KPE // LANES),
            row(LANES, C_KPESW // LANES), row(LANES, 0), row(LANES, 0),
            full((1, Q_LORA)), full((Q_LORA, hp)), full((Q_LORA, hp)), full((1, LANES)),
            full((1, KV_LORA)), full((KV_LORA, hp)), full((MLA_WIDTH, KV_LORA)), full((1, LANES)),
        ],
        out_specs=out_specs,
        out_shape=out_shape,
        compiler_params=_cparams("parallel"),
        name="mla_proj",
    )(proj, proj, proj, proj, cos_t, sin_t, lw["q_a_norm_w"], lw["w_q"], lw["w_qsw"], lw["q_norm_pat"],
      lw["kv_a_norm_w"], lw["w_k"], lw["w_v_t"], lw["k_norm_pat"])


def _flash_body(q_ref, k_ref, vt_ref, o_ref, m_ref, l_ref, acc_ref, *, tq, tk):
    qi = pl.program_id(2)
    nfull = qi * (tq // tk)
    m_ref[...] = jnp.full(m_ref.shape, -jnp.inf, F32)
    l_ref[...] = jnp.zeros(l_ref.shape, F32)
    acc_ref[...] = jnp.zeros(acc_ref.shape, F32)

    def step(ki, masked):
        koff = pl.multiple_of(ki * tk, tk)
        vt = vt_ref[ki]
        if masked:
            kpos = ki * tk + lax.broadcasted_iota(jnp.int32, (tk, tq), 0)
            qpos = qi * tq + lax.broadcasted_iota(jnp.int32, (tk, tq), 1)
            visible = kpos <= qpos
        for j in range(2):
            q = q_ref[:, j * LANES:(j + 1) * LANES]
            k = k_ref[pl.ds(koff, tk), j * LANES:(j + 1) * LANES]
            st = _dot_nt(k, q)
            if masked:
                st = jnp.where(visible, st, -jnp.inf)
            m_old = m_ref[j]
            m_new = jnp.maximum(m_old, jnp.max(st, axis=0, keepdims=True))
            p = jnp.exp(st - m_new)
            alpha = jnp.exp(m_old - m_new)
            l_ref[j] = alpha * l_ref[j] + jnp.sum(p.reshape(tk // SUBLANES, SUBLANES, tq), axis=0)
            acc_ref[j] = alpha * acc_ref[j] + _dot(vt[j * V_DIM:(j + 1) * V_DIM, :], p.astype(BF16))
            m_ref[j] = m_new

    def full_step(ki, c):
        step(ki, False)
        return c

    lax.fori_loop(0, nfull, full_step, 0)
    for d in range(tq // tk):
        step(nfull + d, True)
    ot = jnp.concatenate([acc_ref[j] / jnp.sum(l_ref[j], axis=0, keepdims=True) for j in range(2)], axis=0)
    o_ref[...] = ot.T


def _flash(q, k, vt, *, batch, seq, tq=1024):
    tk = vt.shape[2]
    tq = min(tq, seq)
    nq = seq // tq
    nk = seq // tk
    return pl.pallas_call(
        functools.partial(_flash_body, tq=tq, tk=tk),
        grid=(batch, MLA_HEADS // 2, nq),
        in_specs=[
            pl.BlockSpec((tq, 2 * LANES), lambda b, h, i: (b * nq + i, h)),
            pl.BlockSpec((seq, 2 * LANES), lambda b, h, i: (b, h)),
            pl.BlockSpec((nk, 2 * V_DIM, tk), lambda b, h, i: (b, h, 0)),
        ],
        out_specs=pl.BlockSpec((tq, 2 * V_DIM), lambda b, h, i: (b * nq + i, h)),
        out_shape=jax.ShapeDtypeStruct((batch * seq, MLA_WIDTH), F32),
        scratch_shapes=[pltpu.VMEM((2, 1, tq), F32), pltpu.VMEM((2, SUBLANES, tq), F32),
                        pltpu.VMEM((2, V_DIM, tq), F32)],
        compiler_params=_cparams("parallel", "parallel", "arbitrary"),
        name="flash",
    )(q, k, vt)


def _qabs_body(q_ref, knw_ref, wabs_ref, epe_ref, qabs_ref, qpe_ref):
    qh = (q_ref[...].astype(F32) * knw_ref[...]).astype(BF16)
    qabs_ref[...] = _dot(qh, wabs_ref[...]).astype(BF16)
    qpe_ref[...] = _dot(qh, epe_ref[...]).astype(BF16)


def _qabs(q, lw):
    m = q.shape[0]
    return pl.pallas_call(
        _qabs_body,
        grid=(MLA_HEADS,),
        in_specs=[
            pl.BlockSpec((m, LANES), lambda h: (0, h)),
            pl.BlockSpec((1, LANES), lambda h: (0, 0)),
            pl.BlockSpec((None, LANES, KV_LORA), lambda h: (h, 0, 0)),
            pl.BlockSpec((LANES, LANES), lambda h: (0, 0)),
        ],
        out_specs=[pl.BlockSpec((m, KV_LORA), lambda h: (0, h)), pl.BlockSpec((m, LANES), lambda h: (0, h))],
        out_shape=[jax.ShapeDtypeStruct((m, MLA_HEADS * KV_LORA), BF16),
                   jax.ShapeDtypeStruct((m, MLA_HEADS * LANES), BF16)],
        compiler_params=_cparams("parallel"),
        name="qabs",
    )(q, lw["k_norm_pat"], lw["w_abs"], lw["e_pe"])


def _decode_body(pt_ref, *refs, pg, ngroups, nq):
    del pt_ref
    c_refs = refs[:pg]
    p_refs = refs[pg:2 * pg]
    (cnew_ref, pnew_ref, wt_ref, qabs_ref, qpe_ref, wv_ref, o_ref,
     m_ref, l_ref, acc_ref, kpad_ref, lhs_ref) = refs[2 * pg:]
    g = pl.program_id(1)
    rows = nq * MLA_HEADS
    nk = MLA_HEADS * QK_NOPE

    @pl.when(g == 0)
    def _():
        m_ref[...] = jnp.full((rows, 1), -jnp.inf, F32)
        l_ref[...] = jnp.zeros((rows, 1), F32)
        acc_ref[...] = jnp.zeros((rows, KV_LORA), F32)
        kpad_ref[...] = jnp.zeros(kpad_ref.shape, F32)
        lhs_ref[0:nk, :] = wt_ref[...]
        lhs_ref[nk:nk + rows, :] = qabs_ref[...]

    ones = jnp.ones((SUBLANES, LANES), BF16)

    def scores(c, kp):
        t = c.shape[0]
        c16 = c.astype(BF16)
        both = _dot_nt(lhs_ref[...], c16)
        kt = both[0:nk]
        ssq = jnp.sum((kt * kt).reshape(MLA_HEADS, QK_NOPE, t), axis=1)
        kp2 = kp * kp
        kp2h = kp2.astype(BF16)
        kp2l = (kp2 - kp2h.astype(F32)).astype(BF16)
        ssq_pe = (_dot_nt(ones, kp2h) + _dot_nt(ones, kp2l))[0:1, :]
        r = lax.rsqrt((ssq + ssq_pe) * (1.0 / QK_DIM) + EPS)
        st = both[nk:nk + rows] + _dot_nt(qpe_ref[...], kp.astype(BF16))
        return (st.reshape(nq, MLA_HEADS, t) * r[None]).reshape(rows, t), c16

    def update(st, c16s):
        m_old = m_ref[...]
        m_new = jnp.maximum(m_old, jnp.max(st, axis=-1, keepdims=True))
        p = jnp.exp(st - m_new)
        alpha = jnp.exp(m_old - m_new)
        l_ref[...] = alpha * l_ref[...] + jnp.sum(p, axis=-1, keepdims=True)
        p16 = p.astype(BF16)
        pv = None
        off = 0
        for c16 in c16s:
            t = c16.shape[0]
            d = _dot(p16[:, off:off + t], c16)
            pv = d if pv is None else pv + d
            off += t
        acc_ref[...] = alpha * acc_ref[...] + pv
        m_ref[...] = m_new

    @pl.when(g < ngroups)
    def _():
        sts, c16s = [], []
        for k in range(0, pg, 2):
            r0 = k * PAGE
            kpad_ref[r0:r0 + PAGE, 0:QK_ROPE] = p_refs[k][...]
            kpad_ref[r0 + PAGE:r0 + 2 * PAGE, 0:QK_ROPE] = p_refs[k + 1][...]
            c = jnp.concatenate([c_refs[k][...], c_refs[k + 1][...]], axis=0)
            st, c16 = scores(c, kpad_ref[r0:r0 + 2 * PAGE, :])
            sts.append(st)
            c16s.append(c16)
        update(jnp.concatenate(sts, axis=1), c16s)

    @pl.when(g == ngroups)
    def _():
        kpad_ref[0:PAGE, 0:QK_ROPE] = pnew_ref[...]
        qrow = lax.broadcasted_iota(jnp.int32, (rows, PAGE), 0) // MLA_HEADS
        tok = lax.broadcasted_iota(jnp.int32, (rows, PAGE), 1)
        st, c16 = scores(cnew_ref[...], kpad_ref[0:PAGE, :])
        update(jnp.where(tok <= qrow, st, -jnp.inf), [c16])
        olat = (acc_ref[...] / l_ref[...]).astype(BF16)
        of = _dot(olat, wv_ref[...])
        rh = lax.broadcasted_iota(jnp.int32, (rows, MLA_WIDTH), 0) % MLA_HEADS
        ch = lax.broadcasted_iota(jnp.int32, (rows, MLA_WIDTH), 1) // V_DIM
        of = jnp.where(rh == ch, of, 0.0)
        o_ref[...] = jnp.sum(of.reshape(nq, MLA_HEADS, MLA_WIDTH), axis=1)


def _decode_attn(page_table, pool_c, pool_p, layer, cnew, pnew, qabs, qpe, lw, *, pg=16):
    nb, npages = page_table.shape
    nq = qabs.shape[1] // MLA_HEADS
    rows = nq * MLA_HEADS
    pg = min(pg, npages)
    ngroups = npages // pg
    last = ngroups - 1

    def page_spec(width, k):
        return pl.BlockSpec((None, None, PAGE, width),
                            lambda b, g, pt: (layer, pt[b, jnp.minimum(g, last) * pg + k], 0, 0))

    per_seq = lambda shape: pl.BlockSpec((None,) + shape, lambda b, g, pt: (b, 0, 0))
    full = lambda shape: pl.BlockSpec(shape, lambda b, g, pt: (0,) * len(shape))
    in_specs = ([page_spec(KV_LORA, k) for k in range(pg)] + [page_spec(QK_ROPE, k) for k in range(pg)]
                + [per_seq((PAGE, KV_LORA)), per_seq((PAGE, QK_ROPE)), full((MLA_HEADS * QK_NOPE, KV_LORA)),
                   per_seq((rows, KV_LORA)), per_seq((rows, LANES)), full((KV_LORA, MLA_WIDTH))])
    grid_spec = pltpu.PrefetchScalarGridSpec(
        num_scalar_prefetch=1,
        grid=(nb, ngroups + 1),
        in_specs=in_specs,
        out_specs=per_seq((nq, MLA_WIDTH)),
        scratch_shapes=[pltpu.VMEM((rows, 1), F32), pltpu.VMEM((rows, 1), F32), pltpu.VMEM((rows, KV_LORA), F32),
                        pltpu.VMEM((pg * PAGE, LANES), F32),
                        pltpu.VMEM((MLA_HEADS * QK_NOPE + rows, KV_LORA), BF16)],
    )
    return pl.pallas_call(
        functools.partial(_decode_body, pg=pg, ngroups=ngroups, nq=nq),
        grid_spec=grid_spec,
        out_shape=jax.ShapeDtypeStruct((nb, nq, MLA_WIDTH), F32),
        compiler_params=_cparams("parallel", "arbitrary"),
        name="decode_attn",
    )(page_table, *([pool_c] * pg), *([pool_p] * pg), cnew, pnew, lw["w_nope_t"], qabs, qpe, lw["w_v"])


def _outproj_even_body(y_ref, o_ref, g_ref, h_ref, wy_ref, wo_ref, out_ref):
    og = (o_ref[...] * _silu(g_ref[...])).astype(BF16)
    out_ref[...] = h_ref[...] + _dot(y_ref[...].astype(BF16), wy_ref[...]) + _dot(og, wo_ref[...])


def _outproj_even(y, o, proj, h, w_out, tm=512):
    m = h.shape[0]
    tm = min(tm, m)
    row = lambda c: pl.BlockSpec((tm, D_MODEL), lambda i: (i, c))
    return pl.pallas_call(
        _outproj_even_body,
        grid=(m // tm,),
        in_specs=[row(0), row(0), row(C_G // D_MODEL), row(0),
                  pl.BlockSpec((SSM_D_INNER, D_MODEL), lambda i: (0, 0)),
                  pl.BlockSpec((MLA_WIDTH, D_MODEL), lambda i: (1, 0))],
        out_specs=row(0),
        out_shape=jax.ShapeDtypeStruct((m, D_MODEL), F32),
        compiler_params=_cparams("parallel"),
        name="outproj_even",
    )(y, o, proj, h, w_out, w_out)


def _gelu(x):
    return 0.5 * x * (1.0 + jnp.tanh(math.sqrt(2.0 / math.pi) * (x + 0.044715 * (x * x * x))))


def _s5_seq_body(u_ref, bre_ref, bim_ref, cre_ref, cim_ref, d_ref, are_ref, aim_ref, pre_ref, pim_ref,
                 y_ref, hre_ref, him_ref, xre_ref, xim_ref, car_ref, cai_ref, *, lc):
    c = pl.program_id(1)
    nt = lc // SUBLANES
    ns = S5_SLAB_STATE

    @pl.when(c == 0)
    def _():
        car_ref[...] = jnp.zeros(car_ref.shape, F32)
        cai_ref[...] = jnp.zeros(cai_ref.shape, F32)

    for j in range(S5_SLABS):
        uj = u_ref[:, j * LANES:(j + 1) * LANES].astype(BF16)
        re = _dot(uj, bre_ref[j])
        im = _dot(uj, bim_ref[j])
        for si in range(3):
            sh = 1 << si
            sre = pltpu.roll(re, sh, 0).reshape(nt, SUBLANES, ns)
            sim = pltpu.roll(im, sh, 0).reshape(nt, SUBLANES, ns)
            ar = are_ref[j, si][None]
            ai = aim_ref[j, si][None]
            re3 = re.reshape(nt, SUBLANES, ns) + ar * sre - ai * sim
            im3 = im.reshape(nt, SUBLANES, ns) + ar * sim + ai * sre
            re = re3.reshape(lc, ns)
            im = im3.reshape(lc, ns)
        xre_ref[:, j * ns:(j + 1) * ns] = re
        xim_ref[:, j * ns:(j + 1) * ns] = im

    def tile(t, carry):
        cr, ci = carry
        off = pl.multiple_of(t * SUBLANES, SUBLANES)
        pr = pre_ref[...]
        pi = pim_ref[...]
        xr = xre_ref[pl.ds(off, SUBLANES), :] + pr * cr - pi * ci
        xi = xim_ref[pl.ds(off, SUBLANES), :] + pr * ci + pi * cr
        xre_ref[pl.ds(off, SUBLANES), :] = xr
        xim_ref[pl.ds(off, SUBLANES), :] = xi
        return (jnp.broadcast_to(xr[SUBLANES - 1:SUBLANES, :], xr.shape),
                jnp.broadcast_to(xi[SUBLANES - 1:SUBLANES, :], xi.shape))

    cr, ci = lax.fori_loop(0, nt, tile, (car_ref[...], cai_ref[...]))
    car_ref[...] = cr
    cai_ref[...] = ci
    hre_ref[...] = cr[0:1, :]
    him_ref[...] = ci[0:1, :]

    for j in range(S5_SLABS):
        y = (_dot(xre_ref[:, j * ns:(j + 1) * ns].astype(BF16), cre_ref[j])
             + _dot(xim_ref[:, j * ns:(j + 1) * ns].astype(BF16), cim_ref[j]))
        u = u_ref[:, j * LANES:(j + 1) * LANES]
        y_ref[:, j * LANES:(j + 1) * LANES] = _gelu(y + d_ref[:, j * LANES:(j + 1) * LANES] * u)


def _s5_seq(proj, lw, *, batch, seq, lc=256):
    lc = min(lc, seq)
    nc = seq // lc
    nst = S5_GROUPS * S5_STATE
    full = lambda shape: pl.BlockSpec(shape, lambda b, c: (0,) * len(shape))
    return pl.pallas_call(
        functools.partial(_s5_seq_body, lc=lc),
        grid=(batch, nc),
        in_specs=[
            pl.BlockSpec((lc, S5_WIDTH), lambda b, c: (b * nc + c, 0)),
            full((S5_SLABS, LANES, S5_SLAB_STATE)), full((S5_SLABS, LANES, S5_SLAB_STATE)),
            full((S5_SLABS, S5_SLAB_STATE, LANES)), full((S5_SLABS, S5_SLAB_STATE, LANES)),
            full((1, S5_WIDTH)),
            full((S5_SLABS, 3, SUBLANES, S5_SLAB_STATE)), full((S5_SLABS, 3, SUBLANES, S5_SLAB_STATE)),
            full((SUBLANES, nst)), full((SUBLANES, nst)),
        ],
        out_specs=[
            pl.BlockSpec((lc, S5_WIDTH), lambda b, c: (b * nc + c, 0)),
            pl.BlockSpec((None, 1, nst), lambda b, c: (b, 0, 0)),
            pl.BlockSpec((None, 1, nst), lambda b, c: (b, 0, 0)),
        ],
        out_shape=[
            jax.ShapeDtypeStruct((batch * seq, S5_WIDTH), F32),
            jax.ShapeDtypeStruct((batch, 1, nst), F32),
            jax.ShapeDtypeStruct((batch, 1, nst), F32),
        ],
        scratch_shapes=[pltpu.VMEM((lc, nst), F32), pltpu.VMEM((lc, nst), F32),
                        pltpu.VMEM((SUBLANES, nst), F32), pltpu.VMEM((SUBLANES, nst), F32)],
        compiler_params=_cparams("parallel", "arbitrary"),
        name="s5_seq",
    )(proj, lw["b_re"], lw["b_im"], lw["c_re"], lw["c_im"], lw["d"], lw["a_re_hs"], lw["a_im_hs"],
      lw["p_re"], lw["p_im"])


def _s5_step_body(u_ref, h0re_ref, h0im_ref, bre_ref, bim_ref, brel_ref, biml_ref, cre_ref, cim_ref, d_ref,
                  are_ref, aim_ref, y_ref, hre_ref, him_ref, *, nt):
    xr = h0re_ref[...]
    xi = h0im_ref[...]
    ar = are_ref[0]
    ai = aim_ref[0]
    for t in range(nt):
        u = u_ref[t]
        uh = u.astype(BF16)
        ul = (u - uh.astype(F32)).astype(BF16)
        bur = _dot(uh, bre_ref[0]) + _dot(ul, bre_ref[0]) + _dot(uh, brel_ref[0])
        bui = _dot(uh, bim_ref[0]) + _dot(ul, bim_ref[0]) + _dot(uh, biml_ref[0])
        xr, xi = ar * xr - ai * xi + bur, ar * xi + ai * xr + bui
        y = _dot(xr.astype(BF16), cre_ref[0]) + _dot(xi.astype(BF16), cim_ref[0])
        y_ref[t] = _gelu(y + d_ref[...] * u)
    hre_ref[...] = xr
    him_ref[...] = xi


def _s5_step(u_t, h0re, h0im, lw):
    nt, nb, _ = u_t.shape
    ns = S5_SLAB_STATE
    slab3 = lambda a, b: pl.BlockSpec((1, a, b), lambda j: (j, 0, 0))
    return pl.pallas_call(
        functools.partial(_s5_step_body, nt=nt),
        grid=(S5_SLABS,),
        in_specs=[
            pl.BlockSpec((nt, nb, LANES), lambda j: (0, 0, j)),
            pl.BlockSpec((nb, ns), lambda j: (0, j)), pl.BlockSpec((nb, ns), lambda j: (0, j)),
            slab3(LANES, ns), slab3(LANES, ns), slab3(LANES, ns), slab3(LANES, ns),
            slab3(ns, LANES), slab3(ns, LANES),
            pl.BlockSpec((1, LANES), lambda j: (0, j)),
            slab3(1, ns), slab3(1, ns),
        ],
        out_specs=[
            pl.BlockSpec((nt, nb, LANES), lambda j: (0, 0, j)),
            pl.BlockSpec((nb, ns), lambda j: (0, j)), pl.BlockSpec((nb, ns), lambda j: (0, j)),
        ],
        out_shape=[
            jax.ShapeDtypeStruct((nt, nb, S5_WIDTH), F32),
            jax.ShapeDtypeStruct((nb, S5_GROUPS * S5_STATE), F32),
            jax.ShapeDtypeStruct((nb, S5_GROUPS * S5_STATE), F32),
        ],
        compiler_params=_cparams("parallel"),
        name="s5_step",
    )(u_t, h0re, h0im, lw["b_re"], lw["b_im"], lw["b_re_lo"], lw["b_im_lo"], lw["c_re"], lw["c_im"], lw["d"],
      lw["a_re1"], lw["a_im1"])


def _odd_tail_body(y_ref, z_ref, h_ref, wg_ref, bg_ref, wo_ref, out_ref):
    y = y_ref[...]
    gl = _dot(y.astype(BF16), wg_ref[...]) + bg_ref[...]
    y = y * (1.0 / (1.0 + jnp.exp(-gl)))
    y = y * _silu(z_ref[...])
    out_ref[...] = h_ref[...] + _dot(y.astype(BF16), wo_ref[...])


def _odd_tail(y, proj, h, lw, tm=512):
    m = h.shape[0]
    tm = min(tm, m)
    row = lambda c: pl.BlockSpec((tm, D_MODEL), lambda i: (i, c))
    full = lambda shape: pl.BlockSpec(shape, lambda i: (0,) * len(shape))
    return pl.pallas_call(
        _odd_tail_body,
        grid=(m // tm,),
        in_specs=[row(0), row(1), row(0), full((S5_WIDTH, S5_WIDTH)), full((1, S5_WIDTH)),
                  full((S5_WIDTH, D_MODEL))],
        out_specs=row(0),
        out_shape=jax.ShapeDtypeStruct((m, D_MODEL), F32),
        compiler_params=_cparams("parallel"),
        name="odd_tail",
    )(y, proj, h, lw["w_glu"], lw["b_glu"], lw["w_out"])


def _rot_half_cols(w):
    half = QK_ROPE // 2
    return jnp.concatenate([-w[..., half:], w[..., :half]], axis=-1)


def _head_pad(x_nope, x_rope):
    z = jnp.zeros(x_nope.shape[:-1] + (LANES - QK_DIM,), x_nope.dtype)
    out = jnp.concatenate([x_nope, x_rope, z], axis=-1)
    return out.reshape(out.shape[:-2] + (MLA_HEADS * LANES,))


def _even_weights(w_in, conv_w, conv_b, dt_bias, a_log, d_ssm, ssm_norm_w, q_a_norm_w, w_qb, kv_a_norm_w, w_kvb,
                  q_norm_w, k_norm_w, w_out):
    k = w_in.shape[0]
    o = 0
    parts = {}
    for name, sz in (("z", SSM_D_INNER), ("xbc", SSM_CONV_CH), ("dt", SSM_HEADS), ("ql", Q_LORA), ("kvl", KV_LORA),
                     ("kpe", QK_ROPE), ("g", MLA_WIDTH)):
        parts[name] = w_in[:, o:o + sz]
        o += sz
    zc = lambda n: jnp.zeros((k, n), F32)
    kpe_blk = jnp.concatenate([zc(QK_NOPE), parts["kpe"], zc(LANES - QK_DIM)], axis=1)
    kpesw_blk = jnp.concatenate([zc(QK_NOPE), _rot_half_cols(parts["kpe"]), zc(LANES - QK_DIM)], axis=1)
    dt_blk = jnp.concatenate([parts["dt"], zc(LANES - SSM_HEADS)], axis=1)
    w_in_p = jnp.concatenate([parts["xbc"], parts["z"], parts["g"], parts["ql"], parts["kvl"], kpe_blk, kpesw_blk,
                              dt_blk, zc(N_EVEN_PAD - C_DT - LANES)], axis=1).astype(BF16)
    wq = w_qb.reshape(Q_LORA, MLA_HEADS, QK_DIM)
    zq = jnp.zeros((Q_LORA, MLA_HEADS, QK_NOPE), F32)
    wkv = w_kvb.reshape(KV_LORA, MLA_HEADS, QK_NOPE + V_DIM)
    w_nope = wkv[..., :QK_NOPE]
    pad1 = lambda v, n: jnp.concatenate([v, jnp.zeros((n - v.shape[0],), F32)]).reshape(1, n)
    norm_pat = lambda w: jnp.concatenate([w, jnp.zeros((LANES - QK_DIM,), F32)]).reshape(1, LANES)
    w_abs = jnp.concatenate([jnp.transpose(w_nope, (1, 2, 0)),
                             jnp.zeros((MLA_HEADS, LANES - QK_NOPE, KV_LORA), F32)], axis=1)
    e_pe = jnp.zeros((LANES, LANES), F32).at[QK_NOPE + jnp.arange(QK_ROPE), jnp.arange(QK_ROPE)].set(1.0)
    head_expand = (jnp.arange(LANES)[:, None] == (jnp.arange(SSM_D_INNER)[None, :] // SSM_HEAD_DIM))
    return {
        "w_in": w_in_p,
        "conv_w": conv_w, "conv_b": conv_b.reshape(1, -1),
        "dt_bias": pad1(dt_bias, LANES), "a_log": pad1(a_log, LANES),
        "d_exp": jnp.repeat(d_ssm, SSM_HEAD_DIM).reshape(1, -1),
        "ssm_norm_w": ssm_norm_w.reshape(1, -1),
        "head_expand": head_expand.astype(BF16),
        "q_a_norm_w": q_a_norm_w.reshape(1, -1),
        "w_q": _head_pad(wq[..., :QK_NOPE], wq[..., QK_NOPE:]).astype(BF16),
        "w_qsw": _head_pad(zq, _rot_half_cols(wq[..., QK_NOPE:])).astype(BF16),
        "q_norm_pat": norm_pat(q_norm_w),
        "kv_a_norm_w": kv_a_norm_w.reshape(1, -1),
        "w_k": _head_pad(w_nope, jnp.zeros((KV_LORA, MLA_HEADS, QK_ROPE), F32)).astype(BF16),
        "w_v": wkv[..., QK_NOPE:].reshape(KV_LORA, MLA_WIDTH).astype(BF16),
        "w_v_t": wkv[..., QK_NOPE:].reshape(KV_LORA, MLA_WIDTH).T.astype(BF16),
        "k_norm_pat": norm_pat(k_norm_w),
        "w_abs": w_abs.astype(BF16),
        "e_pe": e_pe.astype(BF16),
        "w_nope_t": jnp.transpose(w_nope, (1, 2, 0)).reshape(MLA_HEADS * QK_NOPE, KV_LORA).astype(BF16),
        "w_out": w_out.astype(BF16),
    }


def _rope_tables(pos):
    half = QK_ROPE // 2
    inv_freq = jnp.power(ROPE_THETA, -jnp.arange(half, dtype=F32) / half)
    ang = pos[:, None] * inv_freq[None, :]
    c, s = jnp.cos(ang), jnp.sin(ang)
    n = pos.shape[0]
    cos_t = jnp.concatenate([jnp.ones((n, QK_NOPE), F32), c, c, jnp.ones((n, LANES - QK_DIM), F32)], axis=1)
    sin_t = jnp.concatenate([jnp.zeros((n, QK_NOPE), F32), s, s, jnp.zeros((n, LANES - QK_DIM), F32)], axis=1)
    return cos_t, sin_t


def _odd_weights(w_in, a_re, a_im, b_re, b_im, c_re, c_im, d, log_step, w_glu, b_glu, w_out):
    ar, ai = a_re.astype(F32), a_im.astype(F32)
    step = jnp.exp(log_step.astype(F32))[:, None]
    mag = jnp.exp(ar * step)
    ab_re, ab_im = mag * jnp.cos(ai * step), mag * jnp.sin(ai * step)
    den = ar * ar + ai * ai
    nr, ni = ab_re - 1.0, ab_im
    f_re = (nr * ar + ni * ai) / den
    f_im = (ni * ar - nr * ai) / den
    bb_re = f_re[..., None] * b_re - f_im[..., None] * b_im
    bb_im = f_re[..., None] * b_im + f_im[..., None] * b_re
    gl = LANES // S5_GROUP_CH

    def b_blocks(bb):
        x = jnp.transpose(bb.reshape(S5_SLABS, gl, S5_STATE, S5_GROUP_CH), (0, 1, 3, 2))
        eye = jnp.eye(gl, dtype=F32)
        return jnp.einsum("sgcn,gh->sgchn", x, eye).reshape(S5_SLABS, LANES, S5_SLAB_STATE)

    def c_blocks(cc):
        x = jnp.transpose(cc.reshape(S5_SLABS, gl, S5_GROUP_CH, S5_STATE), (0, 1, 3, 2))
        eye = jnp.eye(gl, dtype=F32)
        return jnp.einsum("sgnc,gh->sgnhc", x, eye).reshape(S5_SLABS, S5_SLAB_STATE, LANES)

    def powers(n):
        pr, pi = jnp.ones_like(ab_re), jnp.zeros_like(ab_im)
        out = []
        for _ in range(n):
            pr, pi = pr * ab_re - pi * ab_im, pr * ab_im + pi * ab_re
            out.append((pr, pi))
        return out

    pw = powers(SUBLANES)
    slab = lambda x: x.reshape(S5_SLABS, S5_SLAB_STATE)
    rows = jnp.arange(SUBLANES)[None, :, None]
    hs_re = jnp.stack([jnp.where(rows >= sh, slab(pw[sh - 1][0])[:, None, :], 0.0) for sh in (1, 2, 4)], axis=1)
    hs_im = jnp.stack([jnp.where(rows >= sh, slab(pw[sh - 1][1])[:, None, :], 0.0) for sh in (1, 2, 4)], axis=1)
    p_re = jnp.stack([p[0].reshape(-1) for p in pw], axis=0)
    p_im = jnp.stack([p[1].reshape(-1) for p in pw], axis=0)
    bre_f, bim_f = b_blocks(bb_re), b_blocks(bb_im)
    bre16, bim16 = bre_f.astype(BF16), bim_f.astype(BF16)
    return {
        "w_in": w_in.astype(BF16),
        "b_re": bre16, "b_im": bim16,
        "b_re_lo": (bre_f - bre16.astype(F32)).astype(BF16), "b_im_lo": (bim_f - bim16.astype(F32)).astype(BF16),
        "c_re": c_blocks(c_re.astype(F32)).astype(BF16), "c_im": c_blocks(-c_im.astype(F32)).astype(BF16),
        "d": d.reshape(1, -1),
        "a_re_hs": hs_re, "a_im_hs": hs_im, "p_re": p_re, "p_im": p_im,
        "a_re1": slab(ab_re).reshape(S5_SLABS, 1, S5_SLAB_STATE),
        "a_im1": slab(ab_im).reshape(S5_SLABS, 1, S5_SLAB_STATE),
        "w_glu": w_glu.astype(BF16), "b_glu": b_glu.reshape(1, -1), "w_out": w_out.astype(BF16),
    }


SAMPLE_ROWS = 8


def _even_layer(hp, hs, norm_w, lw, tabs, state_ssm, state_conv, pool_c, pool_p, layer, page_table, bp, sp, bs, ss):
    cos_p, sin_p, cos_s, sin_s = tabs
    hist = SSM_CONV - 1
    proj_p = _inproj(hp, norm_w, lw["w_in"])
    proj_s = _inproj(hs, norm_w, lw["w_in"])
    nchunks = sp // SSM_CHUNK
    zero_state = jnp.zeros((bp, SSM_D_INNER, SSM_STATE), F32)
    y_p, ssm_p = _ssd(proj_p, proj_p, proj_p, zero_state, lw, nblk=bp * nchunks, nchunks=nchunks, nseq=1,
                      P=SSM_CHUNK, n_hist=0, n_real=SSM_CHUNK, carry=True,
                      xbc_col=0, z_col=C_Z // SSM_D_INNER, dt_col=C_DT // LANES)
    conv_p = proj_p[:, C_XBC:C_XBC + SSM_CONV_CH].reshape(bp, sp, SSM_CONV_CH)[:, sp - hist:]
    xbc_s = proj_s[:, C_XBC:C_XBC + SSM_CONV_CH].reshape(bs, ss, SSM_CONV_CH)
    xbc_full = jnp.concatenate([state_conv, xbc_s], axis=1)
    pad_rows = SAMPLE_ROWS - hist - ss
    pad3 = lambda a: jnp.pad(a, ((0, 0), (hist, pad_rows), (0, 0)))
    xbc_pad = jnp.pad(xbc_full, ((0, 0), (0, pad_rows), (0, 0))).reshape(bs * SAMPLE_ROWS, SSM_CONV_CH)
    z_pad = pad3(proj_s[:, C_Z:C_Z + SSM_D_INNER].reshape(bs, ss, -1)).reshape(bs * SAMPLE_ROWS, -1)
    dt_pad = pad3(proj_s[:, C_DT:C_DT + LANES].reshape(bs, ss, -1)).reshape(bs * SAMPLE_ROWS, -1)
    nseq = SSM_CHUNK // SAMPLE_ROWS
    y_s_pad, ssm_s = _ssd(xbc_pad, z_pad, dt_pad, state_ssm.reshape(bs, SSM_D_INNER, SSM_STATE), lw,
                          nblk=bs // nseq, nchunks=1, nseq=nseq, P=SAMPLE_ROWS, n_hist=hist, n_real=ss,
                          carry=False, xbc_col=0, z_col=0, dt_col=0)
    y_s = y_s_pad.reshape(bs, SAMPLE_ROWS, SSM_D_INNER)[:, hist:hist + ss].reshape(bs * ss, SSM_D_INNER)
    conv_s = xbc_full[:, ss:]
    q_p, ckv_p, kper_p, k_p, v_p = _mla_proj(proj_p, cos_p, sin_p, lw, with_kv=True, tm=FLASH_TK)
    q_s, ckv_s, kper_s = _mla_proj(proj_s, cos_s, sin_s, lw, with_kv=False)
    o_p = _flash(q_p, k_p, v_p, batch=bp, seq=sp)
    kpe_p = kper_p[:, QK_NOPE:QK_DIM]
    kpe_s = kper_s[:, QK_NOPE:QK_DIM]
    qabs, qpe = _qabs(q_s, lw)
    qabs = qabs.reshape(bs, ss * MLA_HEADS, KV_LORA)
    qpe = qpe.reshape(bs, ss * MLA_HEADS, LANES)
    cnew = jnp.pad(ckv_s.reshape(bs, ss, KV_LORA), ((0, 0), (0, PAGE - ss), (0, 0)))
    pnew = jnp.pad(kpe_s.reshape(bs, ss, QK_ROPE), ((0, 0), (0, PAGE - ss), (0, 0)))
    o_s = _decode_attn(page_table, pool_c, pool_p, layer, cnew, pnew, qabs, qpe, lw).reshape(bs * ss, MLA_WIDTH)
    hp = _outproj_even(y_p, o_p, proj_p, hp, lw["w_out"])
    hs = _outproj_even(y_s, o_s, proj_s, hs, lw["w_out"])
    outs = (ckv_p.reshape(bp, sp, KV_LORA), kpe_p.reshape(bp, sp, QK_ROPE), ckv_s.reshape(bs, ss, KV_LORA),
            kpe_s.reshape(bs, ss, QK_ROPE), ssm_p.reshape(bp, SSM_HEADS, SSM_HEAD_DIM, SSM_STATE),
            ssm_s.reshape(bs, SSM_HEADS, SSM_HEAD_DIM, SSM_STATE), conv_p, conv_s)
    return hp, hs, outs


def _odd_layer(hp, hs, norm_w, lw, s5_re, s5_im, bp, sp, bs, ss):
    proj_p = _inproj(hp, norm_w, lw["w_in"])
    proj_s = _inproj(hs, norm_w, lw["w_in"])
    y_p, r_p, i_p = _s5_seq(proj_p, lw, batch=bp, seq=sp)
    u_t = jnp.transpose(proj_s[:, :S5_WIDTH].reshape(bs, ss, S5_WIDTH), (1, 0, 2))
    y_t, r_s, i_s = _s5_step(u_t, s5_re.reshape(bs, -1), s5_im.reshape(bs, -1), lw)
    y_s = jnp.transpose(y_t, (1, 0, 2)).reshape(bs * ss, S5_WIDTH)
    hp = _odd_tail(y_p, proj_p, hp, lw)
    hs = _odd_tail(y_s, proj_s, hs, lw)
    shp = lambda a, b: a.reshape(b, S5_GROUPS, S5_STATE)
    return hp, hs, (shp(r_p, bp), shp(i_p, bp), shp(r_s, bs), shp(i_s, bs))


def kernel(x_prompt, x_sample, cache_ckv, cache_kpe, page_table, state_ssm, state_conv, state_s5_re, state_s5_im, norm_w, w_in_even, conv_w, conv_b, dt_bias, a_log, d_ssm, ssm_norm_w, q_a_norm_w, w_qb, kv_a_norm_w, w_kvb, q_norm_w, k_norm_w, w_out_even, w_in_odd, s5_a_re, s5_a_im, s5_b_re, s5_b_im, s5_c_re, s5_c_im, s5_d, s5_log_step, w_glu, b_glu, w_out_odd):
    bp, sp, _ = x_prompt.shape
    bs, ss, _ = x_sample.shape
    past_len = page_table.shape[1] * PAGE
    depth = norm_w.shape[0]
    pos_p = jnp.tile(jnp.arange(sp, dtype=F32), bp)
    pos_s = jnp.tile(past_len + jnp.arange(ss, dtype=F32), bs)
    tabs = _rope_tables(pos_p) + _rope_tables(pos_s)
    hp = x_prompt.reshape(bp * sp, D_MODEL)
    hs = x_sample.reshape(bs * ss, D_MODEL)
    even_out, odd_out = [], []
    for i in range(depth):
        j = i // 2
        if i % 2 == 0:
            lw = _even_weights(w_in_even[j], conv_w[j], conv_b[j], dt_bias[j], a_log[j], d_ssm[j], ssm_norm_w[j],
                               q_a_norm_w[j], w_qb[j], kv_a_norm_w[j], w_kvb[j], q_norm_w[j], k_norm_w[j],
                               w_out_even[j])
            hp, hs, o = _even_layer(hp, hs, norm_w[i], lw, tabs, state_ssm[j], state_conv[j], cache_ckv,
                                    cache_kpe, j, page_table, bp, sp, bs, ss)
            even_out.append(o)
        else:
            lw = _odd_weights(w_in_odd[j], s5_a_re[j], s5_a_im[j], s5_b_re[j], s5_b_im[j], s5_c_re[j], s5_c_im[j],
                              s5_d[j], s5_log_step[j], w_glu[j], b_glu[j], w_out_odd[j])
            hp, hs, o = _odd_layer(hp, hs, norm_w[i], lw, state_s5_re[j], state_s5_im[j], bp, sp, bs, ss)
            odd_out.append(o)
    ev = [jnp.stack([o[k] for o in even_out]) for k in range(8)]
    od = [jnp.stack([o[k] for o in odd_out]) for k in range(4)]
    return (hp.reshape(bp, sp, D_MODEL), hs.reshape(bs, ss, D_MODEL),
            ev[0], ev[1], ev[2], ev[3], ev[4], ev[5], ev[6], ev[7], od[0], od[1], od[2], od[3])
```

```python
import functools
import math

import jax
import jax.numpy as jnp
from jax import lax
from jax.experimental import pallas as pl
from jax.experimental.pallas import tpu as pltpu

F32 = jnp.float32
BF16 = jnp.bfloat16
EPS = 1e-6

D_MODEL = 1024
LANES = 128
SUBLANES = 8
VMEM_LIMIT = 48 * 1024 * 1024

SSM_D_INNER = 1024
SSM_HEAD_DIM = 64
SSM_HEADS = 16
SSM_GROUPS = 4
SSM_STATE = 128
SSM_CONV = 4
SSM_CHUNK = 128
SSM_GN = SSM_GROUPS * SSM_STATE
SSM_CONV_CH = SSM_D_INNER + 2 * SSM_GN

MLA_HEADS = 16
Q_LORA = 256
KV_LORA = 256
QK_NOPE = 64
QK_ROPE = 32
QK_DIM = QK_NOPE + QK_ROPE
V_DIM = 64
MLA_WIDTH = MLA_HEADS * V_DIM
ROPE_THETA = 10000.0
PAGE = 128
FLASH_TK = 512
FLASH_TN = 512
FLASH_DEPTH = 3

S5_WIDTH = 1024
S5_GROUP_CH = 16
S5_GROUPS = 64
S5_STATE = 64
S5_SLABS = S5_WIDTH // LANES
S5_SLAB_STATE = (LANES // S5_GROUP_CH) * S5_STATE

C_XBC, C_Z, C_G, C_QL, C_KVL, C_KPE, C_KPESW, C_DT, N_EVEN_PAD = 0, 2048, 3072, 4096, 4352, 4608, 4736, 4864, 5120


def _cparams(*sem):
    return pltpu.CompilerParams(dimension_semantics=sem, vmem_limit_bytes=VMEM_LIMIT)


def _silu(x):
    return x * (1.0 / (1.0 + jnp.exp(-x)))


def _dot(a, b):
    return jnp.dot(a, b, preferred_element_type=F32)


def _dot_nt(a, b):
    return lax.dot_general(a, b, (((1,), (1,)), ((), ())), preferred_element_type=F32)


def _split3(v):
    hi = v.astype(BF16)
    r1 = v - hi.astype(F32)
    mid = r1.astype(BF16)
    lo = (r1 - mid.astype(F32)).astype(BF16)
    return hi, mid, lo


def _dot_exact_lhs(m_bf16, v):
    hi, mid, lo = _split3(v)
    return _dot(m_bf16, hi) + _dot(m_bf16, mid) + _dot(m_bf16, lo)


def _dot_exact_rhs(v, m_bf16):
    hi, mid, lo = _split3(v)
    return _dot(hi, m_bf16) + _dot(mid, m_bf16) + _dot(lo, m_bf16)


def _inproj_body(x_ref, nw_ref, w_ref, o_ref, xn_ref):
    @pl.when(pl.program_id(1) == 0)
    def _():
        x = x_ref[...]
        ms = jnp.mean(x * x, axis=-1, keepdims=True)
        xn_ref[...] = (x * lax.rsqrt(ms + EPS) * nw_ref[...]).astype(BF16)

    o_ref[...] = _dot(xn_ref[...], w_ref[...])


def _inproj(x, norm_w, w, tm=512, tn=1024):
    m, k = x.shape
    n = w.shape[1]
    tm = min(tm, m)
    return pl.pallas_call(
        _inproj_body,
        grid=(m // tm, n // tn),
        in_specs=[
            pl.BlockSpec((tm, k), lambda i, j: (i, 0)),
            pl.BlockSpec((1, k), lambda i, j: (0, 0)),
            pl.BlockSpec((k, tn), lambda i, j: (0, j)),
        ],
        out_specs=pl.BlockSpec((tm, tn), lambda i, j: (i, j)),
        out_shape=jax.ShapeDtypeStruct((m, n), F32),
        scratch_shapes=[pltpu.VMEM((tm, k), BF16)],
        compiler_params=_cparams("parallel", "arbitrary"),
        name="inproj",
    )(x, norm_w.reshape(1, k), w)


def _ssd_body(xbc_ref, z_ref, dt_ref, h0_ref, convw_ref, convb_ref, dtb_ref, alog_ref, dexp_ref, nw_ref, e_ref,
              y_ref, hout_ref,
              xp_ref, ysc_ref, xwt_ref, b_ref, c_ref, cse_ref, tott_ref,
              *, L, P, n_hist, n_real, nseq, nchunks, carry):
    blk = pl.program_id(0)
    s = pl.program_id(1)
    chunk = blk % nchunks
    HIST = SUBLANES

    @pl.when(s == 0)
    def _intra():
        if carry:
            @pl.when(chunk == 0)
            def _():
                xp_ref[0:HIST, :] = jnp.zeros((HIST, SSM_CONV_CH), F32)
        else:
            xp_ref[0:HIST, :] = jnp.zeros((HIST, SSM_CONV_CH), F32)
        xp_ref[HIST:HIST + L, :] = xbc_ref[...]
        conv = convb_ref[...] + convw_ref[3:4, :] * xp_ref[HIST:HIST + L, :]
        for k in range(SSM_CONV - 1):
            off = HIST - (SSM_CONV - 1) + k
            conv = conv + convw_ref[k:k + 1, :] * xp_ref[off:off + L, :]
        if carry:
            xp_ref[0:HIST, :] = xp_ref[L:L + HIST, :]
        xc = _silu(conv)
        xs = xc[:, :SSM_D_INNER]
        b_ref[...] = xc[:, SSM_D_INNER:SSM_D_INNER + SSM_GN].astype(BF16)
        c_ref[...] = xc[:, SSM_D_INNER + SSM_GN:].astype(BF16)

        raw = dt_ref[...] + dtb_ref[...]
        dt = jnp.maximum(raw, 0.0) + jnp.log1p(jnp.exp(-jnp.abs(raw)))
        ri = lax.broadcasted_iota(jnp.int32, (L, L), 0)
        ci = lax.broadcasted_iota(jnp.int32, (L, L), 1)
        if P < L:
            rp = lax.broadcasted_iota(jnp.int32, (L, LANES), 0) % P
            dt = jnp.where((rp >= n_hist) & (rp < n_hist + n_real), dt, 0.0)
            same = (ri // P) == (ci // P)
            causal = same & (ci <= ri)
        else:
            same = ri >= 0
            causal = ci <= ri
        a_neg = -jnp.exp(alog_ref[...])
        da = dt * a_neg
        m_cum = jnp.where(causal, 1.0, 0.0).astype(BF16)
        m_tot = jnp.where(same, 1.0, 0.0).astype(BF16)
        cs = _dot_exact_lhs(m_cum, da)
        tot = _dot_exact_lhs(m_tot, da)
        cst = cs.T
        e = e_ref[...]
        dt_e = _dot_exact_rhs(dt, e)
        cs_e = _dot_exact_rhs(cs, e)
        tot_e = _dot_exact_rhs(tot, e)
        cse_ref[...] = cs_e
        tott_ref[...] = tot_e.T
        xdt = xs * dt_e
        xw = xdt * jnp.exp(tot_e - cs_e)
        xwt_ref[...] = xw.T.astype(BF16)
        xdt16 = xdt.astype(BF16)
        lane = lax.broadcasted_iota(jnp.int32, (L, LANES), 1)
        for g in range(SSM_GROUPS):
            cb = _dot_nt(c_ref[:, g * SSM_STATE:(g + 1) * SSM_STATE], b_ref[:, g * SSM_STATE:(g + 1) * SSM_STATE])
            for pr in range(2):
                h0i = 4 * g + 2 * pr
                col = h0i // 2
                xpair = xdt16[:, col * LANES:(col + 1) * LANES]
                ys = []
                for hh in (h0i, h0i + 1):
                    dec = jnp.exp(jnp.where(causal, cs[:, hh:hh + 1] - cst[hh:hh + 1, :], -1e30))
                    ys.append(_dot((cb * dec).astype(BF16), xpair))
                ypair = jnp.where(lane < SSM_HEAD_DIM, ys[0], ys[1])
                ysc_ref[:, col * LANES:(col + 1) * LANES] = (
                    ypair + dexp_ref[:, col * LANES:(col + 1) * LANES] * xs[:, col * LANES:(col + 1) * LANES])

    if carry:
        @pl.when(chunk == 0)
        def _():
            hout_ref[...] = h0_ref[...]
    else:
        hout_ref[...] = h0_ref[...]

    if nseq > 1:
        rmask = (lax.broadcasted_iota(jnp.int32, (L, LANES), 0) // P) == s
        cmask = (lax.broadcasted_iota(jnp.int32, (LANES, L), 1) // P) == s
        onehot = lax.broadcasted_iota(jnp.int32, (LANES, L), 1) == s * P
    for col in range(SSM_HEADS // 2):
        g = col // 2
        sp = hout_ref[col * LANES:(col + 1) * LANES, :]
        yoff = _dot_nt(c_ref[:, g * SSM_STATE:(g + 1) * SSM_STATE], sp.astype(BF16))
        yoff = yoff * jnp.exp(cse_ref[:, col * LANES:(col + 1) * LANES])
        xwt = xwt_ref[col * LANES:(col + 1) * LANES, :]
        tott = tott_ref[col * LANES:(col + 1) * LANES, :]
        if nseq > 1:
            yoff = jnp.where(rmask, yoff, 0.0)
            xwt = jnp.where(cmask, xwt, jnp.zeros_like(xwt))
            deccol = jnp.exp(jnp.sum(jnp.where(onehot, tott, 0.0), axis=1, keepdims=True))
        else:
            deccol = jnp.exp(tott[:, 0:1])
        ysc_ref[:, col * LANES:(col + 1) * LANES] += yoff
        hout_ref[col * LANES:(col + 1) * LANES, :] = sp * deccol + _dot(xwt, b_ref[:, g * SSM_STATE:(g + 1) * SSM_STATE])

    @pl.when(s == nseq - 1)
    def _epilogue():
        gw = SSM_D_INNER // SSM_GROUPS
        for g in range(SSM_GROUPS):
            y = ysc_ref[:, g * gw:(g + 1) * gw] * _silu(z_ref[:, g * gw:(g + 1) * gw])
            ms = jnp.mean(y * y, axis=-1, keepdims=True)
            y_ref[:, g * gw:(g + 1) * gw] = y * lax.rsqrt(ms + EPS) * nw_ref[:, g * gw:(g + 1) * gw]


def _ssd(xbc_src, z_src, dt_src, h0, lw, *, nblk, nchunks, nseq, P, n_hist, n_real, carry,
         xbc_col, z_col, dt_col):
    L = SSM_CHUNK
    nstate = h0.shape[0]
    if carry:
        state_idx = lambda b, s: (b // nchunks, 0, 0)
    else:
        state_idx = lambda b, s: (b * nseq + s, 0, 0)
    full = lambda shape: pl.BlockSpec(shape, lambda b, s: (0,) * len(shape))
    body = functools.partial(_ssd_body, L=L, P=P, n_hist=n_hist, n_real=n_real, nseq=nseq, nchunks=nchunks,
                             carry=carry)
    return pl.pallas_call(
        body,
        grid=(nblk, nseq),
        in_specs=[
            pl.BlockSpec((L, SSM_CONV_CH), lambda b, s: (b, xbc_col)),
            pl.BlockSpec((L, SSM_D_INNER), lambda b, s: (b, z_col)),
            pl.BlockSpec((L, LANES), lambda b, s: (b, dt_col)),
            pl.BlockSpec((None, SSM_D_INNER, SSM_STATE), state_idx),
            full((SSM_CONV, SSM_CONV_CH)),
            full((1, SSM_CONV_CH)),
            full((1, LANES)),
            full((1, LANES)),
            full((1, SSM_D_INNER)),
            full((1, SSM_D_INNER)),
            full((LANES, SSM_D_INNER)),
        ],
        out_specs=[
            pl.BlockSpec((L, SSM_D_INNER), lambda b, s: (b, 0)),
            pl.BlockSpec((None, SSM_D_INNER, SSM_STATE), state_idx),
        ],
        out_shape=[
            jax.ShapeDtypeStruct((nblk * L, SSM_D_INNER), F32),
            jax.ShapeDtypeStruct((nstate, SSM_D_INNER, SSM_STATE), F32),
        ],
        scratch_shapes=[
            pltpu.VMEM((L + SUBLANES, SSM_CONV_CH), F32),
            pltpu.VMEM((L, SSM_D_INNER), F32),
            pltpu.VMEM((SSM_D_INNER, L), BF16),
            pltpu.VMEM((L, SSM_GN), BF16),
            pltpu.VMEM((L, SSM_GN), BF16),
            pltpu.VMEM((L, SSM_D_INNER), F32),
            pltpu.VMEM((SSM_D_INNER, L), F32),
        ],
        compiler_params=_cparams("arbitrary", "arbitrary"),
        name="ssd",
    )(xbc_src, z_src, dt_src, h0, lw["conv_w"], lw["conv_b"], lw["dt_bias"], lw["a_log"], lw["d_exp"],
      lw["ssm_norm_w"], lw["head_expand"])


def _mla_proj_body(ql_ref, kvl_ref, kpe_ref, kpesw_ref, cos_ref, sin_ref, qanw_ref, wq_ref, wqsw_ref, qnw_ref,
                   kvanw_ref, wk_ref, wv_ref, knw_ref,
                   q_ref, ckv_ref, kper_ref, *kv_refs, with_kv):
    cos = cos_ref[...]
    sin = sin_ref[...]
    ql = ql_ref[...]
    qn = (ql * lax.rsqrt(jnp.mean(ql * ql, axis=-1, keepdims=True) + EPS) * qanw_ref[...]).astype(BF16)
    q0 = _dot(qn, wq_ref[...])
    q1 = _dot(qn, wqsw_ref[...])
    scale = QK_DIM ** -0.5 * math.log2(math.e)
    for h in range(MLA_HEADS):
        sl = slice(h * LANES, (h + 1) * LANES)
        qh = q0[:, sl] * cos + q1[:, sl] * sin
        ms = jnp.sum(qh * qh, axis=-1, keepdims=True) * (1.0 / QK_DIM)
        q_ref[:, sl] = (qh * lax.rsqrt(ms + EPS) * (qnw_ref[...] * scale)).astype(BF16)
    kvl = kvl_ref[...]
    ckv = kvl * lax.rsqrt(jnp.mean(kvl * kvl, axis=-1, keepdims=True) + EPS) * kvanw_ref[...]
    ckv_ref[...] = ckv
    kper = kpe_ref[...] * cos + kpesw_ref[...] * sin
    kper_ref[...] = kper
    if with_kv:
        k_ref, v_ref = kv_refs
        c16 = ckv.astype(BF16)
        kn = _dot(c16, wk_ref[...])
        for h in range(MLA_HEADS):
            sl = slice(h * LANES, (h + 1) * LANES)
            kh = kn[:, sl] + kper
            ms = jnp.sum(kh * kh, axis=-1, keepdims=True) * (1.0 / QK_DIM)
            k_ref[:, sl] = (kh * lax.rsqrt(ms + EPS) * knw_ref[...]).astype(BF16)
        v_ref[...] = _dot_nt(wv_ref[...], c16).astype(BF16)


def _mla_proj(proj, cos_t, sin_t, lw, *, with_kv, tm=256):
    m = proj.shape[0]
    tm = min(tm, m)
    hp = MLA_HEADS * LANES
    full = lambda shape: pl.BlockSpec(shape, lambda i: (0,) * len(shape))
    row = lambda w, c: pl.BlockSpec((tm, w), lambda i: (i, c))
    out_specs = [row(hp, 0), row(KV_LORA, 0), row(LANES, 0)]
    out_shape = [jax.ShapeDtypeStruct((m, hp), BF16), jax.ShapeDtypeStruct((m, KV_LORA), F32),
                 jax.ShapeDtypeStruct((m, LANES), F32)]
    if with_kv:
        out_specs += [row(hp, 0), pl.BlockSpec((None, MLA_WIDTH, tm), lambda i: (i, 0, 0))]
        out_shape += [jax.ShapeDtypeStruct((m, hp), BF16), jax.ShapeDtypeStruct((m // tm, MLA_WIDTH, tm), BF16)]
    return pl.pallas_call(
        functools.partial(_mla_proj_body, with_kv=with_kv),
        grid=(m // tm,),
        in_specs=[
            row(Q_LORA, C_QL // Q_LORA), row(KV_LORA, C_KVL // KV_LORA), row(LANES, C_KPE // LANES),
            row(LANES, C_KPESW // LANES), row(LANES, 0), row(LANES, 0),
            full((1, Q_LORA)), full((Q_LORA, hp)), full((Q_LORA, hp)), full((1, LANES)),
            full((1, KV_LORA)), full((KV_LORA, hp)), full((MLA_WIDTH, KV_LORA)), full((1, LANES)),
        ],
        out_specs=out_specs,
        out_shape=out_shape,
        compiler_params=_cparams("parallel"),
        name="mla_proj",
    )(proj, proj, proj, proj, cos_t, sin_t, lw["q_a_norm_w"], lw["w_q"], lw["w_qsw"], lw["q_norm_pat"],
      lw["kv_a_norm_w"], lw["w_k"], lw["w_v_t"], lw["k_norm_pat"])


def _flash_body(q_ref, k_ref, vt_ref, o_ref, m_ref, l_ref, acc_ref, *, tq, tk):
    qi = pl.program_id(2)
    nfull = qi * (tq // tk)
    m_ref[...] = jnp.full(m_ref.shape, -jnp.inf, F32)
    l_ref[...] = jnp.zeros(l_ref.shape, F32)
    acc_ref[...] = jnp.zeros(acc_ref.shape, F32)

    tn = min(FLASH_TN, tq)

    def steps(kis, masked):
        chains = [(ki, nt, j) for ki in kis for nt in range(tq // tn) for j in range(2)]

        def qk(ch):
            ki, nt, j = ch
            koff = pl.multiple_of(ki * tk, tk)
            q = q_ref[nt * tn:(nt + 1) * tn, j * LANES:(j + 1) * LANES]
            k = k_ref[pl.ds(koff, tk), j * LANES:(j + 1) * LANES]
            return _dot_nt(k, q)

        def softmax(ch, st):
            ki, nt, j = ch
            cols = slice(nt * tn, (nt + 1) * tn)
            if masked:
                kpos = ki * tk + lax.broadcasted_iota(jnp.int32, (tk, tn), 0)
                qpos = qi * tq + nt * tn + lax.broadcasted_iota(jnp.int32, (tk, tn), 1)
                st = jnp.where(kpos <= qpos, st, -jnp.inf)
            m_old = m_ref[j, :, cols]
            m_new = jnp.maximum(m_old, jnp.max(st, axis=0, keepdims=True))
            p = jnp.exp2(st - m_new)
            alpha = jnp.exp2(m_old - m_new)
            l_ref[j, :, cols] = alpha * l_ref[j, :, cols] + jnp.sum(p.reshape(tk // SUBLANES, SUBLANES, tn), axis=0)
            m_ref[j, :, cols] = m_new
            pv = _dot(vt_ref[ki, j * V_DIM:(j + 1) * V_DIM, :], p.astype(BF16))
            return alpha, pv

        def fold(ch, alpha, pv):
            _, nt, j = ch
            cols = slice(nt * tn, (nt + 1) * tn)
            acc_ref[j, :, cols] = alpha * acc_ref[j, :, cols] + pv

        sts = [qk(c) for c in chains[:FLASH_DEPTH]]
        pending = None
        for i, ch in enumerate(chains):
            st = sts.pop(0)
            if i + FLASH_DEPTH < len(chains):
                sts.append(qk(chains[i + FLASH_DEPTH]))
            alpha, pv = softmax(ch, st)
            if pending is not None:
                fold(*pending)
            pending = (ch, alpha, pv)
        fold(*pending)

    unroll = tq // tk

    def full_steps(it, c):
        steps([it * unroll + u for u in range(unroll)], False)
        return c

    lax.fori_loop(0, qi, full_steps, 0)
    steps([nfull + d for d in range(tq // tk)], True)
    ot = jnp.concatenate([acc_ref[j] / jnp.sum(l_ref[j], axis=0, keepdims=True) for j in range(2)], axis=0)
    o_ref[...] = ot.T


def _flash(q, k, vt, *, batch, seq, tq=1024):
    tk = vt.shape[2]
    tq = min(tq, seq)
    nq = seq // tq
    nk = seq // tk
    return pl.pallas_call(
        functools.partial(_flash_body, tq=tq, tk=tk),
        grid=(batch, MLA_HEADS // 2, nq),
        in_specs=[
            pl.BlockSpec((tq, 2 * LANES), lambda b, h, i: (b * nq + i, h)),
            pl.BlockSpec((seq, 2 * LANES), lambda b, h, i: (b, h)),
            pl.BlockSpec((nk, 2 * V_DIM, tk), lambda b, h, i: (b, h, 0)),
        ],
        out_specs=pl.BlockSpec((tq, 2 * V_DIM), lambda b, h, i: (b * nq + i, h)),
        out_shape=jax.ShapeDtypeStruct((batch * seq, MLA_WIDTH), F32),
        scratch_shapes=[pltpu.VMEM((2, 1, tq), F32), pltpu.VMEM((2, SUBLANES, tq), F32),
                        pltpu.VMEM((2, V_DIM, tq), F32)],
        compiler_params=_cparams("parallel", "parallel", "arbitrary"),
        name="flash",
    )(q, k, vt)


def _qabs_body(q_ref, knw_ref, wabs_ref, epe_ref, qabs_ref, qpe_ref):
    qh = (q_ref[...].astype(F32) * knw_ref[...]).astype(BF16)
    qabs_ref[...] = _dot(qh, wabs_ref[...]).astype(BF16)
    qpe_ref[...] = _dot(qh, epe_ref[...]).astype(BF16)


def _qabs(q, lw):
    m = q.shape[0]
    return pl.pallas_call(
        _qabs_body,
        grid=(MLA_HEADS,),
        in_specs=[
            pl.BlockSpec((m, LANES), lambda h: (0, h)),
            pl.BlockSpec((1, LANES), lambda h: (0, 0)),
            pl.BlockSpec((None, LANES, KV_LORA), lambda h: (h, 0, 0)),
            pl.BlockSpec((LANES, LANES), lambda h: (0, 0)),
        ],
        out_specs=[pl.BlockSpec((m, KV_LORA), lambda h: (0, h)), pl.BlockSpec((m, LANES), lambda h: (0, h))],
        out_shape=[jax.ShapeDtypeStruct((m, MLA_HEADS * KV_LORA), BF16),
                   jax.ShapeDtypeStruct((m, MLA_HEADS * LANES), BF16)],
        compiler_params=_cparams("parallel"),
        name="qabs",
    )(q, lw["k_norm_pat"], lw["w_abs"], lw["e_pe"])


def _decode_body(pt_ref, *refs, pg, ngroups, nq):
    del pt_ref
    c_refs = refs[:pg]
    p_refs = refs[pg:2 * pg]
    (cnew_ref, pnew_ref, wt_ref, qabs_ref, qpe_ref, wv_ref, o_ref,
     m_ref, l_ref, acc_ref, lhs_ref) = refs[2 * pg:]
    g = pl.program_id(1)
    rows = nq * MLA_HEADS
    nk = MLA_HEADS * QK_NOPE

    @pl.when(g == 0)
    def _():
        m_ref[...] = jnp.full((rows, 1), -jnp.inf, F32)
        l_ref[...] = jnp.zeros((rows, 1), F32)
        acc_ref[...] = jnp.zeros((rows, KV_LORA), F32)
        lhs_ref[0:nk, :] = wt_ref[...]
        lhs_ref[nk:nk + rows, :] = qabs_ref[...]

    def project(c16):
        return _dot_nt(lhs_ref[...], c16)

    def scores(both, kpt):
        t = both.shape[1]
        kt = both[0:nk]
        ssq = jnp.sum((kt * kt).reshape(MLA_HEADS, QK_NOPE, t), axis=1)
        ssq_pe = jnp.sum(kpt * kpt, axis=0, keepdims=True)
        r = lax.rsqrt((ssq + ssq_pe) * (1.0 / QK_DIM) + EPS)
        st = both[nk:nk + rows] + _dot(qpe_ref[...], kpt.astype(BF16))
        return (st.reshape(nq, MLA_HEADS, t) * r[None]).reshape(rows, t)

    def update(st, c16s):
        m_old = m_ref[...]
        m_new = jnp.maximum(m_old, jnp.max(st, axis=-1, keepdims=True))
        p = jnp.exp2(st - m_new)
        alpha = jnp.exp2(m_old - m_new)
        l_ref[...] = alpha * l_ref[...] + jnp.sum(p, axis=-1, keepdims=True)
        p16 = p.astype(BF16)
        pv = None
        off = 0
        for c16 in c16s:
            t = c16.shape[0]
            d = _dot(p16[:, off:off + t], c16)
            pv = d if pv is None else pv + d
            off += t
        acc_ref[...] = alpha * acc_ref[...] + pv
        m_ref[...] = m_new

    @pl.when(g < ngroups)
    def _():
        def latent(k):
            return jnp.concatenate([c_refs[k][...], c_refs[k + 1][...]], axis=0).astype(BF16)

        subs = list(range(0, pg, 2))
        c16s = [latent(subs[0])]
        boths = [project(c16s[0])]
        sts = []
        for i, k in enumerate(subs):
            if i + 1 < len(subs):
                c16s.append(latent(subs[i + 1]))
                boths.append(project(c16s[-1]))
            kpt = jnp.concatenate([p_refs[k][...], p_refs[k + 1][...]], axis=1)
            sts.append(scores(boths[i], kpt))
        update(jnp.concatenate(sts, axis=1), c16s)

    @pl.when(g == ngroups)
    def _():
        qrow = lax.broadcasted_iota(jnp.int32, (rows, PAGE), 0) // MLA_HEADS
        tok = lax.broadcasted_iota(jnp.int32, (rows, PAGE), 1)
        c16 = cnew_ref[...].astype(BF16)
        st = scores(project(c16), pnew_ref[...])
        update(jnp.where(tok <= qrow, st, -jnp.inf), [c16])
        olat = (acc_ref[...] / l_ref[...]).astype(BF16)
        of = _dot(olat, wv_ref[...])
        rh = lax.broadcasted_iota(jnp.int32, (rows, MLA_WIDTH), 0) % MLA_HEADS
        ch = lax.broadcasted_iota(jnp.int32, (rows, MLA_WIDTH), 1) // V_DIM
        of = jnp.where(rh == ch, of, 0.0)
        o_ref[...] = jnp.sum(of.reshape(nq, MLA_HEADS, MLA_WIDTH), axis=1)


def _decode_attn(page_table, pool_c, pool_pt, layer, cnew, pnew_t, qabs, qpe, lw, *, pg=16):
    nb, npages = page_table.shape
    nq = qabs.shape[1] // MLA_HEADS
    rows = nq * MLA_HEADS
    pg = min(pg, npages)
    ngroups = npages // pg
    last = ngroups - 1

    def page_spec(shape, k):
        return pl.BlockSpec((None, None) + shape,
                            lambda b, g, pt: (layer, pt[b, jnp.minimum(g, last) * pg + k], 0, 0))

    per_seq = lambda shape: pl.BlockSpec((None,) + shape, lambda b, g, pt: (b, 0, 0))
    full = lambda shape: pl.BlockSpec(shape, lambda b, g, pt: (0,) * len(shape))
    in_specs = ([page_spec((PAGE, KV_LORA), k) for k in range(pg)]
                + [page_spec((QK_ROPE, PAGE), k) for k in range(pg)]
                + [per_seq((PAGE, KV_LORA)), per_seq((QK_ROPE, PAGE)), full((MLA_HEADS * QK_NOPE, KV_LORA)),
                   per_seq((rows, KV_LORA)), per_seq((rows, QK_ROPE)), full((KV_LORA, MLA_WIDTH))])
    grid_spec = pltpu.PrefetchScalarGridSpec(
        num_scalar_prefetch=1,
        grid=(nb, ngroups + 1),
        in_specs=in_specs,
        out_specs=per_seq((nq, MLA_WIDTH)),
        scratch_shapes=[pltpu.VMEM((rows, 1), F32), pltpu.VMEM((rows, 1), F32), pltpu.VMEM((rows, KV_LORA), F32),
                        pltpu.VMEM((MLA_HEADS * QK_NOPE + rows, KV_LORA), BF16)],
    )
    return pl.pallas_call(
        functools.partial(_decode_body, pg=pg, ngroups=ngroups, nq=nq),
        grid_spec=grid_spec,
        out_shape=jax.ShapeDtypeStruct((nb, nq, MLA_WIDTH), F32),
        compiler_params=_cparams("parallel", "arbitrary"),
        name="decode_attn",
    )(page_table, *([pool_c] * pg), *([pool_pt] * pg), cnew, pnew_t, lw["w_nope_t"], qabs, qpe, lw["w_v"])


def _outproj_even_body(y_ref, o_ref, g_ref, h_ref, wy_ref, wo_ref, out_ref):
    og = (o_ref[...] * _silu(g_ref[...])).astype(BF16)
    out_ref[...] = h_ref[...] + _dot(y_ref[...].astype(BF16), wy_ref[...]) + _dot(og, wo_ref[...])


def _outproj_even(y, o, proj, h, w_out, tm=512):
    m = h.shape[0]
    tm = min(tm, m)
    row = lambda c: pl.BlockSpec((tm, D_MODEL), lambda i: (i, c))
    return pl.pallas_call(
        _outproj_even_body,
        grid=(m // tm,),
        in_specs=[row(0), row(0), row(C_G // D_MODEL), row(0),
                  pl.BlockSpec((SSM_D_INNER, D_MODEL), lambda i: (0, 0)),
                  pl.BlockSpec((MLA_WIDTH, D_MODEL), lambda i: (1, 0))],
        out_specs=row(0),
        out_shape=jax.ShapeDtypeStruct((m, D_MODEL), F32),
        compiler_params=_cparams("parallel"),
        name="outproj_even",
    )(y, o, proj, h, w_out, w_out)


def _gelu(x):
    return 0.5 * x * (1.0 + jnp.tanh(math.sqrt(2.0 / math.pi) * (x + 0.044715 * (x * x * x))))


def _s5_seq_body(u_ref, bre_ref, bim_ref, cre_ref, cim_ref, d_ref, are_ref, aim_ref, pre_ref, pim_ref,
                 y_ref, hre_ref, him_ref, xre_ref, xim_ref, car_ref, cai_ref, *, lc):
    c = pl.program_id(1)
    nt = lc // SUBLANES
    ns = S5_SLAB_STATE

    @pl.when(c == 0)
    def _():
        car_ref[...] = jnp.zeros(car_ref.shape, F32)
        cai_ref[...] = jnp.zeros(cai_ref.shape, F32)

    for j in range(S5_SLABS):
        uj = u_ref[:, j * LANES:(j + 1) * LANES].astype(BF16)
        re = _dot(uj, bre_ref[j])
        im = _dot(uj, bim_ref[j])
        for si in range(3):
            sh = 1 << si
            sre = pltpu.roll(re, sh, 0).reshape(nt, SUBLANES, ns)
            sim = pltpu.roll(im, sh, 0).reshape(nt, SUBLANES, ns)
            ar = are_ref[j, si][None]
            ai = aim_ref[j, si][None]
            re3 = re.reshape(nt, SUBLANES, ns) + ar * sre - ai * sim
            im3 = im.reshape(nt, SUBLANES, ns) + ar * sim + ai * sre
            re = re3.reshape(lc, ns)
            im = im3.reshape(lc, ns)
        xre_ref[:, j * ns:(j + 1) * ns] = re
        xim_ref[:, j * ns:(j + 1) * ns] = im

    def tile(t, carry):
        cr, ci = carry
        off = pl.multiple_of(t * SUBLANES, SUBLANES)
        pr = pre_ref[...]
        pi = pim_ref[...]
        xr = xre_ref[pl.ds(off, SUBLANES), :] + pr * cr - pi * ci
        xi = xim_ref[pl.ds(off, SUBLANES), :] + pr * ci + pi * cr
        xre_ref[pl.ds(off, SUBLANES), :] = xr
        xim_ref[pl.ds(off, SUBLANES), :] = xi
        return (jnp.broadcast_to(xr[SUBLANES - 1:SUBLANES, :], xr.shape),
                jnp.broadcast_to(xi[SUBLANES - 1:SUBLANES, :], xi.shape))

    cr, ci = lax.fori_loop(0, nt, tile, (car_ref[...], cai_ref[...]))
    car_ref[...] = cr
    cai_ref[...] = ci
    hre_ref[...] = cr[0:1, :]
    him_ref[...] = ci[0:1, :]

    for j in range(S5_SLABS):
        y = (_dot(xre_ref[:, j * ns:(j + 1) * ns].astype(BF16), cre_ref[j])
             + _dot(xim_ref[:, j * ns:(j + 1) * ns].astype(BF16), cim_ref[j]))
        u = u_ref[:, j * LANES:(j + 1) * LANES]
        y_ref[:, j * LANES:(j + 1) * LANES] = _gelu(y + d_ref[:, j * LANES:(j + 1) * LANES] * u)


def _s5_seq(proj, lw, *, batch, seq, lc=256):
    lc = min(lc, seq)
    nc = seq // lc
    nst = S5_GROUPS * S5_STATE
    full = lambda shape: pl.BlockSpec(shape, lambda b, c: (0,) * len(shape))
    return pl.pallas_call(
        functools.partial(_s5_seq_body, lc=lc),
        grid=(batch, nc),
        in_specs=[
            pl.BlockSpec((lc, S5_WIDTH), lambda b, c: (b * nc + c, 0)),
            full((S5_SLABS, LANES, S5_SLAB_STATE)), full((S5_SLABS, LANES, S5_SLAB_STATE)),
            full((S5_SLABS, S5_SLAB_STATE, LANES)), full((S5_SLABS, S5_SLAB_STATE, LANES)),
            full((1, S5_WIDTH)),
            full((S5_SLABS, 3, SUBLANES, S5_SLAB_STATE)), full((S5_SLABS, 3, SUBLANES, S5_SLAB_STATE)),
            full((SUBLANES, nst)), full((SUBLANES, nst)),
        ],
        out_specs=[
            pl.BlockSpec((lc, S5_WIDTH), lambda b, c: (b * nc + c, 0)),
            pl.BlockSpec((None, 1, nst), lambda b, c: (b, 0, 0)),
            pl.BlockSpec((None, 1, nst), lambda b, c: (b, 0, 0)),
        ],
        out_shape=[
            jax.ShapeDtypeStruct((batch * seq, S5_WIDTH), F32),
            jax.ShapeDtypeStruct((batch, 1, nst), F32),
            jax.ShapeDtypeStruct((batch, 1, nst), F32),
        ],
        scratch_shapes=[pltpu.VMEM((lc, nst), F32), pltpu.VMEM((lc, nst), F32),
                        pltpu.VMEM((SUBLANES, nst), F32), pltpu.VMEM((SUBLANES, nst), F32)],
        compiler_params=_cparams("parallel", "arbitrary"),
        name="s5_seq",
    )(proj, lw["b_re"], lw["b_im"], lw["c_re"], lw["c_im"], lw["d"], lw["a_re_hs"], lw["a_im_hs"],
      lw["p_re"], lw["p_im"])


def _s5_step_body(u_ref, h0re_ref, h0im_ref, bre_ref, bim_ref, brel_ref, biml_ref, cre_ref, cim_ref, d_ref,
                  are_ref, aim_ref, y_ref, hre_ref, him_ref, *, nt):
    xr = h0re_ref[...]
    xi = h0im_ref[...]
    ar = are_ref[0]
    ai = aim_ref[0]
    for t in range(nt):
        u = u_ref[t]
        uh = u.astype(BF16)
        ul = (u - uh.astype(F32)).astype(BF16)
        bur = _dot(uh, bre_ref[0]) + _dot(ul, bre_ref[0]) + _dot(uh, brel_ref[0])
        bui = _dot(uh, bim_ref[0]) + _dot(ul, bim_ref[0]) + _dot(uh, biml_ref[0])
        xr, xi = ar * xr - ai * xi + bur, ar * xi + ai * xr + bui
        y = _dot(xr.astype(BF16), cre_ref[0]) + _dot(xi.astype(BF16), cim_ref[0])
        y_ref[t] = _gelu(y + d_ref[...] * u)
    hre_ref[...] = xr
    him_ref[...] = xi


def _s5_step(u_t, h0re, h0im, lw):
    nt, nb, _ = u_t.shape
    ns = S5_SLAB_STATE
    slab3 = lambda a, b: pl.BlockSpec((1, a, b), lambda j: (j, 0, 0))
    return pl.pallas_call(
        functools.partial(_s5_step_body, nt=nt),
        grid=(S5_SLABS,),
        in_specs=[
            pl.BlockSpec((nt, nb, LANES), lambda j: (0, 0, j)),
            pl.BlockSpec((nb, ns), lambda j: (0, j)), pl.BlockSpec((nb, ns), lambda j: (0, j)),
            slab3(LANES, ns), slab3(LANES, ns), slab3(LANES, ns), slab3(LANES, ns),
            slab3(ns, LANES), slab3(ns, LANES),
            pl.BlockSpec((1, LANES), lambda j: (0, j)),
            slab3(1, ns), slab3(1, ns),
        ],
        out_specs=[
            pl.BlockSpec((nt, nb, LANES), lambda j: (0, 0, j)),
            pl.BlockSpec((nb, ns), lambda j: (0, j)), pl.BlockSpec((nb, ns), lambda j: (0, j)),
        ],
        out_shape=[
            jax.ShapeDtypeStruct((nt, nb, S5_WIDTH), F32),
            jax.ShapeDtypeStruct((nb, S5_GROUPS * S5_STATE), F32),
            jax.ShapeDtypeStruct((nb, S5_GROUPS * S5_STATE), F32),
        ],
        compiler_params=_cparams("parallel"),
        name="s5_step",
    )(u_t, h0re, h0im, lw["b_re"], lw["b_im"], lw["b_re_lo"], lw["b_im_lo"], lw["c_re"], lw["c_im"], lw["d"],
      lw["a_re1"], lw["a_im1"])


def _odd_tail_body(y_ref, z_ref, h_ref, wg_ref, bg_ref, wo_ref, out_ref):
    y = y_ref[...]
    gl = _dot(y.astype(BF16), wg_ref[...]) + bg_ref[...]
    y = y * (1.0 / (1.0 + jnp.exp(-gl)))
    y = y * _silu(z_ref[...])
    out_ref[...] = h_ref[...] + _dot(y.astype(BF16), wo_ref[...])


def _odd_tail(y, proj, h, lw, tm=512):
    m = h.shape[0]
    tm = min(tm, m)
    row = lambda c: pl.BlockSpec((tm, D_MODEL), lambda i: (i, c))
    full = lambda shape: pl.BlockSpec(shape, lambda i: (0,) * len(shape))
    return pl.pallas_call(
        _odd_tail_body,
        grid=(m // tm,),
        in_specs=[row(0), row(1), row(0), full((S5_WIDTH, S5_WIDTH)), full((1, S5_WIDTH)),
                  full((S5_WIDTH, D_MODEL))],
        out_specs=row(0),
        out_shape=jax.ShapeDtypeStruct((m, D_MODEL), F32),
        compiler_params=_cparams("parallel"),
        name="odd_tail",
    )(y, proj, h, lw["w_glu"], lw["b_glu"], lw["w_out"])


def _rot_half_cols(w):
    half = QK_ROPE // 2
    return jnp.concatenate([-w[..., half:], w[..., :half]], axis=-1)


def _head_pad(x_nope, x_rope):
    z = jnp.zeros(x_nope.shape[:-1] + (LANES - QK_DIM,), x_nope.dtype)
    out = jnp.concatenate([x_nope, x_rope, z], axis=-1)
    return out.reshape(out.shape[:-2] + (MLA_HEADS * LANES,))


def _even_weights(w_in, conv_w, conv_b, dt_bias, a_log, d_ssm, ssm_norm_w, q_a_norm_w, w_qb, kv_a_norm_w, w_kvb,
                  q_norm_w, k_norm_w, w_out):
    k = w_in.shape[0]
    o = 0
    parts = {}
    for name, sz in (("z", SSM_D_INNER), ("xbc", SSM_CONV_CH), ("dt", SSM_HEADS), ("ql", Q_LORA), ("kvl", KV_LORA),
                     ("kpe", QK_ROPE), ("g", MLA_WIDTH)):
        parts[name] = w_in[:, o:o + sz]
        o += sz
    zc = lambda n: jnp.zeros((k, n), F32)
    kpe_blk = jnp.concatenate([zc(QK_NOPE), parts["kpe"], zc(LANES - QK_DIM)], axis=1)
    kpesw_blk = jnp.concatenate([zc(QK_NOPE), _rot_half_cols(parts["kpe"]), zc(LANES - QK_DIM)], axis=1)
    dt_blk = jnp.concatenate([parts["dt"], zc(LANES - SSM_HEADS)], axis=1)
    w_in_p = jnp.concatenate([parts["xbc"], parts["z"], parts["g"], parts["ql"], parts["kvl"], kpe_blk, kpesw_blk,
                              dt_blk, zc(N_EVEN_PAD - C_DT - LANES)], axis=1).astype(BF16)
    wq = w_qb.reshape(Q_LORA, MLA_HEADS, QK_DIM)
    zq = jnp.zeros((Q_LORA, MLA_HEADS, QK_NOPE), F32)
    wkv = w_kvb.reshape(KV_LORA, MLA_HEADS, QK_NOPE + V_DIM)
    w_nope = wkv[..., :QK_NOPE]
    pad1 = lambda v, n: jnp.concatenate([v, jnp.zeros((n - v.shape[0],), F32)]).reshape(1, n)
    norm_pat = lambda w: jnp.concatenate([w, jnp.zeros((LANES - QK_DIM,), F32)]).reshape(1, LANES)
    w_abs = jnp.concatenate([jnp.transpose(w_nope, (1, 2, 0)),
                             jnp.zeros((MLA_HEADS, LANES - QK_NOPE, KV_LORA), F32)], axis=1)
    e_pe = jnp.zeros((LANES, LANES), F32).at[QK_NOPE + jnp.arange(QK_ROPE), jnp.arange(QK_ROPE)].set(1.0)
    head_expand = (jnp.arange(LANES)[:, None] == (jnp.arange(SSM_D_INNER)[None, :] // SSM_HEAD_DIM))
    return {
        "w_in": w_in_p,
        "conv_w": conv_w, "conv_b": conv_b.reshape(1, -1),
        "dt_bias": pad1(dt_bias, LANES), "a_log": pad1(a_log, LANES),
        "d_exp": jnp.repeat(d_ssm, SSM_HEAD_DIM).reshape(1, -1),
        "ssm_norm_w": ssm_norm_w.reshape(1, -1),
        "head_expand": head_expand.astype(BF16),
        "q_a_norm_w": q_a_norm_w.reshape(1, -1),
        "w_q": _head_pad(wq[..., :QK_NOPE], wq[..., QK_NOPE:]).astype(BF16),
        "w_qsw": _head_pad(zq, _rot_half_cols(wq[..., QK_NOPE:])).astype(BF16),
        "q_norm_pat": norm_pat(q_norm_w),
        "kv_a_norm_w": kv_a_norm_w.reshape(1, -1),
        "w_k": _head_pad(w_nope, jnp.zeros((KV_LORA, MLA_HEADS, QK_ROPE), F32)).astype(BF16),
        "w_v": wkv[..., QK_NOPE:].reshape(KV_LORA, MLA_WIDTH).astype(BF16),
        "w_v_t": wkv[..., QK_NOPE:].reshape(KV_LORA, MLA_WIDTH).T.astype(BF16),
        "k_norm_pat": norm_pat(k_norm_w),
        "w_abs": w_abs.astype(BF16),
        "e_pe": e_pe.astype(BF16),
        "w_nope_t": jnp.transpose(w_nope, (1, 2, 0)).reshape(MLA_HEADS * QK_NOPE, KV_LORA).astype(BF16),
        "w_out": w_out.astype(BF16),
    }


def _rope_tables(pos):
    half = QK_ROPE // 2
    inv_freq = jnp.power(ROPE_THETA, -jnp.arange(half, dtype=F32) / half)
    ang = pos[:, None] * inv_freq[None, :]
    c, s = jnp.cos(ang), jnp.sin(ang)
    n = pos.shape[0]
    cos_t = jnp.concatenate([jnp.ones((n, QK_NOPE), F32), c, c, jnp.ones((n, LANES - QK_DIM), F32)], axis=1)
    sin_t = jnp.concatenate([jnp.zeros((n, QK_NOPE), F32), s, s, jnp.zeros((n, LANES - QK_DIM), F32)], axis=1)
    return cos_t, sin_t


def _odd_weights(w_in, a_re, a_im, b_re, b_im, c_re, c_im, d, log_step, w_glu, b_glu, w_out):
    ar, ai = a_re.astype(F32), a_im.astype(F32)
    step = jnp.exp(log_step.astype(F32))[:, None]
    mag = jnp.exp(ar * step)
    ab_re, ab_im = mag * jnp.cos(ai * step), mag * jnp.sin(ai * step)
    den = ar * ar + ai * ai
    nr, ni = ab_re - 1.0, ab_im
    f_re = (nr * ar + ni * ai) / den
    f_im = (ni * ar - nr * ai) / den
    bb_re = f_re[..., None] * b_re - f_im[..., None] * b_im
    bb_im = f_re[..., None] * b_im + f_im[..., None] * b_re
    gl = LANES // S5_GROUP_CH

    def b_blocks(bb):
        x = jnp.transpose(bb.reshape(S5_SLABS, gl, S5_STATE, S5_GROUP_CH), (0, 1, 3, 2))
        eye = jnp.eye(gl, dtype=F32)
        return jnp.einsum("sgcn,gh->sgchn", x, eye).reshape(S5_SLABS, LANES, S5_SLAB_STATE)

    def c_blocks(cc):
        x = jnp.transpose(cc.reshape(S5_SLABS, gl, S5_GROUP_CH, S5_STATE), (0, 1, 3, 2))
        eye = jnp.eye(gl, dtype=F32)
        return jnp.einsum("sgnc,gh->sgnhc", x, eye).reshape(S5_SLABS, S5_SLAB_STATE, LANES)

    def powers(n):
        pr, pi = jnp.ones_like(ab_re), jnp.zeros_like(ab_im)
        out = []
        for _ in range(n):
            pr, pi = pr * ab_re - pi * ab_im, pr * ab_im + pi * ab_re
            out.append((pr, pi))
        return out

    pw = powers(SUBLANES)
    slab = lambda x: x.reshape(S5_SLABS, S5_SLAB_STATE)
    rows = jnp.arange(SUBLANES)[None, :, None]
    hs_re = jnp.stack([jnp.where(rows >= sh, slab(pw[sh - 1][0])[:, None, :], 0.0) for sh in (1, 2, 4)], axis=1)
    hs_im = jnp.stack([jnp.where(rows >= sh, slab(pw[sh - 1][1])[:, None, :], 0.0) for sh in (1, 2, 4)], axis=1)
    p_re = jnp.stack([p[0].reshape(-1) for p in pw], axis=0)
    p_im = jnp.stack([p[1].reshape(-1) for p in pw], axis=0)
    bre_f, bim_f = b_blocks(bb_re), b_blocks(bb_im)
    bre16, bim16 = bre_f.astype(BF16), bim_f.astype(BF16)
    return {
        "w_in": w_in.astype(BF16),
        "b_re": bre16, "b_im": bim16,
        "b_re_lo": (bre_f - bre16.astype(F32)).astype(BF16), "b_im_lo": (bim_f - bim16.astype(F32)).astype(BF16),
        "c_re": c_blocks(c_re.astype(F32)).astype(BF16), "c_im": c_blocks(-c_im.astype(F32)).astype(BF16),
        "d": d.reshape(1, -1),
        "a_re_hs": hs_re, "a_im_hs": hs_im, "p_re": p_re, "p_im": p_im,
        "a_re1": slab(ab_re).reshape(S5_SLABS, 1, S5_SLAB_STATE),
        "a_im1": slab(ab_im).reshape(S5_SLABS, 1, S5_SLAB_STATE),
        "w_glu": w_glu.astype(BF16), "b_glu": b_glu.reshape(1, -1), "w_out": w_out.astype(BF16),
    }


SAMPLE_ROWS = 8


def _even_layer(hp, hs, norm_w, lw, tabs, state_ssm, state_conv, pool_c, pool_pt, layer, page_table, bp, sp, bs, ss):
    cos_p, sin_p, cos_s, sin_s = tabs
    hist = SSM_CONV - 1
    proj_p = _inproj(hp, norm_w, lw["w_in"])
    proj_s = _inproj(hs, norm_w, lw["w_in"])
    nchunks = sp // SSM_CHUNK
    zero_state = jnp.zeros((bp, SSM_D_INNER, SSM_STATE), F32)
    y_p, ssm_p = _ssd(proj_p, proj_p, proj_p, zero_state, lw, nblk=bp * nchunks, nchunks=nchunks, nseq=1,
                      P=SSM_CHUNK, n_hist=0, n_real=SSM_CHUNK, carry=True,
                      xbc_col=0, z_col=C_Z // SSM_D_INNER, dt_col=C_DT // LANES)
    conv_p = proj_p.reshape(bp, sp, N_EVEN_PAD)[:, sp - hist:, C_XBC:C_XBC + SSM_CONV_CH]
    xbc_s = proj_s[:, C_XBC:C_XBC + SSM_CONV_CH].reshape(bs, ss, SSM_CONV_CH)
    xbc_full = jnp.concatenate([state_conv, xbc_s], axis=1)
    pad_rows = SAMPLE_ROWS - hist - ss
    pad3 = lambda a: jnp.pad(a, ((0, 0), (hist, pad_rows), (0, 0)))
    xbc_pad = jnp.pad(xbc_full, ((0, 0), (0, pad_rows), (0, 0))).reshape(bs * SAMPLE_ROWS, SSM_CONV_CH)
    z_pad = pad3(proj_s[:, C_Z:C_Z + SSM_D_INNER].reshape(bs, ss, -1)).reshape(bs * SAMPLE_ROWS, -1)
    dt_pad = pad3(proj_s[:, C_DT:C_DT + LANES].reshape(bs, ss, -1)).reshape(bs * SAMPLE_ROWS, -1)
    nseq = SSM_CHUNK // SAMPLE_ROWS
    y_s_pad, ssm_s = _ssd(xbc_pad, z_pad, dt_pad, state_ssm.reshape(bs, SSM_D_INNER, SSM_STATE), lw,
                          nblk=bs // nseq, nchunks=1, nseq=nseq, P=SAMPLE_ROWS, n_hist=hist, n_real=ss,
                          carry=False, xbc_col=0, z_col=0, dt_col=0)
    y_s = y_s_pad.reshape(bs, SAMPLE_ROWS, SSM_D_INNER)[:, hist:hist + ss].reshape(bs * ss, SSM_D_INNER)
    conv_s = xbc_full[:, ss:]
    q_p, ckv_p, kper_p, k_p, v_p = _mla_proj(proj_p, cos_p, sin_p, lw, with_kv=True, tm=FLASH_TK)
    q_s, ckv_s, kper_s = _mla_proj(proj_s, cos_s, sin_s, lw, with_kv=False)
    o_p = _flash(q_p, k_p, v_p, batch=bp, seq=sp)
    kpe_p = kper_p[:, QK_NOPE:QK_DIM]
    kpe_s = kper_s[:, QK_NOPE:QK_DIM]
    qabs, qpe = _qabs(q_s, lw)
    qabs = qabs.reshape(bs, ss * MLA_HEADS, KV_LORA)
    qpe = qpe.reshape(bs, ss * MLA_HEADS, LANES)[:, :, :QK_ROPE]
    cnew = jnp.pad(ckv_s.reshape(bs, ss, KV_LORA), ((0, 0), (0, PAGE - ss), (0, 0)))
    pnew_t = jnp.pad(jnp.swapaxes(kpe_s.reshape(bs, ss, QK_ROPE), 1, 2), ((0, 0), (0, 0), (0, PAGE - ss)))
    o_s = _decode_attn(page_table, pool_c, pool_pt, layer, cnew, pnew_t, qabs, qpe, lw).reshape(bs * ss, MLA_WIDTH)
    hp = _outproj_even(y_p, o_p, proj_p, hp, lw["w_out"])
    hs = _outproj_even(y_s, o_s, proj_s, hs, lw["w_out"])
    outs = (ckv_p.reshape(bp, sp, KV_LORA), kpe_p.reshape(bp, sp, QK_ROPE), ckv_s.reshape(bs, ss, KV_LORA),
            kpe_s.reshape(bs, ss, QK_ROPE), ssm_p.reshape(bp, SSM_HEADS, SSM_HEAD_DIM, SSM_STATE),
            ssm_s.reshape(bs, SSM_HEADS, SSM_HEAD_DIM, SSM_STATE), conv_p, conv_s)
    return hp, hs, outs


def _odd_layer(hp, hs, norm_w, lw, s5_re, s5_im, bp, sp, bs, ss):
    proj_p = _inproj(hp, norm_w, lw["w_in"])
    proj_s = _inproj(hs, norm_w, lw["w_in"])
    y_p, r_p, i_p = _s5_seq(proj_p, lw, batch=bp, seq=sp)
    u_t = jnp.transpose(proj_s[:, :S5_WIDTH].reshape(bs, ss, S5_WIDTH), (1, 0, 2))
    y_t, r_s, i_s = _s5_step(u_t, s5_re.reshape(bs, -1), s5_im.reshape(bs, -1), lw)
    y_s = jnp.transpose(y_t, (1, 0, 2)).reshape(bs * ss, S5_WIDTH)
    hp = _odd_tail(y_p, proj_p, hp, lw)
    hs = _odd_tail(y_s, proj_s, hs, lw)
    shp = lambda a, b: a.reshape(b, S5_GROUPS, S5_STATE)
    return hp, hs, (shp(r_p, bp), shp(i_p, bp), shp(r_s, bs), shp(i_s, bs))


def kernel(x_prompt, x_sample, cache_ckv, cache_kpe, page_table, state_ssm, state_conv, state_s5_re, state_s5_im, norm_w, w_in_even, conv_w, conv_b, dt_bias, a_log, d_ssm, ssm_norm_w, q_a_norm_w, w_qb, kv_a_norm_w, w_kvb, q_norm_w, k_norm_w, w_out_even, w_in_odd, s5_a_re, s5_a_im, s5_b_re, s5_b_im, s5_c_re, s5_c_im, s5_d, s5_log_step, w_glu, b_glu, w_out_odd):
    bp, sp, _ = x_prompt.shape
    bs, ss, _ = x_sample.shape
    past_len = page_table.shape[1] * PAGE
    depth = norm_w.shape[0]
    pos_p = jnp.tile(jnp.arange(sp, dtype=F32), bp)
    pos_s = jnp.tile(past_len + jnp.arange(ss, dtype=F32), bs)
    tabs = _rope_tables(pos_p) + _rope_tables(pos_s)
    cache_kpe_t = jnp.swapaxes(cache_kpe, 2, 3)
    hp = x_prompt.reshape(bp * sp, D_MODEL)
    hs = x_sample.reshape(bs * ss, D_MODEL)
    even_out, odd_out = [], []
    for i in range(depth):
        j = i // 2
        if i % 2 == 0:
            lw = _even_weights(w_in_even[j], conv_w[j], conv_b[j], dt_bias[j], a_log[j], d_ssm[j], ssm_norm_w[j],
                               q_a_norm_w[j], w_qb[j], kv_a_norm_w[j], w_kvb[j], q_norm_w[j], k_norm_w[j],
                               w_out_even[j])
            hp, hs, o = _even_layer(hp, hs, norm_w[i], lw, tabs, state_ssm[j], state_conv[j], cache_ckv,
                                    cache_kpe_t, j, page_table, bp, sp, bs, ss)
            even_out.append(o)
        else:
            lw = _odd_weights(w_in_odd[j], s5_a_re[j], s5_a_im[j], s5_b_re[j], s5_b_im[j], s5_c_re[j], s5_c_im[j],
                              s5_d[j], s5_log_step[j], w_glu[j], b_glu[j], w_out_odd[j])
            hp, hs, o = _odd_layer(hp, hs, norm_w[i], lw, state_s5_re[j], state_s5_im[j], bp, sp, bs, ss)
            odd_out.append(o)
    ev = [jnp.stack([o[k] for o in even_out]) for k in range(8)]
    od = [jnp.stack([o[k] for o in odd_out]) for k in range(4)]
    return (hp.reshape(bp, sp, D_MODEL), hs.reshape(bs, ss, D_MODEL),
            ev[0], ev[1], ev[2], ev[3], ev[4], ev[5], ev[6], ev[7], od[0], od[1], od[2], od[3])
```

```python
import functools
import math

import jax
import jax.numpy as jnp
from jax import lax
from jax.experimental import pallas as pl
from jax.experimental.pallas import tpu as pltpu

F32 = jnp.float32
BF16 = jnp.bfloat16
EPS = 1e-6

D_MODEL = 1024
LANES = 128
SUBLANES = 8
VMEM_LIMIT = 48 * 1024 * 1024

SSM_D_INNER = 1024
SSM_HEAD_DIM = 64
SSM_HEADS = 16
SSM_GROUPS = 4
SSM_STATE = 128
SSM_CONV = 4
SSM_CHUNK = 128
SSM_GN = SSM_GROUPS * SSM_STATE
SSM_CONV_CH = SSM_D_INNER + 2 * SSM_GN

MLA_HEADS = 16
Q_LORA = 256
KV_LORA = 256
QK_NOPE = 64
QK_ROPE = 32
QK_DIM = QK_NOPE + QK_ROPE
V_DIM = 64
MLA_WIDTH = MLA_HEADS * V_DIM
ROPE_THETA = 10000.0
PAGE = 128
FLASH_TK = 512
FLASH_TN = 256
FLASH_DEPTH = 6
FLASH_LROWS = 16

S5_WIDTH = 1024
S5_GROUP_CH = 16
S5_GROUPS = 64
S5_STATE = 64
S5_SLABS = S5_WIDTH // LANES
S5_SLAB_STATE = (LANES // S5_GROUP_CH) * S5_STATE

C_XBC, C_Z, C_G, C_QL, C_KVL, C_KPE, C_KPESW, C_DT, N_EVEN_PAD = 0, 2048, 3072, 4096, 4352, 4608, 4736, 4864, 5120


def _cparams(*sem):
    return pltpu.CompilerParams(dimension_semantics=sem, vmem_limit_bytes=VMEM_LIMIT)


def _silu(x):
    return x * (1.0 / (1.0 + jnp.exp(-x)))


def _dot(a, b):
    return jnp.dot(a, b, preferred_element_type=F32)


def _dot_nt(a, b):
    return lax.dot_general(a, b, (((1,), (1,)), ((), ())), preferred_element_type=F32)


def _split3(v):
    hi = v.astype(BF16)
    r1 = v - hi.astype(F32)
    mid = r1.astype(BF16)
    lo = (r1 - mid.astype(F32)).astype(BF16)
    return hi, mid, lo


def _dot_exact_lhs(m_bf16, v):
    hi, mid, lo = _split3(v)
    return _dot(m_bf16, hi) + _dot(m_bf16, mid) + _dot(m_bf16, lo)


def _dot_exact_rhs(v, m_bf16):
    hi, mid, lo = _split3(v)
    return _dot(hi, m_bf16) + _dot(mid, m_bf16) + _dot(lo, m_bf16)


def _inproj_body(x_ref, nw_ref, w_ref, o_ref, xn_ref):
    @pl.when(pl.program_id(1) == 0)
    def _():
        x = x_ref[...]
        ms = jnp.mean(x * x, axis=-1, keepdims=True)
        xn_ref[...] = (x * lax.rsqrt(ms + EPS) * nw_ref[...]).astype(BF16)

    o_ref[...] = _dot(xn_ref[...], w_ref[...])


def _inproj(x, norm_w, w, tm=512, tn=1024):
    m, k = x.shape
    n = w.shape[1]
    tm = min(tm, m)
    return pl.pallas_call(
        _inproj_body,
        grid=(m // tm, n // tn),
        in_specs=[
            pl.BlockSpec((tm, k), lambda i, j: (i, 0)),
            pl.BlockSpec((1, k), lambda i, j: (0, 0)),
            pl.BlockSpec((k, tn), lambda i, j: (0, j)),
        ],
        out_specs=pl.BlockSpec((tm, tn), lambda i, j: (i, j)),
        out_shape=jax.ShapeDtypeStruct((m, n), F32),
        scratch_shapes=[pltpu.VMEM((tm, k), BF16)],
        compiler_params=_cparams("parallel", "arbitrary"),
        name="inproj",
    )(x, norm_w.reshape(1, k), w)


def _ssd_body(xbc_ref, z_ref, dt_ref, h0_ref, convw_ref, convb_ref, dtb_ref, alog_ref, dexp_ref, nw_ref, e_ref,
              y_ref, hout_ref,
              xp_ref, ysc_ref, xwt_ref, b_ref, c_ref, cse_ref, tott_ref,
              *, L, P, n_hist, n_real, nseq, nchunks, carry):
    blk = pl.program_id(0)
    s = pl.program_id(1)
    chunk = blk % nchunks
    HIST = SUBLANES

    @pl.when(s == 0)
    def _intra():
        if carry:
            @pl.when(chunk == 0)
            def _():
                xp_ref[0:HIST, :] = jnp.zeros((HIST, SSM_CONV_CH), F32)
        else:
            xp_ref[0:HIST, :] = jnp.zeros((HIST, SSM_CONV_CH), F32)
        xp_ref[HIST:HIST + L, :] = xbc_ref[...]
        conv = convb_ref[...] + convw_ref[3:4, :] * xp_ref[HIST:HIST + L, :]
        for k in range(SSM_CONV - 1):
            off = HIST - (SSM_CONV - 1) + k
            conv = conv + convw_ref[k:k + 1, :] * xp_ref[off:off + L, :]
        if carry:
            xp_ref[0:HIST, :] = xp_ref[L:L + HIST, :]
        xc = _silu(conv)
        xs = xc[:, :SSM_D_INNER]
        b_ref[...] = xc[:, SSM_D_INNER:SSM_D_INNER + SSM_GN].astype(BF16)
        c_ref[...] = xc[:, SSM_D_INNER + SSM_GN:].astype(BF16)

        raw = dt_ref[...] + dtb_ref[...]
        dt = jnp.maximum(raw, 0.0) + jnp.log1p(jnp.exp(-jnp.abs(raw)))
        ri = lax.broadcasted_iota(jnp.int32, (L, L), 0)
        ci = lax.broadcasted_iota(jnp.int32, (L, L), 1)
        if P < L:
            rp = lax.broadcasted_iota(jnp.int32, (L, LANES), 0) % P
            dt = jnp.where((rp >= n_hist) & (rp < n_hist + n_real), dt, 0.0)
            same = (ri // P) == (ci // P)
            causal = same & (ci <= ri)
        else:
            same = ri >= 0
            causal = ci <= ri
        a_neg = -jnp.exp(alog_ref[...])
        da = dt * a_neg
        m_cum = jnp.where(causal, 1.0, 0.0).astype(BF16)
        m_tot = jnp.where(same, 1.0, 0.0).astype(BF16)
        cs = _dot_exact_lhs(m_cum, da)
        tot = _dot_exact_lhs(m_tot, da)
        cst = cs.T
        e = e_ref[...]
        dt_e = _dot_exact_rhs(dt, e)
        cs_e = _dot_exact_rhs(cs, e)
        tot_e = _dot_exact_rhs(tot, e)
        cse_ref[...] = cs_e
        tott_ref[...] = tot_e.T
        xdt = xs * dt_e
        xw = xdt * jnp.exp(tot_e - cs_e)
        xwt_ref[...] = xw.T.astype(BF16)
        xdt16 = xdt.astype(BF16)
        lane = lax.broadcasted_iota(jnp.int32, (L, LANES), 1)
        for g in range(SSM_GROUPS):
            cb = _dot_nt(c_ref[:, g * SSM_STATE:(g + 1) * SSM_STATE], b_ref[:, g * SSM_STATE:(g + 1) * SSM_STATE])
            for pr in range(2):
                h0i = 4 * g + 2 * pr
                col = h0i // 2
                xpair = xdt16[:, col * LANES:(col + 1) * LANES]
                ys = []
                for hh in (h0i, h0i + 1):
                    dec = jnp.exp(jnp.where(causal, cs[:, hh:hh + 1] - cst[hh:hh + 1, :], -1e30))
                    ys.append(_dot((cb * dec).astype(BF16), xpair))
                ypair = jnp.where(lane < SSM_HEAD_DIM, ys[0], ys[1])
                ysc_ref[:, col * LANES:(col + 1) * LANES] = (
                    ypair + dexp_ref[:, col * LANES:(col + 1) * LANES] * xs[:, col * LANES:(col + 1) * LANES])

    if carry:
        @pl.when(chunk == 0)
        def _():
            hout_ref[...] = h0_ref[...]
    else:
        hout_ref[...] = h0_ref[...]

    if nseq > 1:
        rmask = (lax.broadcasted_iota(jnp.int32, (L, LANES), 0) // P) == s
        cmask = (lax.broadcasted_iota(jnp.int32, (LANES, L), 1) // P) == s
        onehot = lax.broadcasted_iota(jnp.int32, (LANES, L), 1) == s * P
    for col in range(SSM_HEADS // 2):
        g = col // 2
        sp = hout_ref[col * LANES:(col + 1) * LANES, :]
        yoff = _dot_nt(c_ref[:, g * SSM_STATE:(g + 1) * SSM_STATE], sp.astype(BF16))
        yoff = yoff * jnp.exp(cse_ref[:, col * LANES:(col + 1) * LANES])
        xwt = xwt_ref[col * LANES:(col + 1) * LANES, :]
        tott = tott_ref[col * LANES:(col + 1) * LANES, :]
        if nseq > 1:
            yoff = jnp.where(rmask, yoff, 0.0)
            xwt = jnp.where(cmask, xwt, jnp.zeros_like(xwt))
            deccol = jnp.exp(jnp.sum(jnp.where(onehot, tott, 0.0), axis=1, keepdims=True))
        else:
            deccol = jnp.exp(tott[:, 0:1])
        ysc_ref[:, col * LANES:(col + 1) * LANES] += yoff
        hout_ref[col * LANES:(col + 1) * LANES, :] = sp * deccol + _dot(xwt, b_ref[:, g * SSM_STATE:(g + 1) * SSM_STATE])

    @pl.when(s == nseq - 1)
    def _epilogue():
        gw = SSM_D_INNER // SSM_GROUPS
        for g in range(SSM_GROUPS):
            y = ysc_ref[:, g * gw:(g + 1) * gw] * _silu(z_ref[:, g * gw:(g + 1) * gw])
            ms = jnp.mean(y * y, axis=-1, keepdims=True)
            y_ref[:, g * gw:(g + 1) * gw] = y * lax.rsqrt(ms + EPS) * nw_ref[:, g * gw:(g + 1) * gw]


def _ssd(xbc_src, z_src, dt_src, h0, lw, *, nblk, nchunks, nseq, P, n_hist, n_real, carry,
         xbc_col, z_col, dt_col, h0_offset=0):
    L = SSM_CHUNK
    if carry:
        nstate = nblk // nchunks
        state_idx = lambda b, s: (b // nchunks, 0, 0)
    else:
        nstate = nblk * nseq
        state_idx = lambda b, s: (b * nseq + s, 0, 0)
    h0_idx = lambda b, s: (h0_offset + state_idx(b, s)[0], 0, 0)
    full = lambda shape: pl.BlockSpec(shape, lambda b, s: (0,) * len(shape))
    body = functools.partial(_ssd_body, L=L, P=P, n_hist=n_hist, n_real=n_real, nseq=nseq, nchunks=nchunks,
                             carry=carry)
    return pl.pallas_call(
        body,
        grid=(nblk, nseq),
        in_specs=[
            pl.BlockSpec((L, SSM_CONV_CH), lambda b, s: (b, xbc_col)),
            pl.BlockSpec((L, SSM_D_INNER), lambda b, s: (b, z_col)),
            pl.BlockSpec((L, LANES), lambda b, s: (b, dt_col)),
            pl.BlockSpec((None, SSM_D_INNER, SSM_STATE), h0_idx),
            full((SSM_CONV, SSM_CONV_CH)),
            full((1, SSM_CONV_CH)),
            full((1, LANES)),
            full((1, LANES)),
            full((1, SSM_D_INNER)),
            full((1, SSM_D_INNER)),
            full((LANES, SSM_D_INNER)),
        ],
        out_specs=[
            pl.BlockSpec((L, SSM_D_INNER), lambda b, s: (b, 0)),
            pl.BlockSpec((None, SSM_D_INNER, SSM_STATE), state_idx),
        ],
        out_shape=[
            jax.ShapeDtypeStruct((nblk * L, SSM_D_INNER), F32),
            jax.ShapeDtypeStruct((nstate, SSM_D_INNER, SSM_STATE), F32),
        ],
        scratch_shapes=[
            pltpu.VMEM((L + SUBLANES, SSM_CONV_CH), F32),
            pltpu.VMEM((L, SSM_D_INNER), F32),
            pltpu.VMEM((SSM_D_INNER, L), BF16),
            pltpu.VMEM((L, SSM_GN), BF16),
            pltpu.VMEM((L, SSM_GN), BF16),
            pltpu.VMEM((L, SSM_D_INNER), F32),
            pltpu.VMEM((SSM_D_INNER, L), F32),
        ],
        compiler_params=_cparams("arbitrary", "arbitrary"),
        name="ssd",
    )(xbc_src, z_src, dt_src, h0, lw["conv_w"], lw["conv_b"], lw["dt_bias"], lw["a_log"], lw["d_exp"],
      lw["ssm_norm_w"], lw["head_expand"])


def _mla_proj_body(ql_ref, kvl_ref, kpe_ref, kpesw_ref, cos_ref, sin_ref, qanw_ref, wq_ref, wqsw_ref, qnw_ref,
                   kvanw_ref, wk_ref, wv_ref, knw_ref,
                   q_ref, ckv_ref, kper_ref, *kv_refs, with_kv):
    cos = cos_ref[...]
    sin = sin_ref[...]
    ql = ql_ref[...]
    qn = (ql * lax.rsqrt(jnp.mean(ql * ql, axis=-1, keepdims=True) + EPS) * qanw_ref[...]).astype(BF16)
    q0 = _dot(qn, wq_ref[...])
    q1 = _dot(qn, wqsw_ref[...])
    scale = QK_DIM ** -0.5 * math.log2(math.e)
    for h in range(MLA_HEADS):
        sl = slice(h * LANES, (h + 1) * LANES)
        qh = q0[:, sl] * cos + q1[:, sl] * sin
        ms = jnp.sum(qh * qh, axis=-1, keepdims=True) * (1.0 / QK_DIM)
        q_ref[:, sl] = (qh * lax.rsqrt(ms + EPS) * (qnw_ref[...] * scale)).astype(BF16)
    kvl = kvl_ref[...]
    ckv = kvl * lax.rsqrt(jnp.mean(kvl * kvl, axis=-1, keepdims=True) + EPS) * kvanw_ref[...]
    ckv_ref[...] = ckv
    kper = kpe_ref[...] * cos + kpesw_ref[...] * sin
    kper_ref[...] = kper
    if with_kv:
        k_ref, v_ref = kv_refs
        c16 = ckv.astype(BF16)
        kn = _dot(c16, wk_ref[...])
        for h in range(MLA_HEADS):
            sl = slice(h * LANES, (h + 1) * LANES)
            kh = kn[:, sl] + kper
            ms = jnp.sum(kh * kh, axis=-1, keepdims=True) * (1.0 / QK_DIM)
            k_ref[:, sl] = (kh * lax.rsqrt(ms + EPS) * knw_ref[...]).astype(BF16)
        v_ref[...] = _dot_nt(wv_ref[...], c16).astype(BF16)


def _mla_proj(proj, cos_t, sin_t, lw, *, with_kv, tm=256):
    m = proj.shape[0]
    tm = min(tm, m)
    hp = MLA_HEADS * LANES
    full = lambda shape: pl.BlockSpec(shape, lambda i: (0,) * len(shape))
    row = lambda w, c: pl.BlockSpec((tm, w), lambda i: (i, c))
    out_specs = [row(hp, 0), row(KV_LORA, 0), row(LANES, 0)]
    out_shape = [jax.ShapeDtypeStruct((m, hp), BF16), jax.ShapeDtypeStruct((m, KV_LORA), F32),
                 jax.ShapeDtypeStruct((m, LANES), F32)]
    if with_kv:
        out_specs += [row(hp, 0), pl.BlockSpec((None, MLA_WIDTH, tm), lambda i: (i, 0, 0))]
        out_shape += [jax.ShapeDtypeStruct((m, hp), BF16), jax.ShapeDtypeStruct((m // tm, MLA_WIDTH, tm), BF16)]
    return pl.pallas_call(
        functools.partial(_mla_proj_body, with_kv=with_kv),
        grid=(m // tm,),
        in_specs=[
            row(Q_LORA, C_QL // Q_LORA), row(KV_LORA, C_KVL // KV_LORA), row(LANES, C_KPE // LANES),
            row(LANES, C_KPESW // LANES), row(LANES, 0), row(LANES, 0),
            full((1, Q_LORA)), full((Q_LORA, hp)), full((Q_LORA, hp)), full((1, LANES)),
            full((1, KV_LORA)), full((KV_LORA, hp)), full((MLA_WIDTH, KV_LORA)), full((1, LANES)),
        ],
        out_specs=out_specs,
        out_shape=out_shape,
        compiler_params=_cparams("parallel"),
        name="mla_proj",
    )(proj, proj, proj, proj, cos_t, sin_t, lw["q_a_norm_w"], lw["w_q"], lw["w_qsw"], lw["q_norm_pat"],
      lw["kv_a_norm_w"], lw["w_k"], lw["w_v_t"], lw["k_norm_pat"])


def _flash_body(q_ref, k_ref, vt_ref, o_ref, m_ref, acc_ref, *, tq, tk):
    qi = pl.program_id(2)
    nfull = qi * (tq // tk)
    m_ref[...] = jnp.full(m_ref.shape, -jnp.inf, F32)
    acc_ref[...] = jnp.zeros(acc_ref.shape, F32)
    ones = jnp.ones((FLASH_LROWS, tk), BF16)

    tn = min(FLASH_TN, tq)

    def steps(kis, masked):
        chains = [(ki, nt, j) for ki in kis for nt in range(tq // tn) for j in range(2)]

        def qk(ch):
            ki, nt, j = ch
            koff = pl.multiple_of(ki * tk, tk)
            q = q_ref[nt * tn:(nt + 1) * tn, j * LANES:(j + 1) * LANES]
            k = k_ref[pl.ds(koff, tk), j * LANES:(j + 1) * LANES]
            return _dot_nt(k, q)

        def softmax(ch, st):
            ki, nt, j = ch
            cols = slice(nt * tn, (nt + 1) * tn)
            if masked:
                kpos = ki * tk + lax.broadcasted_iota(jnp.int32, (tk, tn), 0)
                qpos = qi * tq + nt * tn + lax.broadcasted_iota(jnp.int32, (tk, tn), 1)
                st = jnp.where(kpos <= qpos, st, -jnp.inf)
            m_old = m_ref[j, :, cols]
            m_new = jnp.maximum(m_old, jnp.max(st, axis=0, keepdims=True))
            p = jnp.exp2(st - m_new)
            alpha = jnp.exp2(m_old - m_new)
            m_ref[j, :, cols] = m_new
            lhs = jnp.concatenate([vt_ref[ki, j * V_DIM:(j + 1) * V_DIM, :], ones], axis=0)
            return alpha, _dot(lhs, p.astype(BF16))

        def fold(ch, alpha, pv):
            _, nt, j = ch
            cols = slice(nt * tn, (nt + 1) * tn)
            acc_ref[j, :, cols] = alpha * acc_ref[j, :, cols] + pv

        sts = [qk(c) for c in chains[:FLASH_DEPTH]]
        pending = None
        for i, ch in enumerate(chains):
            st = sts.pop(0)
            if i + FLASH_DEPTH < len(chains):
                sts.append(qk(chains[i + FLASH_DEPTH]))
            alpha, pv = softmax(ch, st)
            if pending is not None:
                fold(*pending)
            pending = (ch, alpha, pv)
        fold(*pending)

    unroll = tq // tk

    def full_steps(it, c):
        steps([it * unroll + u for u in range(unroll)], False)
        return c

    lax.fori_loop(0, qi, full_steps, 0)
    steps([nfull + d for d in range(tq // tk)], True)
    ot = jnp.concatenate([acc_ref[j, 0:V_DIM, :] / acc_ref[j, V_DIM:V_DIM + 1, :] for j in range(2)], axis=0)
    o_ref[...] = ot.T


def _flash(q, k, vt, *, batch, seq, tq=1024):
    tk = vt.shape[2]
    tq = min(tq, seq)
    nq = seq // tq
    nk = seq // tk
    return pl.pallas_call(
        functools.partial(_flash_body, tq=tq, tk=tk),
        grid=(batch, MLA_HEADS // 2, nq),
        in_specs=[
            pl.BlockSpec((tq, 2 * LANES), lambda b, h, i: (b * nq + i, h)),
            pl.BlockSpec((seq, 2 * LANES), lambda b, h, i: (b, h)),
            pl.BlockSpec((nk, 2 * V_DIM, tk), lambda b, h, i: (b, h, 0)),
        ],
        out_specs=pl.BlockSpec((tq, 2 * V_DIM), lambda b, h, i: (b * nq + i, h)),
        out_shape=jax.ShapeDtypeStruct((batch * seq, MLA_WIDTH), F32),
        scratch_shapes=[pltpu.VMEM((2, 1, tq), F32), pltpu.VMEM((2, V_DIM + FLASH_LROWS, tq), F32)],
        compiler_params=_cparams("parallel", "parallel", "arbitrary"),
        name="flash",
    )(q, k, vt)


def _qabs_body(q_ref, knw_ref, wabs_ref, epe_ref, qabs_ref, qpe_ref):
    qh = (q_ref[...].astype(F32) * knw_ref[...]).astype(BF16)
    qabs_ref[...] = _dot(qh, wabs_ref[...]).astype(BF16)
    qpe_ref[...] = _dot(qh, epe_ref[...]).astype(BF16)


def _qabs(q, lw):
    m = q.shape[0]
    return pl.pallas_call(
        _qabs_body,
        grid=(MLA_HEADS,),
        in_specs=[
            pl.BlockSpec((m, LANES), lambda h: (0, h)),
            pl.BlockSpec((1, LANES), lambda h: (0, 0)),
            pl.BlockSpec((None, LANES, KV_LORA), lambda h: (h, 0, 0)),
            pl.BlockSpec((LANES, LANES), lambda h: (0, 0)),
        ],
        out_specs=[pl.BlockSpec((m, KV_LORA), lambda h: (0, h)), pl.BlockSpec((m, LANES), lambda h: (0, h))],
        out_shape=[jax.ShapeDtypeStruct((m, MLA_HEADS * KV_LORA), BF16),
                   jax.ShapeDtypeStruct((m, MLA_HEADS * LANES), BF16)],
        compiler_params=_cparams("parallel"),
        name="qabs",
    )(q, lw["k_norm_pat"], lw["w_abs"], lw["e_pe"])


def _decode_body(pt_ref, *refs, pg, ngroups, nq):
    del pt_ref
    c_refs = refs[:pg]
    p_refs = refs[pg:2 * pg]
    (cnew_ref, pnew_ref, wt_ref, qabs_ref, qpe_ref, wv_ref, o_ref,
     m_ref, l_ref, acc_ref, lhs_ref) = refs[2 * pg:]
    g = pl.program_id(1)
    rows = nq * MLA_HEADS
    nk = MLA_HEADS * QK_NOPE

    @pl.when(g == 0)
    def _():
        m_ref[...] = jnp.full((rows, 1), -jnp.inf, F32)
        l_ref[...] = jnp.zeros((rows, 1), F32)
        acc_ref[...] = jnp.zeros((rows, KV_LORA), F32)
        lhs_ref[0:nk, :] = wt_ref[...]
        lhs_ref[nk:nk + rows, :] = qabs_ref[...]

    def project(c16):
        return _dot_nt(lhs_ref[...], c16)

    def scores(both, kpt):
        t = both.shape[1]
        kt = both[0:nk]
        ssq = jnp.sum((kt * kt).reshape(MLA_HEADS, QK_NOPE, t), axis=1)
        ssq_pe = jnp.sum(kpt * kpt, axis=0, keepdims=True)
        r = lax.rsqrt((ssq + ssq_pe) * (1.0 / QK_DIM) + EPS)
        st = both[nk:nk + rows] + _dot(qpe_ref[...], kpt.astype(BF16))
        return (st.reshape(nq, MLA_HEADS, t) * r[None]).reshape(rows, t)

    def update(st, c16s):
        m_old = m_ref[...]
        m_new = jnp.maximum(m_old, jnp.max(st, axis=-1, keepdims=True))
        p = jnp.exp2(st - m_new)
        alpha = jnp.exp2(m_old - m_new)
        l_ref[...] = alpha * l_ref[...] + jnp.sum(p, axis=-1, keepdims=True)
        p16 = p.astype(BF16)
        pv = None
        off = 0
        for c16 in c16s:
            t = c16.shape[0]
            d = _dot(p16[:, off:off + t], c16)
            pv = d if pv is None else pv + d
            off += t
        acc_ref[...] = alpha * acc_ref[...] + pv
        m_ref[...] = m_new

    def latent(k):
        return jnp.concatenate([c_refs[k][...], c_refs[k + 1][...]], axis=0).astype(BF16)

    subs = list(range(0, pg, 2))
    c16s = [latent(subs[0])]
    boths = [project(c16s[0])]
    sts = []
    for i, k in enumerate(subs):
        if i + 1 < len(subs):
            c16s.append(latent(subs[i + 1]))
            boths.append(project(c16s[-1]))
        kpt = jnp.concatenate([p_refs[k][...], p_refs[k + 1][...]], axis=1)
        sts.append(scores(boths[i], kpt))
    update(jnp.concatenate(sts, axis=1), c16s)

    @pl.when(g == ngroups - 1)
    def _():
        qrow = lax.broadcasted_iota(jnp.int32, (rows, PAGE), 0) // MLA_HEADS
        tok = lax.broadcasted_iota(jnp.int32, (rows, PAGE), 1)
        c16 = cnew_ref[...].astype(BF16)
        st = scores(project(c16), pnew_ref[...])
        update(jnp.where(tok <= qrow, st, -jnp.inf), [c16])
        olat = (acc_ref[...] / l_ref[...]).astype(BF16)
        of = _dot(olat, wv_ref[...])
        rh = lax.broadcasted_iota(jnp.int32, (rows, MLA_WIDTH), 0) % MLA_HEADS
        ch = lax.broadcasted_iota(jnp.int32, (rows, MLA_WIDTH), 1) // V_DIM
        of = jnp.where(rh == ch, of, 0.0)
        o_ref[...] = jnp.sum(of.reshape(nq, MLA_HEADS, MLA_WIDTH), axis=1)


def _decode_attn(page_table, pool_c, pool_pt, layer, cnew, pnew_t, qabs, qpe, lw, *, pg=16):
    nb, npages = page_table.shape
    nq = qabs.shape[1] // MLA_HEADS
    rows = nq * MLA_HEADS
    pg = min(pg, npages)
    ngroups = npages // pg

    def page_spec(shape, k):
        return pl.BlockSpec((None, None) + shape, lambda b, g, pt: (layer, pt[b, g * pg + k], 0, 0))

    per_seq = lambda shape: pl.BlockSpec((None,) + shape, lambda b, g, pt: (b, 0, 0))
    full = lambda shape: pl.BlockSpec(shape, lambda b, g, pt: (0,) * len(shape))
    in_specs = ([page_spec((PAGE, KV_LORA), k) for k in range(pg)]
                + [page_spec((QK_ROPE, PAGE), k) for k in range(pg)]
                + [per_seq((PAGE, KV_LORA)), per_seq((QK_ROPE, PAGE)), full((MLA_HEADS * QK_NOPE, KV_LORA)),
                   per_seq((rows, KV_LORA)), per_seq((rows, QK_ROPE)), full((KV_LORA, MLA_WIDTH))])
    grid_spec = pltpu.PrefetchScalarGridSpec(
        num_scalar_prefetch=1,
        grid=(nb, ngroups),
        in_specs=in_specs,
        out_specs=per_seq((nq, MLA_WIDTH)),
        scratch_shapes=[pltpu.VMEM((rows, 1), F32), pltpu.VMEM((rows, 1), F32), pltpu.VMEM((rows, KV_LORA), F32),
                        pltpu.VMEM((MLA_HEADS * QK_NOPE + rows, KV_LORA), BF16)],
    )
    return pl.pallas_call(
        functools.partial(_decode_body, pg=pg, ngroups=ngroups, nq=nq),
        grid_spec=grid_spec,
        out_shape=jax.ShapeDtypeStruct((nb, nq, MLA_WIDTH), F32),
        compiler_params=_cparams("parallel", "arbitrary"),
        name="decode_attn",
    )(page_table, *([pool_c] * pg), *([pool_pt] * pg), cnew, pnew_t, lw["w_nope_t"], qabs, qpe, lw["w_v"])


def _outproj_even_body(y_ref, o_ref, g_ref, h_ref, wy_ref, wo_ref, out_ref):
    og = (o_ref[...] * _silu(g_ref[...])).astype(BF16)
    out_ref[...] = h_ref[...] + _dot(y_ref[...].astype(BF16), wy_ref[...]) + _dot(og, wo_ref[...])


def _outproj_even(y, o, proj, h, w_out, tm=512):
    m = h.shape[0]
    tm = min(tm, m)
    row = lambda c: pl.BlockSpec((tm, D_MODEL), lambda i: (i, c))
    return pl.pallas_call(
        _outproj_even_body,
        grid=(m // tm,),
        in_specs=[row(0), row(0), row(C_G // D_MODEL), row(0),
                  pl.BlockSpec((SSM_D_INNER, D_MODEL), lambda i: (0, 0)),
                  pl.BlockSpec((MLA_WIDTH, D_MODEL), lambda i: (1, 0))],
        out_specs=row(0),
        out_shape=jax.ShapeDtypeStruct((m, D_MODEL), F32),
        compiler_params=_cparams("parallel"),
        name="outproj_even",
    )(y, o, proj, h, w_out, w_out)


def _gelu(x):
    return 0.5 * x * (1.0 + jnp.tanh(math.sqrt(2.0 / math.pi) * (x + 0.044715 * (x * x * x))))


def _s5_seq_body(u_ref, bre_ref, bim_ref, cre_ref, cim_ref, d_ref, are_ref, aim_ref, pre_ref, pim_ref,
                 y_ref, hre_ref, him_ref, xre_ref, xim_ref, car_ref, cai_ref, *, lc):
    c = pl.program_id(1)
    nt = lc // SUBLANES
    ns = S5_SLAB_STATE

    @pl.when(c == 0)
    def _():
        car_ref[...] = jnp.zeros(car_ref.shape, F32)
        cai_ref[...] = jnp.zeros(cai_ref.shape, F32)

    for j in range(S5_SLABS):
        uj = u_ref[:, j * LANES:(j + 1) * LANES].astype(BF16)
        re = _dot(uj, bre_ref[j])
        im = _dot(uj, bim_ref[j])
        for si in range(3):
            sh = 1 << si
            sre = pltpu.roll(re, sh, 0).reshape(nt, SUBLANES, ns)
            sim = pltpu.roll(im, sh, 0).reshape(nt, SUBLANES, ns)
            ar = are_ref[j, si][None]
            ai = aim_ref[j, si][None]
            re3 = re.reshape(nt, SUBLANES, ns) + ar * sre - ai * sim
            im3 = im.reshape(nt, SUBLANES, ns) + ar * sim + ai * sre
            re = re3.reshape(lc, ns)
            im = im3.reshape(lc, ns)
        xre_ref[:, j * ns:(j + 1) * ns] = re
        xim_ref[:, j * ns:(j + 1) * ns] = im

    def tile(t, carry):
        cr, ci = carry
        off = pl.multiple_of(t * SUBLANES, SUBLANES)
        pr = pre_ref[...]
        pi = pim_ref[...]
        xr = xre_ref[pl.ds(off, SUBLANES), :] + pr * cr - pi * ci
        xi = xim_ref[pl.ds(off, SUBLANES), :] + pr * ci + pi * cr
        xre_ref[pl.ds(off, SUBLANES), :] = xr
        xim_ref[pl.ds(off, SUBLANES), :] = xi
        return (jnp.broadcast_to(xr[SUBLANES - 1:SUBLANES, :], xr.shape),
                jnp.broadcast_to(xi[SUBLANES - 1:SUBLANES, :], xi.shape))

    cr, ci = lax.fori_loop(0, nt, tile, (car_ref[...], cai_ref[...]))
    car_ref[...] = cr
    cai_ref[...] = ci
    hre_ref[...] = cr[0:1, :]
    him_ref[...] = ci[0:1, :]

    for j in range(S5_SLABS):
        y = (_dot(xre_ref[:, j * ns:(j + 1) * ns].astype(BF16), cre_ref[j])
             + _dot(xim_ref[:, j * ns:(j + 1) * ns].astype(BF16), cim_ref[j]))
        u = u_ref[:, j * LANES:(j + 1) * LANES]
        y_ref[:, j * LANES:(j + 1) * LANES] = _gelu(y + d_ref[:, j * LANES:(j + 1) * LANES] * u)


def _s5_seq(proj, lw, *, batch, seq, lc=256):
    lc = min(lc, seq)
    nc = seq // lc
    nst = S5_GROUPS * S5_STATE
    full = lambda shape: pl.BlockSpec(shape, lambda b, c: (0,) * len(shape))
    return pl.pallas_call(
        functools.partial(_s5_seq_body, lc=lc),
        grid=(batch, nc),
        in_specs=[
            pl.BlockSpec((lc, S5_WIDTH), lambda b, c: (b * nc + c, 0)),
            full((S5_SLABS, LANES, S5_SLAB_STATE)), full((S5_SLABS, LANES, S5_SLAB_STATE)),
            full((S5_SLABS, S5_SLAB_STATE, LANES)), full((S5_SLABS, S5_SLAB_STATE, LANES)),
            full((1, S5_WIDTH)),
            full((S5_SLABS, 3, SUBLANES, S5_SLAB_STATE)), full((S5_SLABS, 3, SUBLANES, S5_SLAB_STATE)),
            full((SUBLANES, nst)), full((SUBLANES, nst)),
        ],
        out_specs=[
            pl.BlockSpec((lc, S5_WIDTH), lambda b, c: (b * nc + c, 0)),
            pl.BlockSpec((None, 1, nst), lambda b, c: (b, 0, 0)),
            pl.BlockSpec((None, 1, nst), lambda b, c: (b, 0, 0)),
        ],
        out_shape=[
            jax.ShapeDtypeStruct((batch * seq, S5_WIDTH), F32),
            jax.ShapeDtypeStruct((batch, 1, nst), F32),
            jax.ShapeDtypeStruct((batch, 1, nst), F32),
        ],
        scratch_shapes=[pltpu.VMEM((lc, nst), F32), pltpu.VMEM((lc, nst), F32),
                        pltpu.VMEM((SUBLANES, nst), F32), pltpu.VMEM((SUBLANES, nst), F32)],
        compiler_params=_cparams("parallel", "arbitrary"),
        name="s5_seq",
    )(proj, lw["b_re"], lw["b_im"], lw["c_re"], lw["c_im"], lw["d"], lw["a_re_hs"], lw["a_im_hs"],
      lw["p_re"], lw["p_im"])


def _s5_step_body(u_ref, h0re_ref, h0im_ref, bre_ref, bim_ref, brel_ref, biml_ref, cre_ref, cim_ref, d_ref,
                  are_ref, aim_ref, y_ref, hre_ref, him_ref, *, nt):
    xr = h0re_ref[...]
    xi = h0im_ref[...]
    ar = are_ref[0]
    ai = aim_ref[0]
    for t in range(nt):
        u = u_ref[t]
        uh = u.astype(BF16)
        ul = (u - uh.astype(F32)).astype(BF16)
        bur = _dot(uh, bre_ref[0]) + _dot(ul, bre_ref[0]) + _dot(uh, brel_ref[0])
        bui = _dot(uh, bim_ref[0]) + _dot(ul, bim_ref[0]) + _dot(uh, biml_ref[0])
        xr, xi = ar * xr - ai * xi + bur, ar * xi + ai * xr + bui
        y = _dot(xr.astype(BF16), cre_ref[0]) + _dot(xi.astype(BF16), cim_ref[0])
        y_ref[t] = _gelu(y + d_ref[...] * u)
    hre_ref[...] = xr
    him_ref[...] = xi


def _s5_step(u_t, h0re, h0im, lw):
    nt, nb, _ = u_t.shape
    ns = S5_SLAB_STATE
    slab3 = lambda a, b: pl.BlockSpec((1, a, b), lambda j: (j, 0, 0))
    return pl.pallas_call(
        functools.partial(_s5_step_body, nt=nt),
        grid=(S5_SLABS,),
        in_specs=[
            pl.BlockSpec((nt, nb, LANES), lambda j: (0, 0, j)),
            pl.BlockSpec((nb, ns), lambda j: (0, j)), pl.BlockSpec((nb, ns), lambda j: (0, j)),
            slab3(LANES, ns), slab3(LANES, ns), slab3(LANES, ns), slab3(LANES, ns),
            slab3(ns, LANES), slab3(ns, LANES),
            pl.BlockSpec((1, LANES), lambda j: (0, j)),
            slab3(1, ns), slab3(1, ns),
        ],
        out_specs=[
            pl.BlockSpec((nt, nb, LANES), lambda j: (0, 0, j)),
            pl.BlockSpec((nb, ns), lambda j: (0, j)), pl.BlockSpec((nb, ns), lambda j: (0, j)),
        ],
        out_shape=[
            jax.ShapeDtypeStruct((nt, nb, S5_WIDTH), F32),
            jax.ShapeDtypeStruct((nb, S5_GROUPS * S5_STATE), F32),
            jax.ShapeDtypeStruct((nb, S5_GROUPS * S5_STATE), F32),
        ],
        compiler_params=_cparams("parallel"),
        name="s5_step",
    )(u_t, h0re, h0im, lw["b_re"], lw["b_im"], lw["b_re_lo"], lw["b_im_lo"], lw["c_re"], lw["c_im"], lw["d"],
      lw["a_re1"], lw["a_im1"])


def _odd_tail_body(y_ref, z_ref, h_ref, wg_ref, bg_ref, wo_ref, out_ref):
    y = y_ref[...]
    gl = _dot(y.astype(BF16), wg_ref[...]) + bg_ref[...]
    y = y * (1.0 / (1.0 + jnp.exp(-gl)))
    y = y * _silu(z_ref[...])
    out_ref[...] = h_ref[...] + _dot(y.astype(BF16), wo_ref[...])


def _odd_tail(y, proj, h, lw, tm=512):
    m = h.shape[0]
    tm = min(tm, m)
    row = lambda c: pl.BlockSpec((tm, D_MODEL), lambda i: (i, c))
    full = lambda shape: pl.BlockSpec(shape, lambda i: (0,) * len(shape))
    return pl.pallas_call(
        _odd_tail_body,
        grid=(m // tm,),
        in_specs=[row(0), row(1), row(0), full((S5_WIDTH, S5_WIDTH)), full((1, S5_WIDTH)),
                  full((S5_WIDTH, D_MODEL))],
        out_specs=row(0),
        out_shape=jax.ShapeDtypeStruct((m, D_MODEL), F32),
        compiler_params=_cparams("parallel"),
        name="odd_tail",
    )(y, proj, h, lw["w_glu"], lw["b_glu"], lw["w_out"])


def _rot_half_cols(w):
    half = QK_ROPE // 2
    return jnp.concatenate([-w[..., half:], w[..., :half]], axis=-1)


def _head_pad(x_nope, x_rope):
    z = jnp.zeros(x_nope.shape[:-1] + (LANES - QK_DIM,), x_nope.dtype)
    out = jnp.concatenate([x_nope, x_rope, z], axis=-1)
    return out.reshape(out.shape[:-2] + (MLA_HEADS * LANES,))


def _even_weights(w_in, conv_w, conv_b, dt_bias, a_log, d_ssm, ssm_norm_w, q_a_norm_w, w_qb, kv_a_norm_w, w_kvb,
                  q_norm_w, k_norm_w, w_out):
    k = w_in.shape[0]
    o = 0
    parts = {}
    for name, sz in (("z", SSM_D_INNER), ("xbc", SSM_CONV_CH), ("dt", SSM_HEADS), ("ql", Q_LORA), ("kvl", KV_LORA),
                     ("kpe", QK_ROPE), ("g", MLA_WIDTH)):
        parts[name] = w_in[:, o:o + sz]
        o += sz
    zc = lambda n: jnp.zeros((k, n), F32)
    kpe_blk = jnp.concatenate([zc(QK_NOPE), parts["kpe"], zc(LANES - QK_DIM)], axis=1)
    kpesw_blk = jnp.concatenate([zc(QK_NOPE), _rot_half_cols(parts["kpe"]), zc(LANES - QK_DIM)], axis=1)
    dt_blk = jnp.concatenate([parts["dt"], zc(LANES - SSM_HEADS)], axis=1)
    w_in_p = jnp.concatenate([parts["xbc"], parts["z"], parts["g"], parts["ql"], parts["kvl"], kpe_blk, kpesw_blk,
                              dt_blk, zc(N_EVEN_PAD - C_DT - LANES)], axis=1).astype(BF16)
    wq = w_qb.reshape(Q_LORA, MLA_HEADS, QK_DIM)
    zq = jnp.zeros((Q_LORA, MLA_HEADS, QK_NOPE), F32)
    wkv = w_kvb.reshape(KV_LORA, MLA_HEADS, QK_NOPE + V_DIM)
    w_nope = wkv[..., :QK_NOPE]
    pad1 = lambda v, n: jnp.concatenate([v, jnp.zeros((n - v.shape[0],), F32)]).reshape(1, n)
    norm_pat = lambda w: jnp.concatenate([w, jnp.zeros((LANES - QK_DIM,), F32)]).reshape(1, LANES)
    w_abs = jnp.concatenate([jnp.transpose(w_nope, (1, 2, 0)),
                             jnp.zeros((MLA_HEADS, LANES - QK_NOPE, KV_LORA), F32)], axis=1)
    e_pe = jnp.zeros((LANES, LANES), F32).at[QK_NOPE + jnp.arange(QK_ROPE), jnp.arange(QK_ROPE)].set(1.0)
    head_expand = (jnp.arange(LANES)[:, None] == (jnp.arange(SSM_D_INNER)[None, :] // SSM_HEAD_DIM))
    return {
        "w_in": w_in_p,
        "conv_w": conv_w, "conv_b": conv_b.reshape(1, -1),
        "dt_bias": pad1(dt_bias, LANES), "a_log": pad1(a_log, LANES),
        "d_exp": jnp.repeat(d_ssm, SSM_HEAD_DIM).reshape(1, -1),
        "ssm_norm_w": ssm_norm_w.reshape(1, -1),
        "head_expand": head_expand.astype(BF16),
        "q_a_norm_w": q_a_norm_w.reshape(1, -1),
        "w_q": _head_pad(wq[..., :QK_NOPE], wq[..., QK_NOPE:]).astype(BF16),
        "w_qsw": _head_pad(zq, _rot_half_cols(wq[..., QK_NOPE:])).astype(BF16),
        "q_norm_pat": norm_pat(q_norm_w),
        "kv_a_norm_w": kv_a_norm_w.reshape(1, -1),
        "w_k": _head_pad(w_nope, jnp.zeros((KV_LORA, MLA_HEADS, QK_ROPE), F32)).astype(BF16),
        "w_v": wkv[..., QK_NOPE:].reshape(KV_LORA, MLA_WIDTH).astype(BF16),
        "w_v_t": wkv[..., QK_NOPE:].reshape(KV_LORA, MLA_WIDTH).T.astype(BF16),
        "k_norm_pat": norm_pat(k_norm_w),
        "w_abs": w_abs.astype(BF16),
        "e_pe": e_pe.astype(BF16),
        "w_nope_t": jnp.transpose(w_nope, (1, 2, 0)).reshape(MLA_HEADS * QK_NOPE, KV_LORA).astype(BF16),
        "w_out": w_out.astype(BF16),
    }


def _rope_tables(pos):
    half = QK_ROPE // 2
    inv_freq = jnp.power(ROPE_THETA, -jnp.arange(half, dtype=F32) / half)
    ang = pos[:, None] * inv_freq[None, :]
    c, s = jnp.cos(ang), jnp.sin(ang)
    n = pos.shape[0]
    cos_t = jnp.concatenate([jnp.ones((n, QK_NOPE), F32), c, c, jnp.ones((n, LANES - QK_DIM), F32)], axis=1)
    sin_t = jnp.concatenate([jnp.zeros((n, QK_NOPE), F32), s, s, jnp.zeros((n, LANES - QK_DIM), F32)], axis=1)
    return cos_t, sin_t


def _odd_weights(w_in, a_re, a_im, b_re, b_im, c_re, c_im, d, log_step, w_glu, b_glu, w_out):
    ar, ai = a_re.astype(F32), a_im.astype(F32)
    step = jnp.exp(log_step.astype(F32))[:, None]
    mag = jnp.exp(ar * step)
    ab_re, ab_im = mag * jnp.cos(ai * step), mag * jnp.sin(ai * step)
    den = ar * ar + ai * ai
    nr, ni = ab_re - 1.0, ab_im
    f_re = (nr * ar + ni * ai) / den
    f_im = (ni * ar - nr * ai) / den
    bb_re = f_re[..., None] * b_re - f_im[..., None] * b_im
    bb_im = f_re[..., None] * b_im + f_im[..., None] * b_re
    gl = LANES // S5_GROUP_CH

    def b_blocks(bb):
        x = jnp.transpose(bb.reshape(S5_SLABS, gl, S5_STATE, S5_GROUP_CH), (0, 1, 3, 2))
        eye = jnp.eye(gl, dtype=F32)
        return jnp.einsum("sgcn,gh->sgchn", x, eye).reshape(S5_SLABS, LANES, S5_SLAB_STATE)

    def c_blocks(cc):
        x = jnp.transpose(cc.reshape(S5_SLABS, gl, S5_GROUP_CH, S5_STATE), (0, 1, 3, 2))
        eye = jnp.eye(gl, dtype=F32)
        return jnp.einsum("sgnc,gh->sgnhc", x, eye).reshape(S5_SLABS, S5_SLAB_STATE, LANES)

    def powers(n):
        pr, pi = jnp.ones_like(ab_re), jnp.zeros_like(ab_im)
        out = []
        for _ in range(n):
            pr, pi = pr * ab_re - pi * ab_im, pr * ab_im + pi * ab_re
            out.append((pr, pi))
        return out

    pw = powers(SUBLANES)
    slab = lambda x: x.reshape(S5_SLABS, S5_SLAB_STATE)
    rows = jnp.arange(SUBLANES)[None, :, None]
    hs_re = jnp.stack([jnp.where(rows >= sh, slab(pw[sh - 1][0])[:, None, :], 0.0) for sh in (1, 2, 4)], axis=1)
    hs_im = jnp.stack([jnp.where(rows >= sh, slab(pw[sh - 1][1])[:, None, :], 0.0) for sh in (1, 2, 4)], axis=1)
    p_re = jnp.stack([p[0].reshape(-1) for p in pw], axis=0)
    p_im = jnp.stack([p[1].reshape(-1) for p in pw], axis=0)
    bre_f, bim_f = b_blocks(bb_re), b_blocks(bb_im)
    bre16, bim16 = bre_f.astype(BF16), bim_f.astype(BF16)
    return {
        "w_in": w_in.astype(BF16),
        "b_re": bre16, "b_im": bim16,
        "b_re_lo": (bre_f - bre16.astype(F32)).astype(BF16), "b_im_lo": (bim_f - bim16.astype(F32)).astype(BF16),
        "c_re": c_blocks(c_re.astype(F32)).astype(BF16), "c_im": c_blocks(-c_im.astype(F32)).astype(BF16),
        "d": d.reshape(1, -1),
        "a_re_hs": hs_re, "a_im_hs": hs_im, "p_re": p_re, "p_im": p_im,
        "a_re1": slab(ab_re).reshape(S5_SLABS, 1, S5_SLAB_STATE),
        "a_im1": slab(ab_im).reshape(S5_SLABS, 1, S5_SLAB_STATE),
        "w_glu": w_glu.astype(BF16), "b_glu": b_glu.reshape(1, -1), "w_out": w_out.astype(BF16),
    }


SAMPLE_ROWS = 8


def _even_layer(hp, hs, norm_w, lw, tabs, state_ssm, state_conv, pool_c, pool_pt, layer, page_table, bp, sp, bs, ss):
    cos_p, sin_p, cos_s, sin_s = tabs
    hist = SSM_CONV - 1
    proj_p = _inproj(hp, norm_w, lw["w_in"])
    proj_s = _inproj(hs, norm_w, lw["w_in"])
    nchunks = sp // SSM_CHUNK
    zero_state = jnp.zeros((bp, SSM_D_INNER, SSM_STATE), F32)
    y_p, ssm_p = _ssd(proj_p, proj_p, proj_p, zero_state, lw, nblk=bp * nchunks, nchunks=nchunks, nseq=1,
                      P=SSM_CHUNK, n_hist=0, n_real=SSM_CHUNK, carry=True,
                      xbc_col=0, z_col=C_Z // SSM_D_INNER, dt_col=C_DT // LANES)
    conv_p = proj_p.reshape(bp, sp, N_EVEN_PAD)[:, sp - hist:, C_XBC:C_XBC + SSM_CONV_CH]
    xbc_s = proj_s[:, C_XBC:C_XBC + SSM_CONV_CH].reshape(bs, ss, SSM_CONV_CH)
    xbc_full = jnp.concatenate([state_conv, xbc_s], axis=1)
    pad_rows = SAMPLE_ROWS - hist - ss
    pad3 = lambda a: jnp.pad(a, ((0, 0), (hist, pad_rows), (0, 0)))
    xbc_pad = jnp.pad(xbc_full, ((0, 0), (0, pad_rows), (0, 0))).reshape(bs * SAMPLE_ROWS, SSM_CONV_CH)
    z_pad = pad3(proj_s[:, C_Z:C_Z + SSM_D_INNER].reshape(bs, ss, -1)).reshape(bs * SAMPLE_ROWS, -1)
    dt_pad = pad3(proj_s[:, C_DT:C_DT + LANES].reshape(bs, ss, -1)).reshape(bs * SAMPLE_ROWS, -1)
    nseq = SSM_CHUNK // SAMPLE_ROWS
    y_s_pad, ssm_s = _ssd(xbc_pad, z_pad, dt_pad, state_ssm.reshape(-1, SSM_D_INNER, SSM_STATE), lw,
                          nblk=bs // nseq, nchunks=1, nseq=nseq, P=SAMPLE_ROWS, n_hist=hist, n_real=ss,
                          carry=False, xbc_col=0, z_col=0, dt_col=0, h0_offset=layer * bs)
    y_s = y_s_pad.reshape(bs, SAMPLE_ROWS, SSM_D_INNER)[:, hist:hist + ss].reshape(bs * ss, SSM_D_INNER)
    conv_s = xbc_full[:, ss:]
    q_p, ckv_p, kper_p, k_p, v_p = _mla_proj(proj_p, cos_p, sin_p, lw, with_kv=True, tm=FLASH_TK)
    q_s, ckv_s, kper_s = _mla_proj(proj_s, cos_s, sin_s, lw, with_kv=False)
    o_p = _flash(q_p, k_p, v_p, batch=bp, seq=sp)
    kpe_p = kper_p[:, QK_NOPE:QK_DIM]
    kpe_s = kper_s[:, QK_NOPE:QK_DIM]
    qabs, qpe = _qabs(q_s, lw)
    qabs = qabs.reshape(bs, ss * MLA_HEADS, KV_LORA)
    qpe = qpe.reshape(bs, ss * MLA_HEADS, LANES)[:, :, :QK_ROPE]
    cnew = jnp.pad(ckv_s.reshape(bs, ss, KV_LORA), ((0, 0), (0, PAGE - ss), (0, 0)))
    pnew_t = jnp.pad(jnp.swapaxes(kpe_s.reshape(bs, ss, QK_ROPE), 1, 2), ((0, 0), (0, 0), (0, PAGE - ss)))
    o_s = _decode_attn(page_table, pool_c, pool_pt, layer, cnew, pnew_t, qabs, qpe, lw).reshape(bs * ss, MLA_WIDTH)
    hp = _outproj_even(y_p, o_p, proj_p, hp, lw["w_out"])
    hs = _outproj_even(y_s, o_s, proj_s, hs, lw["w_out"])
    outs = (ckv_p.reshape(bp, sp, KV_LORA), kpe_p.reshape(bp, sp, QK_ROPE), ckv_s.reshape(bs, ss, KV_LORA),
            kpe_s.reshape(bs, ss, QK_ROPE), ssm_p.reshape(bp, SSM_HEADS, SSM_HEAD_DIM, SSM_STATE),
            ssm_s.reshape(bs, SSM_HEADS, SSM_HEAD_DIM, SSM_STATE), conv_p, conv_s)
    return hp, hs, outs


def _odd_layer(hp, hs, norm_w, lw, s5_re, s5_im, bp, sp, bs, ss):
    proj_p = _inproj(hp, norm_w, lw["w_in"])
    proj_s = _inproj(hs, norm_w, lw["w_in"])
    y_p, r_p, i_p = _s5_seq(proj_p, lw, batch=bp, seq=sp)
    u_t = jnp.transpose(proj_s[:, :S5_WIDTH].reshape(bs, ss, S5_WIDTH), (1, 0, 2))
    y_t, r_s, i_s = _s5_step(u_t, s5_re.reshape(bs, -1), s5_im.reshape(bs, -1), lw)
    y_s = jnp.transpose(y_t, (1, 0, 2)).reshape(bs * ss, S5_WIDTH)
    hp = _odd_tail(y_p, proj_p, hp, lw)
    hs = _odd_tail(y_s, proj_s, hs, lw)
    shp = lambda a, b: a.reshape(b, S5_GROUPS, S5_STATE)
    return hp, hs, (shp(r_p, bp), shp(i_p, bp), shp(r_s, bs), shp(i_s, bs))


def kernel(x_prompt, x_sample, cache_ckv, cache_kpe, page_table, state_ssm, state_conv, state_s5_re, state_s5_im, norm_w, w_in_even, conv_w, conv_b, dt_bias, a_log, d_ssm, ssm_norm_w, q_a_norm_w, w_qb, kv_a_norm_w, w_kvb, q_norm_w, k_norm_w, w_out_even, w_in_odd, s5_a_re, s5_a_im, s5_b_re, s5_b_im, s5_c_re, s5_c_im, s5_d, s5_log_step, w_glu, b_glu, w_out_odd):
    bp, sp, _ = x_prompt.shape
    bs, ss, _ = x_sample.shape
    past_len = page_table.shape[1] * PAGE
    depth = norm_w.shape[0]
    pos_p = jnp.tile(jnp.arange(sp, dtype=F32), bp)
    pos_s = jnp.tile(past_len + jnp.arange(ss, dtype=F32), bs)
    tabs = _rope_tables(pos_p) + _rope_tables(pos_s)
    cache_kpe_t = jnp.swapaxes(cache_kpe, 2, 3)
    hp = x_prompt.reshape(bp * sp, D_MODEL)
    hs = x_sample.reshape(bs * ss, D_MODEL)
    even_out, odd_out = [], []
    for i in range(depth):
        j = i // 2
        if i % 2 == 0:
            lw = _even_weights(w_in_even[j], conv_w[j], conv_b[j], dt_bias[j], a_log[j], d_ssm[j], ssm_norm_w[j],
                               q_a_norm_w[j], w_qb[j], kv_a_norm_w[j], w_kvb[j], q_norm_w[j], k_norm_w[j],
                               w_out_even[j])
            hp, hs, o = _even_layer(hp, hs, norm_w[i], lw, tabs, state_ssm, state_conv[j], cache_ckv,
                                    cache_kpe_t, j, page_table, bp, sp, bs, ss)
            even_out.append(o)
        else:
            lw = _odd_weights(w_in_odd[j], s5_a_re[j], s5_a_im[j], s5_b_re[j], s5_b_im[j], s5_c_re[j], s5_c_im[j],
                              s5_d[j], s5_log_step[j], w_glu[j], b_glu[j], w_out_odd[j])
            hp, hs, o = _odd_layer(hp, hs, norm_w[i], lw, state_s5_re[j], state_s5_im[j], bp, sp, bs, ss)
            odd_out.append(o)
    ev = [jnp.stack([o[k] for o in even_out]) for k in range(8)]
    od = [jnp.stack([o[k] for o in odd_out]) for k in range(4)]
    return (hp.reshape(bp, sp, D_MODEL), hs.reshape(bs, ss, D_MODEL),
            ev[0], ev[1], ev[2], ev[3], ev[4], ev[5], ev[6], ev[7], od[0], od[1], od[2], od[3])
```

```python
import functools
import math

import jax
import jax.numpy as jnp
from jax import lax
from jax.experimental import pallas as pl
from jax.experimental.pallas import tpu as pltpu

F32 = jnp.float32
BF16 = jnp.bfloat16
EPS = 1e-6

D_MODEL = 1024
LANES = 128
SUBLANES = 8
VMEM_LIMIT = 48 * 1024 * 1024

SSM_D_INNER = 1024
SSM_HEAD_DIM = 64
SSM_HEADS = 16
SSM_GROUPS = 4
SSM_STATE = 128
SSM_CONV = 4
SSM_CHUNK = 128
SSM_GN = SSM_GROUPS * SSM_STATE
SSM_CONV_CH = SSM_D_INNER + 2 * SSM_GN

MLA_HEADS = 16
Q_LORA = 256
KV_LORA = 256
QK_NOPE = 64
QK_ROPE = 32
QK_DIM = QK_NOPE + QK_ROPE
V_DIM = 64
MLA_WIDTH = MLA_HEADS * V_DIM
ROPE_THETA = 10000.0
PAGE = 128
FLASH_TK = 512
FLASH_TN = 256
FLASH_DEPTH = 6
FLASH_LROWS = 16

S5_WIDTH = 1024
S5_GROUP_CH = 16
S5_GROUPS = 64
S5_STATE = 64
S5_SLABS = S5_WIDTH // LANES
S5_SLAB_STATE = (LANES // S5_GROUP_CH) * S5_STATE

C_XBC, C_Z, C_G, C_QL, C_KVL, C_KPE, C_KPESW, C_DT, N_EVEN_PAD = 0, 2048, 3072, 4096, 4352, 4608, 4736, 4864, 5120


def _cparams(*sem):
    return pltpu.CompilerParams(dimension_semantics=sem, vmem_limit_bytes=VMEM_LIMIT)


def _silu(x):
    return x * (1.0 / (1.0 + jnp.exp(-x)))


def _dot(a, b):
    return jnp.dot(a, b, preferred_element_type=F32)


def _dot_nt(a, b):
    return lax.dot_general(a, b, (((1,), (1,)), ((), ())), preferred_element_type=F32)


def _split3(v):
    hi = v.astype(BF16)
    r1 = v - hi.astype(F32)
    mid = r1.astype(BF16)
    lo = (r1 - mid.astype(F32)).astype(BF16)
    return hi, mid, lo


def _dot_exact_lhs(m_bf16, v):
    hi, mid, lo = _split3(v)
    return _dot(m_bf16, hi) + _dot(m_bf16, mid) + _dot(m_bf16, lo)


def _dot_exact_rhs(v, m_bf16):
    hi, mid, lo = _split3(v)
    return _dot(hi, m_bf16) + _dot(mid, m_bf16) + _dot(lo, m_bf16)


def _inproj_body(x_ref, nw_ref, w_ref, o_ref, xn_ref):
    @pl.when(pl.program_id(1) == 0)
    def _():
        x = x_ref[...]
        ms = jnp.mean(x * x, axis=-1, keepdims=True)
        xn_ref[...] = (x * lax.rsqrt(ms + EPS) * nw_ref[...]).astype(BF16)

    o_ref[...] = _dot(xn_ref[...], w_ref[...])


def _inproj(x, norm_w, w, tm=512, tn=1024):
    m, k = x.shape
    n = w.shape[1]
    tm = min(tm, m)
    return pl.pallas_call(
        _inproj_body,
        grid=(m // tm, n // tn),
        in_specs=[
            pl.BlockSpec((tm, k), lambda i, j: (i, 0)),
            pl.BlockSpec((1, k), lambda i, j: (0, 0)),
            pl.BlockSpec((k, tn), lambda i, j: (0, j)),
        ],
        out_specs=pl.BlockSpec((tm, tn), lambda i, j: (i, j)),
        out_shape=jax.ShapeDtypeStruct((m, n), F32),
        scratch_shapes=[pltpu.VMEM((tm, k), BF16)],
        compiler_params=_cparams("parallel", "arbitrary"),
        name="inproj",
    )(x, norm_w.reshape(1, k), w)


def _ssd_body(xbc_ref, z_ref, dt_ref, h0_ref, convw_ref, convb_ref, dtb_ref, alog_ref, dexp_ref, nw_ref, e_ref,
              y_ref, hout_ref,
              xp_ref, ysc_ref, xwt_ref, b_ref, c_ref, cse_ref, tott_ref,
              *, L, P, n_hist, n_real, nseq, nchunks, carry):
    blk = pl.program_id(0)
    s = pl.program_id(1)
    chunk = blk % nchunks
    HIST = SUBLANES

    @pl.when(s == 0)
    def _intra():
        if carry:
            @pl.when(chunk == 0)
            def _():
                xp_ref[0:HIST, :] = jnp.zeros((HIST, SSM_CONV_CH), F32)
        else:
            xp_ref[0:HIST, :] = jnp.zeros((HIST, SSM_CONV_CH), F32)
        xp_ref[HIST:HIST + L, :] = xbc_ref[...]
        conv = convb_ref[...] + convw_ref[3:4, :] * xp_ref[HIST:HIST + L, :]
        for k in range(SSM_CONV - 1):
            off = HIST - (SSM_CONV - 1) + k
            conv = conv + convw_ref[k:k + 1, :] * xp_ref[off:off + L, :]
        if carry:
            xp_ref[0:HIST, :] = xp_ref[L:L + HIST, :]
        xc = _silu(conv)
        xs = xc[:, :SSM_D_INNER]
        b_ref[...] = xc[:, SSM_D_INNER:SSM_D_INNER + SSM_GN].astype(BF16)
        c_ref[...] = xc[:, SSM_D_INNER + SSM_GN:].astype(BF16)

        raw = dt_ref[...] + dtb_ref[...]
        dt = jnp.maximum(raw, 0.0) + jnp.log1p(jnp.exp(-jnp.abs(raw)))
        ri = lax.broadcasted_iota(jnp.int32, (L, L), 0)
        ci = lax.broadcasted_iota(jnp.int32, (L, L), 1)
        if P < L:
            rp = lax.broadcasted_iota(jnp.int32, (L, LANES), 0) % P
            dt = jnp.where((rp >= n_hist) & (rp < n_hist + n_real), dt, 0.0)
            same = (ri // P) == (ci // P)
            causal = same & (ci <= ri)
        else:
            same = ri >= 0
            causal = ci <= ri
        a_neg = -jnp.exp(alog_ref[...])
        da = dt * a_neg
        m_cum = jnp.where(causal, 1.0, 0.0).astype(BF16)
        m_tot = jnp.where(same, 1.0, 0.0).astype(BF16)
        cs = _dot_exact_lhs(m_cum, da)
        tot = _dot_exact_lhs(m_tot, da)
        cst = cs.T
        e = e_ref[...]
        dt_e = _dot_exact_rhs(dt, e)
        cs_e = _dot_exact_rhs(cs, e)
        tot_e = _dot_exact_rhs(tot, e)
        cse_ref[...] = cs_e
        tott_ref[...] = tot_e.T
        xdt = xs * dt_e
        xw = xdt * jnp.exp(tot_e - cs_e)
        xwt_ref[...] = xw.T.astype(BF16)
        xdt16 = xdt.astype(BF16)
        lane = lax.broadcasted_iota(jnp.int32, (L, LANES), 1)
        for g in range(SSM_GROUPS):
            cb = _dot_nt(c_ref[:, g * SSM_STATE:(g + 1) * SSM_STATE], b_ref[:, g * SSM_STATE:(g + 1) * SSM_STATE])
            for pr in range(2):
                h0i = 4 * g + 2 * pr
                col = h0i // 2
                xpair = xdt16[:, col * LANES:(col + 1) * LANES]
                ys = []
                for hh in (h0i, h0i + 1):
                    dec = jnp.exp(jnp.where(causal, cs[:, hh:hh + 1] - cst[hh:hh + 1, :], -1e30))
                    ys.append(_dot((cb * dec).astype(BF16), xpair))
                ypair = jnp.where(lane < SSM_HEAD_DIM, ys[0], ys[1])
                ysc_ref[:, col * LANES:(col + 1) * LANES] = (
                    ypair + dexp_ref[:, col * LANES:(col + 1) * LANES] * xs[:, col * LANES:(col + 1) * LANES])

    if carry:
        @pl.when(chunk == 0)
        def _():
            hout_ref[...] = h0_ref[...]
    else:
        hout_ref[...] = h0_ref[...]

    if nseq > 1:
        rmask = (lax.broadcasted_iota(jnp.int32, (L, LANES), 0) // P) == s
        cmask = (lax.broadcasted_iota(jnp.int32, (LANES, L), 1) // P) == s
        onehot = lax.broadcasted_iota(jnp.int32, (LANES, L), 1) == s * P
    for col in range(SSM_HEADS // 2):
        g = col // 2
        sp = hout_ref[col * LANES:(col + 1) * LANES, :]
        yoff = _dot_nt(c_ref[:, g * SSM_STATE:(g + 1) * SSM_STATE], sp.astype(BF16))
        yoff = yoff * jnp.exp(cse_ref[:, col * LANES:(col + 1) * LANES])
        xwt = xwt_ref[col * LANES:(col + 1) * LANES, :]
        tott = tott_ref[col * LANES:(col + 1) * LANES, :]
        if nseq > 1:
            yoff = jnp.where(rmask, yoff, 0.0)
            xwt = jnp.where(cmask, xwt, jnp.zeros_like(xwt))
            deccol = jnp.exp(jnp.sum(jnp.where(onehot, tott, 0.0), axis=1, keepdims=True))
        else:
            deccol = jnp.exp(tott[:, 0:1])
        ysc_ref[:, col * LANES:(col + 1) * LANES] += yoff
        hout_ref[col * LANES:(col + 1) * LANES, :] = sp * deccol + _dot(xwt, b_ref[:, g * SSM_STATE:(g + 1) * SSM_STATE])

    @pl.when(s == nseq - 1)
    def _epilogue():
        gw = SSM_D_INNER // SSM_GROUPS
        for g in range(SSM_GROUPS):
            y = ysc_ref[:, g * gw:(g + 1) * gw] * _silu(z_ref[:, g * gw:(g + 1) * gw])
            ms = jnp.mean(y * y, axis=-1, keepdims=True)
            y_ref[:, g * gw:(g + 1) * gw] = y * lax.rsqrt(ms + EPS) * nw_ref[:, g * gw:(g + 1) * gw]


def _ssd(xbc_src, z_src, dt_src, h0, lw, *, nblk, nchunks, nseq, P, n_hist, n_real, carry,
         xbc_col, z_col, dt_col, h0_offset=0):
    L = SSM_CHUNK
    if carry:
        nstate = nblk // nchunks
        state_idx = lambda b, s: (b // nchunks, 0, 0)
    else:
        nstate = nblk * nseq
        state_idx = lambda b, s: (b * nseq + s, 0, 0)
    h0_idx = lambda b, s: (h0_offset + state_idx(b, s)[0], 0, 0)
    full = lambda shape: pl.BlockSpec(shape, lambda b, s: (0,) * len(shape))
    body = functools.partial(_ssd_body, L=L, P=P, n_hist=n_hist, n_real=n_real, nseq=nseq, nchunks=nchunks,
                             carry=carry)
    return pl.pallas_call(
        body,
        grid=(nblk, nseq),
        in_specs=[
            pl.BlockSpec((L, SSM_CONV_CH), lambda b, s: (b, xbc_col)),
            pl.BlockSpec((L, SSM_D_INNER), lambda b, s: (b, z_col)),
            pl.BlockSpec((L, LANES), lambda b, s: (b, dt_col)),
            pl.BlockSpec((None, SSM_D_INNER, SSM_STATE), h0_idx),
            full((SSM_CONV, SSM_CONV_CH)),
            full((1, SSM_CONV_CH)),
            full((1, LANES)),
            full((1, LANES)),
            full((1, SSM_D_INNER)),
            full((1, SSM_D_INNER)),
            full((LANES, SSM_D_INNER)),
        ],
        out_specs=[
            pl.BlockSpec((L, SSM_D_INNER), lambda b, s: (b, 0)),
            pl.BlockSpec((None, SSM_D_INNER, SSM_STATE), state_idx),
        ],
        out_shape=[
            jax.ShapeDtypeStruct((nblk * L, SSM_D_INNER), F32),
            jax.ShapeDtypeStruct((nstate, SSM_D_INNER, SSM_STATE), F32),
        ],
        scratch_shapes=[
            pltpu.VMEM((L + SUBLANES, SSM_CONV_CH), F32),
            pltpu.VMEM((L, SSM_D_INNER), F32),
            pltpu.VMEM((SSM_D_INNER, L), BF16),
            pltpu.VMEM((L, SSM_GN), BF16),
            pltpu.VMEM((L, SSM_GN), BF16),
            pltpu.VMEM((L, SSM_D_INNER), F32),
            pltpu.VMEM((SSM_D_INNER, L), F32),
        ],
        compiler_params=_cparams("arbitrary", "arbitrary"),
        name="ssd",
    )(xbc_src, z_src, dt_src, h0, lw["conv_w"], lw["conv_b"], lw["dt_bias"], lw["a_log"], lw["d_exp"],
      lw["ssm_norm_w"], lw["head_expand"])


def _mla_proj_body(ql_ref, kvl_ref, kpe_ref, kpesw_ref, cos_ref, sin_ref, qanw_ref, wq_ref, wqsw_ref, qnw_ref,
                   kvanw_ref, wk_ref, wv_ref, knw_ref,
                   q_ref, ckv_ref, kper_ref, *kv_refs, with_kv):
    cos = cos_ref[...]
    sin = sin_ref[...]
    ql = ql_ref[...]
    qn = (ql * lax.rsqrt(jnp.mean(ql * ql, axis=-1, keepdims=True) + EPS) * qanw_ref[...]).astype(BF16)
    q0 = _dot(qn, wq_ref[...])
    q1 = _dot(qn, wqsw_ref[...])
    scale = QK_DIM ** -0.5 * math.log2(math.e)
    for h in range(MLA_HEADS):
        sl = slice(h * LANES, (h + 1) * LANES)
        qh = q0[:, sl] * cos + q1[:, sl] * sin
        ms = jnp.sum(qh * qh, axis=-1, keepdims=True) * (1.0 / QK_DIM)
        q_ref[:, sl] = (qh * lax.rsqrt(ms + EPS) * (qnw_ref[...] * scale)).astype(BF16)
    kvl = kvl_ref[...]
    ckv = kvl * lax.rsqrt(jnp.mean(kvl * kvl, axis=-1, keepdims=True) + EPS) * kvanw_ref[...]
    ckv_ref[...] = ckv
    kper = kpe_ref[...] * cos + kpesw_ref[...] * sin
    kper_ref[...] = kper
    if with_kv:
        k_ref, v_ref = kv_refs
        c16 = ckv.astype(BF16)
        kn = _dot(c16, wk_ref[...])
        for h in range(MLA_HEADS):
            sl = slice(h * LANES, (h + 1) * LANES)
            kh = kn[:, sl] + kper
            ms = jnp.sum(kh * kh, axis=-1, keepdims=True) * (1.0 / QK_DIM)
            k_ref[:, sl] = (kh * lax.rsqrt(ms + EPS) * knw_ref[...]).astype(BF16)
        v_ref[...] = _dot_nt(wv_ref[...], c16).astype(BF16)


def _mla_proj(proj, cos_t, sin_t, lw, *, with_kv, tm=256):
    m = proj.shape[0]
    tm = min(tm, m)
    hp = MLA_HEADS * LANES
    full = lambda shape: pl.BlockSpec(shape, lambda i: (0,) * len(shape))
    row = lambda w, c: pl.BlockSpec((tm, w), lambda i: (i, c))
    out_specs = [row(hp, 0), row(KV_LORA, 0), row(LANES, 0)]
    out_shape = [jax.ShapeDtypeStruct((m, hp), BF16), jax.ShapeDtypeStruct((m, KV_LORA), F32),
                 jax.ShapeDtypeStruct((m, LANES), F32)]
    if with_kv:
        out_specs += [row(hp, 0), pl.BlockSpec((None, MLA_WIDTH, tm), lambda i: (i, 0, 0))]
        out_shape += [jax.ShapeDtypeStruct((m, hp), BF16), jax.ShapeDtypeStruct((m // tm, MLA_WIDTH, tm), BF16)]
    return pl.pallas_call(
        functools.partial(_mla_proj_body, with_kv=with_kv),
        grid=(m // tm,),
        in_specs=[
            row(Q_LORA, C_QL // Q_LORA), row(KV_LORA, C_KVL // KV_LORA), row(LANES, C_KPE // LANES),
            row(LANES, C_KPESW // LANES), row(LANES, 0), row(LANES, 0),
            full((1, Q_LORA)), full((Q_LORA, hp)), full((Q_LORA, hp)), full((1, LANES)),
            full((1, KV_LORA)), full((KV_LORA, hp)), full((MLA_WIDTH, KV_LORA)), full((1, LANES)),
        ],
        out_specs=out_specs,
        out_shape=out_shape,
        compiler_params=_cparams("parallel"),
        name="mla_proj",
    )(proj, proj, proj, proj, cos_t, sin_t, lw["q_a_norm_w"], lw["w_q"], lw["w_qsw"], lw["q_norm_pat"],
      lw["kv_a_norm_w"], lw["w_k"], lw["w_v_t"], lw["k_norm_pat"])


def _flash_body(q_ref, k_ref, vt_ref, o_ref, m_ref, acc_ref, *, tq, tk):
    qi = pl.program_id(2)
    nfull = qi * (tq // tk)
    m_ref[...] = jnp.full(m_ref.shape, -jnp.inf, F32)
    acc_ref[...] = jnp.zeros(acc_ref.shape, F32)
    ones = jnp.ones((FLASH_LROWS, tk), BF16)

    tn = min(FLASH_TN, tq)

    def steps(kis, masked):
        chains = [(ki, nt, j) for ki in kis for nt in range(tq // tn) for j in range(2)]

        def qk(ch):
            ki, nt, j = ch
            koff = pl.multiple_of(ki * tk, tk)
            q = q_ref[nt * tn:(nt + 1) * tn, j * LANES:(j + 1) * LANES]
            k = k_ref[pl.ds(koff, tk), j * LANES:(j + 1) * LANES]
            return _dot_nt(k, q)

        def softmax(ch, st):
            ki, nt, j = ch
            cols = slice(nt * tn, (nt + 1) * tn)
            if masked:
                kpos = ki * tk + lax.broadcasted_iota(jnp.int32, (tk, tn), 0)
                qpos = qi * tq + nt * tn + lax.broadcasted_iota(jnp.int32, (tk, tn), 1)
                st = jnp.where(kpos <= qpos, st, -jnp.inf)
            m_old = m_ref[j, :, cols]
            m_new = jnp.maximum(m_old, jnp.max(st, axis=0, keepdims=True))
            p = jnp.exp2(st - m_new)
            alpha = jnp.exp2(m_old - m_new)
            m_ref[j, :, cols] = m_new
            lhs = jnp.concatenate([vt_ref[ki, j * V_DIM:(j + 1) * V_DIM, :], ones], axis=0)
            return alpha, _dot(lhs, p.astype(BF16))

        def fold(ch, alpha, pv):
            _, nt, j = ch
            cols = slice(nt * tn, (nt + 1) * tn)
            acc_ref[j, :, cols] = alpha * acc_ref[j, :, cols] + pv

        sts = [qk(c) for c in chains[:FLASH_DEPTH]]
        pending = None
        for i, ch in enumerate(chains):
            st = sts.pop(0)
            if i + FLASH_DEPTH < len(chains):
                sts.append(qk(chains[i + FLASH_DEPTH]))
            alpha, pv = softmax(ch, st)
            if pending is not None:
                fold(*pending)
            pending = (ch, alpha, pv)
        fold(*pending)

    unroll = tq // tk

    def full_steps(it, c):
        steps([it * unroll + u for u in range(unroll)], False)
        return c

    lax.fori_loop(0, qi, full_steps, 0)
    steps([nfull + d for d in range(tq // tk)], True)
    ot = jnp.concatenate([acc_ref[j, 0:V_DIM, :] / acc_ref[j, V_DIM:V_DIM + 1, :] for j in range(2)], axis=0)
    o_ref[...] = ot.T


def _flash(q, k, vt, *, batch, seq, tq=1024):
    tk = vt.shape[2]
    tq = min(tq, seq)
    nq = seq // tq
    nk = seq // tk
    return pl.pallas_call(
        functools.partial(_flash_body, tq=tq, tk=tk),
        grid=(batch, MLA_HEADS // 2, nq),
        in_specs=[
            pl.BlockSpec((tq, 2 * LANES), lambda b, h, i: (b * nq + i, h)),
            pl.BlockSpec((seq, 2 * LANES), lambda b, h, i: (b, h)),
            pl.BlockSpec((nk, 2 * V_DIM, tk), lambda b, h, i: (b, h, 0)),
        ],
        out_specs=pl.BlockSpec((tq, 2 * V_DIM), lambda b, h, i: (b * nq + i, h)),
        out_shape=jax.ShapeDtypeStruct((batch * seq, MLA_WIDTH), F32),
        scratch_shapes=[pltpu.VMEM((2, 1, tq), F32), pltpu.VMEM((2, V_DIM + FLASH_LROWS, tq), F32)],
        compiler_params=_cparams("parallel", "parallel", "arbitrary"),
        name="flash",
    )(q, k, vt)


def _qabs_body(q_ref, knw_ref, wabs_ref, epe_ref, qabs_ref, qpe_ref):
    qh = (q_ref[...].astype(F32) * knw_ref[...]).astype(BF16)
    qabs_ref[...] = _dot(qh, wabs_ref[...]).astype(BF16)
    qpe_ref[...] = _dot(qh, epe_ref[...]).astype(BF16)


def _qabs(q, lw):
    m = q.shape[0]
    return pl.pallas_call(
        _qabs_body,
        grid=(MLA_HEADS,),
        in_specs=[
            pl.BlockSpec((m, LANES), lambda h: (0, h)),
            pl.BlockSpec((1, LANES), lambda h: (0, 0)),
            pl.BlockSpec((None, LANES, KV_LORA), lambda h: (h, 0, 0)),
            pl.BlockSpec((LANES, LANES), lambda h: (0, 0)),
        ],
        out_specs=[pl.BlockSpec((m, KV_LORA), lambda h: (0, h)), pl.BlockSpec((m, LANES), lambda h: (0, h))],
        out_shape=[jax.ShapeDtypeStruct((m, MLA_HEADS * KV_LORA), BF16),
                   jax.ShapeDtypeStruct((m, MLA_HEADS * LANES), BF16)],
        compiler_params=_cparams("parallel"),
        name="qabs",
    )(q, lw["k_norm_pat"], lw["w_abs"], lw["e_pe"])


def _decode_body(pt_ref, *refs, pg, ngroups, nq):
    del pt_ref
    c_refs = refs[:pg]
    p_refs = refs[pg:2 * pg]
    (cnew_ref, pnew_ref, wt_ref, qabs_ref, qpe_ref, wv_ref, o_ref,
     m_ref, l_ref, acc_ref, lhs_ref) = refs[2 * pg:]
    g = pl.program_id(1)
    rows = nq * MLA_HEADS
    nk = MLA_HEADS * QK_NOPE

    @pl.when(g == 0)
    def _():
        m_ref[...] = jnp.full((rows, 1), -jnp.inf, F32)
        l_ref[...] = jnp.zeros((rows, 1), F32)
        acc_ref[...] = jnp.zeros((rows, KV_LORA), F32)
        lhs_ref[0:nk, :] = wt_ref[...]
        lhs_ref[nk:nk + rows, :] = qabs_ref[...]

    def project(c16):
        return _dot_nt(lhs_ref[...], c16)

    def scores(both, kpt):
        t = both.shape[1]
        kt = both[0:nk]
        ssq = jnp.sum((kt * kt).reshape(MLA_HEADS, QK_NOPE, t), axis=1)
        ssq_pe = jnp.sum(kpt * kpt, axis=0, keepdims=True)
        r = lax.rsqrt((ssq + ssq_pe) * (1.0 / QK_DIM) + EPS)
        st = both[nk:nk + rows] + _dot(qpe_ref[...], kpt.astype(BF16))
        return (st.reshape(nq, MLA_HEADS, t) * r[None]).reshape(rows, t)

    def update(st, c16s):
        m_old = m_ref[...]
        m_new = jnp.maximum(m_old, jnp.max(st, axis=-1, keepdims=True))
        p = jnp.exp2(st - m_new)
        alpha = jnp.exp2(m_old - m_new)
        l_ref[...] = alpha * l_ref[...] + jnp.sum(p, axis=-1, keepdims=True)
        p16 = p.astype(BF16)
        pv = None
        off = 0
        for c16 in c16s:
            t = c16.shape[0]
            d = _dot(p16[:, off:off + t], c16)
            pv = d if pv is None else pv + d
            off += t
        acc_ref[...] = alpha * acc_ref[...] + pv
        m_ref[...] = m_new

    def latent(k):
        return jnp.concatenate([c_refs[k][...], c_refs[k + 1][...]], axis=0).astype(BF16)

    subs = list(range(0, pg, 2))
    c16s = [latent(subs[0])]
    boths = [project(c16s[0])]
    sts = []
    for i, k in enumerate(subs):
        if i + 1 < len(subs):
            c16s.append(latent(subs[i + 1]))
            boths.append(project(c16s[-1]))
        kpt = jnp.concatenate([p_refs[k][...], p_refs[k + 1][...]], axis=1)
        sts.append(scores(boths[i], kpt))
    update(jnp.concatenate(sts, axis=1), c16s)

    @pl.when(g == ngroups - 1)
    def _():
        qrow = lax.broadcasted_iota(jnp.int32, (rows, PAGE), 0) // MLA_HEADS
        tok = lax.broadcasted_iota(jnp.int32, (rows, PAGE), 1)
        c16 = cnew_ref[...].astype(BF16)
        st = scores(project(c16), pnew_ref[...])
        update(jnp.where(tok <= qrow, st, -jnp.inf), [c16])
        olat = (acc_ref[...] / l_ref[...]).astype(BF16)
        of = _dot(olat, wv_ref[...])
        rh = lax.broadcasted_iota(jnp.int32, (rows, MLA_WIDTH), 0) % MLA_HEADS
        ch = lax.broadcasted_iota(jnp.int32, (rows, MLA_WIDTH), 1) // V_DIM
        of = jnp.where(rh == ch, of, 0.0)
        o_ref[...] = jnp.sum(of.reshape(nq, MLA_HEADS, MLA_WIDTH), axis=1)


def _decode_attn(page_table, pool_c, pool_pt, layer, cnew, pnew_t, qabs, qpe, lw, *, pg=16):
    nb, npages = page_table.shape
    nq = qabs.shape[1] // MLA_HEADS
    rows = nq * MLA_HEADS
    pg = min(pg, npages)
    ngroups = npages // pg

    def page_spec(shape, k):
        return pl.BlockSpec((None, None) + shape, lambda b, g, pt: (layer, pt[b, g * pg + k], 0, 0))

    per_seq = lambda shape: pl.BlockSpec((None,) + shape, lambda b, g, pt: (b, 0, 0))
    full = lambda shape: pl.BlockSpec(shape, lambda b, g, pt: (0,) * len(shape))
    in_specs = ([page_spec((PAGE, KV_LORA), k) for k in range(pg)]
                + [page_spec((QK_ROPE, PAGE), k) for k in range(pg)]
                + [per_seq((PAGE, KV_LORA)), per_seq((QK_ROPE, PAGE)), full((MLA_HEADS * QK_NOPE, KV_LORA)),
                   per_seq((rows, KV_LORA)), per_seq((rows, QK_ROPE)), full((KV_LORA, MLA_WIDTH))])
    grid_spec = pltpu.PrefetchScalarGridSpec(
        num_scalar_prefetch=1,
        grid=(nb, ngroups),
        in_specs=in_specs,
        out_specs=per_seq((nq, MLA_WIDTH)),
        scratch_shapes=[pltpu.VMEM((rows, 1), F32), pltpu.VMEM((rows, 1), F32), pltpu.VMEM((rows, KV_LORA), F32),
                        pltpu.VMEM((MLA_HEADS * QK_NOPE + rows, KV_LORA), BF16)],
    )
    return pl.pallas_call(
        functools.partial(_decode_body, pg=pg, ngroups=ngroups, nq=nq),
        grid_spec=grid_spec,
        out_shape=jax.ShapeDtypeStruct((nb, nq, MLA_WIDTH), F32),
        compiler_params=_cparams("parallel", "arbitrary"),
        name="decode_attn",
    )(page_table, *([pool_c] * pg), *([pool_pt] * pg), cnew, pnew_t, lw["w_nope_t"], qabs, qpe, lw["w_v"])


def _outproj_even_body(y_ref, o_ref, g_ref, h_ref, wy_ref, wo_ref, out_ref):
    og = (o_ref[...] * _silu(g_ref[...])).astype(BF16)
    out_ref[...] = h_ref[...] + _dot(y_ref[...].astype(BF16), wy_ref[...]) + _dot(og, wo_ref[...])


def _outproj_even(y, o, proj, h, w_out, tm=512):
    m = h.shape[0]
    tm = min(tm, m)
    row = lambda c: pl.BlockSpec((tm, D_MODEL), lambda i: (i, c))
    return pl.pallas_call(
        _outproj_even_body,
        grid=(m // tm,),
        in_specs=[row(0), row(0), row(C_G // D_MODEL), row(0),
                  pl.BlockSpec((SSM_D_INNER, D_MODEL), lambda i: (0, 0)),
                  pl.BlockSpec((MLA_WIDTH, D_MODEL), lambda i: (1, 0))],
        out_specs=row(0),
        out_shape=jax.ShapeDtypeStruct((m, D_MODEL), F32),
        compiler_params=_cparams("parallel"),
        name="outproj_even",
    )(y, o, proj, h, w_out, w_out)


def _gelu(x):
    return 0.5 * x * (1.0 + jnp.tanh(math.sqrt(2.0 / math.pi) * (x + 0.044715 * (x * x * x))))


def _s5_seq_body(u_ref, kst_ref, bst_ref, mr_ref, d_ref, are_ref, aim_ref, pre_ref, pim_ref,
                 y_ref, hre_ref, him_ref, xre_ref, xim_ref, ys_ref, *, lc):
    c = pl.program_id(2)
    tile = SUBLANES
    nt = lc // tile
    ns = S5_SLAB_STATE

    @pl.when(c == 0)
    def _():
        xre_ref[0:tile, :] = jnp.zeros((tile, ns), F32)
        xim_ref[0:tile, :] = jnp.zeros((tile, ns), F32)

    u = u_ref[...]
    row = lax.broadcasted_iota(jnp.int32, (lc, LANES), 0) % tile
    parts = [u.astype(BF16)]
    for j in range(1, tile):
        parts.append(jnp.where(row >= j, pltpu.roll(u, j, 0), 0.0).astype(BF16))
    y_local = _dot(jnp.concatenate(parts, axis=1), kst_ref[0])

    tstack = jnp.concatenate(
        [u_ref[pl.ds(tile - 1 - j, nt, stride=tile), :].astype(BF16) for j in range(tile)], axis=1)
    v = _dot(tstack, bst_ref[0])
    re, im = v[:, :ns], v[:, ns:]
    for si in range(3):
        sh = 1 << si
        sre = pltpu.roll(re, sh, 0).reshape(nt // tile, tile, ns)
        sim = pltpu.roll(im, sh, 0).reshape(nt // tile, tile, ns)
        ar = are_ref[0, si][None]
        ai = aim_ref[0, si][None]
        re3 = re.reshape(nt // tile, tile, ns) + ar * sre - ai * sim
        im3 = im.reshape(nt // tile, tile, ns) + ar * sim + ai * sre
        re = re3.reshape(nt, ns)
        im = im3.reshape(nt, ns)
    cr = jnp.broadcast_to(xre_ref[tile - 1:tile, :], (tile, ns))
    ci = jnp.broadcast_to(xim_ref[tile - 1:tile, :], (tile, ns))
    pr = pre_ref[0]
    pi = pim_ref[0]
    for g in range(nt // tile):
        xr = re[g * tile:(g + 1) * tile] + pr * cr - pi * ci
        xi = im[g * tile:(g + 1) * tile] + pr * ci + pi * cr
        xre_ref[(g + 1) * tile:(g + 2) * tile, :] = xr
        xim_ref[(g + 1) * tile:(g + 2) * tile, :] = xi
        cr = jnp.broadcast_to(xr[tile - 1:tile, :], (tile, ns))
        ci = jnp.broadcast_to(xi[tile - 1:tile, :], (tile, ns))
    hre_ref[...] = cr[0:1, :]
    him_ref[...] = ci[0:1, :]

    xp = jnp.concatenate([xre_ref[tile - 1:tile - 1 + nt, :], xim_ref[tile - 1:tile - 1 + nt, :]],
                         axis=1).astype(BF16)
    for r in range(tile):
        ys_ref[pl.ds(r, nt, stride=tile), :] = _dot(xp, mr_ref[0, r])
    xre_ref[0:tile, :] = xre_ref[nt:nt + tile, :]
    xim_ref[0:tile, :] = xim_ref[nt:nt + tile, :]
    y_ref[...] = _gelu(y_local + ys_ref[...] + d_ref[...] * u)


def _s5_seq(proj, lw, *, batch, seq, lc=1024):
    lc = min(lc, seq)
    nc = seq // lc
    ns = S5_SLAB_STATE
    nst = S5_GROUPS * S5_STATE
    kdim = SUBLANES * LANES
    slab = lambda *shape: pl.BlockSpec((1,) + shape, lambda b, j, c: (j,) + (0,) * len(shape))
    return pl.pallas_call(
        functools.partial(_s5_seq_body, lc=lc),
        grid=(batch, S5_SLABS, nc),
        in_specs=[
            pl.BlockSpec((lc, LANES), lambda b, j, c: (b * nc + c, j)),
            slab(kdim, LANES), slab(kdim, 2 * ns), slab(SUBLANES, 2 * ns, LANES),
            pl.BlockSpec((1, LANES), lambda b, j, c: (0, j)),
            slab(3, SUBLANES, ns), slab(3, SUBLANES, ns), slab(SUBLANES, ns), slab(SUBLANES, ns),
        ],
        out_specs=[
            pl.BlockSpec((lc, LANES), lambda b, j, c: (b * nc + c, j)),
            pl.BlockSpec((None, 1, ns), lambda b, j, c: (b, 0, j)),
            pl.BlockSpec((None, 1, ns), lambda b, j, c: (b, 0, j)),
        ],
        out_shape=[
            jax.ShapeDtypeStruct((batch * seq, S5_WIDTH), F32),
            jax.ShapeDtypeStruct((batch, 1, nst), F32),
            jax.ShapeDtypeStruct((batch, 1, nst), F32),
        ],
        scratch_shapes=[pltpu.VMEM((lc // SUBLANES + SUBLANES, ns), F32),
                        pltpu.VMEM((lc // SUBLANES + SUBLANES, ns), F32), pltpu.VMEM((lc, LANES), F32)],
        compiler_params=_cparams("parallel", "arbitrary", "arbitrary"),
        name="s5_seq",
    )(proj, lw["k_stack"], lw["b_stack"], lw["m_rows"], lw["d"], lw["a8_re_hs"], lw["a8_im_hs"],
      lw["p8_re"], lw["p8_im"])


def _s5_step_body(u_ref, h0re_ref, h0im_ref, bre_ref, bim_ref, brel_ref, biml_ref, cre_ref, cim_ref, d_ref,
                  are_ref, aim_ref, y_ref, hre_ref, him_ref, *, nt):
    xr = h0re_ref[...]
    xi = h0im_ref[...]
    ar = are_ref[0]
    ai = aim_ref[0]
    for t in range(nt):
        u = u_ref[t]
        uh = u.astype(BF16)
        ul = (u - uh.astype(F32)).astype(BF16)
        bur = _dot(uh, bre_ref[0]) + _dot(ul, bre_ref[0]) + _dot(uh, brel_ref[0])
        bui = _dot(uh, bim_ref[0]) + _dot(ul, bim_ref[0]) + _dot(uh, biml_ref[0])
        xr, xi = ar * xr - ai * xi + bur, ar * xi + ai * xr + bui
        y = _dot(xr.astype(BF16), cre_ref[0]) + _dot(xi.astype(BF16), cim_ref[0])
        y_ref[t] = _gelu(y + d_ref[...] * u)
    hre_ref[...] = xr
    him_ref[...] = xi


def _s5_step(u_t, h0re, h0im, lw):
    nt, nb, _ = u_t.shape
    ns = S5_SLAB_STATE
    slab3 = lambda a, b: pl.BlockSpec((1, a, b), lambda j: (j, 0, 0))
    return pl.pallas_call(
        functools.partial(_s5_step_body, nt=nt),
        grid=(S5_SLABS,),
        in_specs=[
            pl.BlockSpec((nt, nb, LANES), lambda j: (0, 0, j)),
            pl.BlockSpec((nb, ns), lambda j: (0, j)), pl.BlockSpec((nb, ns), lambda j: (0, j)),
            slab3(LANES, ns), slab3(LANES, ns), slab3(LANES, ns), slab3(LANES, ns),
            slab3(ns, LANES), slab3(ns, LANES),
            pl.BlockSpec((1, LANES), lambda j: (0, j)),
            slab3(1, ns), slab3(1, ns),
        ],
        out_specs=[
            pl.BlockSpec((nt, nb, LANES), lambda j: (0, 0, j)),
            pl.BlockSpec((nb, ns), lambda j: (0, j)), pl.BlockSpec((nb, ns), lambda j: (0, j)),
        ],
        out_shape=[
            jax.ShapeDtypeStruct((nt, nb, S5_WIDTH), F32),
            jax.ShapeDtypeStruct((nb, S5_GROUPS * S5_STATE), F32),
            jax.ShapeDtypeStruct((nb, S5_GROUPS * S5_STATE), F32),
        ],
        compiler_params=_cparams("parallel"),
        name="s5_step",
    )(u_t, h0re, h0im, lw["b_re"], lw["b_im"], lw["b_re_lo"], lw["b_im_lo"], lw["c_re"], lw["c_im"], lw["d"],
      lw["a_re1"], lw["a_im1"])


def _odd_tail_body(y_ref, z_ref, h_ref, wg_ref, bg_ref, wo_ref, out_ref):
    y = y_ref[...]
    gl = _dot(y.astype(BF16), wg_ref[...]) + bg_ref[...]
    y = y * (1.0 / (1.0 + jnp.exp(-gl)))
    y = y * _silu(z_ref[...])
    out_ref[...] = h_ref[...] + _dot(y.astype(BF16), wo_ref[...])


def _odd_tail(y, proj, h, lw, tm=512):
    m = h.shape[0]
    tm = min(tm, m)
    row = lambda c: pl.BlockSpec((tm, D_MODEL), lambda i: (i, c))
    full = lambda shape: pl.BlockSpec(shape, lambda i: (0,) * len(shape))
    return pl.pallas_call(
        _odd_tail_body,
        grid=(m // tm,),
        in_specs=[row(0), row(1), row(0), full((S5_WIDTH, S5_WIDTH)), full((1, S5_WIDTH)),
                  full((S5_WIDTH, D_MODEL))],
        out_specs=row(0),
        out_shape=jax.ShapeDtypeStruct((m, D_MODEL), F32),
        compiler_params=_cparams("parallel"),
        name="odd_tail",
    )(y, proj, h, lw["w_glu"], lw["b_glu"], lw["w_out"])


def _rot_half_cols(w):
    half = QK_ROPE // 2
    return jnp.concatenate([-w[..., half:], w[..., :half]], axis=-1)


def _head_pad(x_nope, x_rope):
    z = jnp.zeros(x_nope.shape[:-1] + (LANES - QK_DIM,), x_nope.dtype)
    out = jnp.concatenate([x_nope, x_rope, z], axis=-1)
    return out.reshape(out.shape[:-2] + (MLA_HEADS * LANES,))


def _even_weights(w_in, conv_w, conv_b, dt_bias, a_log, d_ssm, ssm_norm_w, q_a_norm_w, w_qb, kv_a_norm_w, w_kvb,
                  q_norm_w, k_norm_w, w_out):
    k = w_in.shape[0]
    o = 0
    parts = {}
    for name, sz in (("z", SSM_D_INNER), ("xbc", SSM_CONV_CH), ("dt", SSM_HEADS), ("ql", Q_LORA), ("kvl", KV_LORA),
                     ("kpe", QK_ROPE), ("g", MLA_WIDTH)):
        parts[name] = w_in[:, o:o + sz]
        o += sz
    zc = lambda n: jnp.zeros((k, n), F32)
    kpe_blk = jnp.concatenate([zc(QK_NOPE), parts["kpe"], zc(LANES - QK_DIM)], axis=1)
    kpesw_blk = jnp.concatenate([zc(QK_NOPE), _rot_half_cols(parts["kpe"]), zc(LANES - QK_DIM)], axis=1)
    dt_blk = jnp.concatenate([parts["dt"], zc(LANES - SSM_HEADS)], axis=1)
    w_in_p = jnp.concatenate([parts["xbc"], parts["z"], parts["g"], parts["ql"], parts["kvl"], kpe_blk, kpesw_blk,
                              dt_blk, zc(N_EVEN_PAD - C_DT - LANES)], axis=1).astype(BF16)
    wq = w_qb.reshape(Q_LORA, MLA_HEADS, QK_DIM)
    zq = jnp.zeros((Q_LORA, MLA_HEADS, QK_NOPE), F32)
    wkv = w_kvb.reshape(KV_LORA, MLA_HEADS, QK_NOPE + V_DIM)
    w_nope = wkv[..., :QK_NOPE]
    pad1 = lambda v, n: jnp.concatenate([v, jnp.zeros((n - v.shape[0],), F32)]).reshape(1, n)
    norm_pat = lambda w: jnp.concatenate([w, jnp.zeros((LANES - QK_DIM,), F32)]).reshape(1, LANES)
    w_abs = jnp.concatenate([jnp.transpose(w_nope, (1, 2, 0)),
                             jnp.zeros((MLA_HEADS, LANES - QK_NOPE, KV_LORA), F32)], axis=1)
    e_pe = jnp.zeros((LANES, LANES), F32).at[QK_NOPE + jnp.arange(QK_ROPE), jnp.arange(QK_ROPE)].set(1.0)
    head_expand = (jnp.arange(LANES)[:, None] == (jnp.arange(SSM_D_INNER)[None, :] // SSM_HEAD_DIM))
    return {
        "w_in": w_in_p,
        "conv_w": conv_w, "conv_b": conv_b.reshape(1, -1),
        "dt_bias": pad1(dt_bias, LANES), "a_log": pad1(a_log, LANES),
        "d_exp": jnp.repeat(d_ssm, SSM_HEAD_DIM).reshape(1, -1),
        "ssm_norm_w": ssm_norm_w.reshape(1, -1),
        "head_expand": head_expand.astype(BF16),
        "q_a_norm_w": q_a_norm_w.reshape(1, -1),
        "w_q": _head_pad(wq[..., :QK_NOPE], wq[..., QK_NOPE:]).astype(BF16),
        "w_qsw": _head_pad(zq, _rot_half_cols(wq[..., QK_NOPE:])).astype(BF16),
        "q_norm_pat": norm_pat(q_norm_w),
        "kv_a_norm_w": kv_a_norm_w.reshape(1, -1),
        "w_k": _head_pad(w_nope, jnp.zeros((KV_LORA, MLA_HEADS, QK_ROPE), F32)).astype(BF16),
        "w_v": wkv[..., QK_NOPE:].reshape(KV_LORA, MLA_WIDTH).astype(BF16),
        "w_v_t": wkv[..., QK_NOPE:].reshape(KV_LORA, MLA_WIDTH).T.astype(BF16),
        "k_norm_pat": norm_pat(k_norm_w),
        "w_abs": w_abs.astype(BF16),
        "e_pe": e_pe.astype(BF16),
        "w_nope_t": jnp.transpose(w_nope, (1, 2, 0)).reshape(MLA_HEADS * QK_NOPE, KV_LORA).astype(BF16),
        "w_out": w_out.astype(BF16),
    }


def _rope_tables(pos):
    half = QK_ROPE // 2
    inv_freq = jnp.power(ROPE_THETA, -jnp.arange(half, dtype=F32) / half)
    ang = pos[:, None] * inv_freq[None, :]
    c, s = jnp.cos(ang), jnp.sin(ang)
    n = pos.shape[0]
    cos_t = jnp.concatenate([jnp.ones((n, QK_NOPE), F32), c, c, jnp.ones((n, LANES - QK_DIM), F32)], axis=1)
    sin_t = jnp.concatenate([jnp.zeros((n, QK_NOPE), F32), s, s, jnp.zeros((n, LANES - QK_DIM), F32)], axis=1)
    return cos_t, sin_t


def _odd_weights(w_in, a_re, a_im, b_re, b_im, c_re, c_im, d, log_step, w_glu, b_glu, w_out):
    ar, ai = a_re.astype(F32), a_im.astype(F32)
    step = jnp.exp(log_step.astype(F32))[:, None]
    mag = jnp.exp(ar * step)
    ab_re, ab_im = mag * jnp.cos(ai * step), mag * jnp.sin(ai * step)
    den = ar * ar + ai * ai
    nr, ni = ab_re - 1.0, ab_im
    f_re = (nr * ar + ni * ai) / den
    f_im = (ni * ar - nr * ai) / den
    bb_re = f_re[..., None] * b_re - f_im[..., None] * b_im
    bb_im = f_re[..., None] * b_im + f_im[..., None] * b_re
    gl = LANES // S5_GROUP_CH

    def b_blocks(bb):
        x = jnp.transpose(bb.reshape(S5_SLABS, gl, S5_STATE, S5_GROUP_CH), (0, 1, 3, 2))
        eye = jnp.eye(gl, dtype=F32)
        return jnp.einsum("sgcn,gh->sgchn", x, eye).reshape(S5_SLABS, LANES, S5_SLAB_STATE)

    def c_blocks(cc):
        x = jnp.transpose(cc.reshape(S5_SLABS, gl, S5_GROUP_CH, S5_STATE), (0, 1, 3, 2))
        eye = jnp.eye(gl, dtype=F32)
        return jnp.einsum("sgnc,gh->sgnhc", x, eye).reshape(S5_SLABS, S5_SLAB_STATE, LANES)

    def k_blocks(kk):
        x = jnp.transpose(kk.reshape(S5_SLABS, gl, S5_GROUP_CH, S5_GROUP_CH), (0, 1, 3, 2))
        eye = jnp.eye(gl, dtype=F32)
        return jnp.einsum("sgdc,gh->sgdhc", x, eye).reshape(S5_SLABS, LANES, LANES)

    def powers(base_re, base_im, n):
        pr, pi = jnp.ones_like(base_re), jnp.zeros_like(base_im)
        out = []
        for _ in range(n):
            pr, pi = pr * base_re - pi * base_im, pr * base_im + pi * base_re
            out.append((pr, pi))
        return out

    pw = powers(ab_re, ab_im, SUBLANES)
    pw0 = [(jnp.ones_like(ab_re), jnp.zeros_like(ab_im))] + pw
    cr, ci = c_re.astype(F32), c_im.astype(F32)
    bst_re, bst_im, kst, mrows = [], [], [], []
    for j in range(SUBLANES):
        pr, pi = pw0[j]
        bj_re = pr[..., None] * bb_re - pi[..., None] * bb_im
        bj_im = pr[..., None] * bb_im + pi[..., None] * bb_re
        bst_re.append(b_blocks(bj_re))
        bst_im.append(b_blocks(bj_im))
        hi = lax.Precision.HIGHEST
        kst.append(k_blocks(jnp.einsum("gcn,gnd->gcd", cr, bj_re, precision=hi)
                            - jnp.einsum("gcn,gnd->gcd", ci, bj_im, precision=hi)))
        qr, qi = pw[j]
        mrows.append(jnp.concatenate([c_blocks(cr * qr[:, None, :] - ci * qi[:, None, :]),
                                      c_blocks(-(cr * qi[:, None, :] + ci * qr[:, None, :]))], axis=1))
    b_stack = jnp.concatenate([jnp.concatenate(bst_re, axis=1), jnp.concatenate(bst_im, axis=1)], axis=2)
    pw8 = powers(pw[-1][0], pw[-1][1], SUBLANES)
    slab = lambda x: x.reshape(S5_SLABS, S5_SLAB_STATE)
    rows = jnp.arange(SUBLANES)[None, :, None]
    hs8_re = jnp.stack([jnp.where(rows >= sh, slab(pw8[sh - 1][0])[:, None, :], 0.0) for sh in (1, 2, 4)], axis=1)
    hs8_im = jnp.stack([jnp.where(rows >= sh, slab(pw8[sh - 1][1])[:, None, :], 0.0) for sh in (1, 2, 4)], axis=1)
    p8_re = jnp.stack([slab(p[0]) for p in pw8], axis=1)
    p8_im = jnp.stack([slab(p[1]) for p in pw8], axis=1)
    bre_f, bim_f = b_blocks(bb_re), b_blocks(bb_im)
    bre16, bim16 = bre_f.astype(BF16), bim_f.astype(BF16)
    return {
        "w_in": w_in.astype(BF16),
        "b_re": bre16, "b_im": bim16,
        "b_re_lo": (bre_f - bre16.astype(F32)).astype(BF16), "b_im_lo": (bim_f - bim16.astype(F32)).astype(BF16),
        "c_re": c_blocks(c_re.astype(F32)).astype(BF16), "c_im": c_blocks(-c_im.astype(F32)).astype(BF16),
        "d": d.reshape(1, -1),
        "k_stack": jnp.concatenate(kst, axis=1).astype(BF16), "b_stack": b_stack.astype(BF16),
        "m_rows": jnp.stack(mrows, axis=1).astype(BF16),
        "a8_re_hs": hs8_re, "a8_im_hs": hs8_im, "p8_re": p8_re, "p8_im": p8_im,
        "a_re1": slab(ab_re).reshape(S5_SLABS, 1, S5_SLAB_STATE),
        "a_im1": slab(ab_im).reshape(S5_SLABS, 1, S5_SLAB_STATE),
        "w_glu": w_glu.astype(BF16), "b_glu": b_glu.reshape(1, -1), "w_out": w_out.astype(BF16),
    }


SAMPLE_ROWS = 8


def _even_layer(hp, hs, norm_w, lw, tabs, state_ssm, state_conv, pool_c, pool_pt, layer, page_table, bp, sp, bs, ss):
    cos_p, sin_p, cos_s, sin_s = tabs
    hist = SSM_CONV - 1
    proj_p = _inproj(hp, norm_w, lw["w_in"])
    proj_s = _inproj(hs, norm_w, lw["w_in"])
    nchunks = sp // SSM_CHUNK
    zero_state = jnp.zeros((bp, SSM_D_INNER, SSM_STATE), F32)
    y_p, ssm_p = _ssd(proj_p, proj_p, proj_p, zero_state, lw, nblk=bp * nchunks, nchunks=nchunks, nseq=1,
                      P=SSM_CHUNK, n_hist=0, n_real=SSM_CHUNK, carry=True,
                      xbc_col=0, z_col=C_Z // SSM_D_INNER, dt_col=C_DT // LANES)
    conv_p = proj_p.reshape(bp, sp, N_EVEN_PAD)[:, sp - hist:, C_XBC:C_XBC + SSM_CONV_CH]
    xbc_s = proj_s[:, C_XBC:C_XBC + SSM_CONV_CH].reshape(bs, ss, SSM_CONV_CH)
    xbc_full = jnp.concatenate([state_conv, xbc_s], axis=1)
    pad_rows = SAMPLE_ROWS - hist - ss
    pad3 = lambda a: jnp.pad(a, ((0, 0), (hist, pad_rows), (0, 0)))
    xbc_pad = jnp.pad(xbc_full, ((0, 0), (0, pad_rows), (0, 0))).reshape(bs * SAMPLE_ROWS, SSM_CONV_CH)
    z_pad = pad3(proj_s[:, C_Z:C_Z + SSM_D_INNER].reshape(bs, ss, -1)).reshape(bs * SAMPLE_ROWS, -1)
    dt_pad = pad3(proj_s[:, C_DT:C_DT + LANES].reshape(bs, ss, -1)).reshape(bs * SAMPLE_ROWS, -1)
    nseq = SSM_CHUNK // SAMPLE_ROWS
    y_s_pad, ssm_s = _ssd(xbc_pad, z_pad, dt_pad, state_ssm.reshape(-1, SSM_D_INNER, SSM_STATE), lw,
                          nblk=bs // nseq, nchunks=1, nseq=nseq, P=SAMPLE_ROWS, n_hist=hist, n_real=ss,
                          carry=False, xbc_col=0, z_col=0, dt_col=0, h0_offset=layer * bs)
    y_s = y_s_pad.reshape(bs, SAMPLE_ROWS, SSM_D_INNER)[:, hist:hist + ss].reshape(bs * ss, SSM_D_INNER)
    conv_s = xbc_full[:, ss:]
    q_p, ckv_p, kper_p, k_p, v_p = _mla_proj(proj_p, cos_p, sin_p, lw, with_kv=True, tm=FLASH_TK)
    q_s, ckv_s, kper_s = _mla_proj(proj_s, cos_s, sin_s, lw, with_kv=False)
    o_p = _flash(q_p, k_p, v_p, batch=bp, seq=sp)
    kpe_p = kper_p[:, QK_NOPE:QK_DIM]
    kpe_s = kper_s[:, QK_NOPE:QK_DIM]
    qabs, qpe = _qabs(q_s, lw)
    qabs = qabs.reshape(bs, ss * MLA_HEADS, KV_LORA)
    qpe = qpe.reshape(bs, ss * MLA_HEADS, LANES)[:, :, :QK_ROPE]
    cnew = jnp.pad(ckv_s.reshape(bs, ss, KV_LORA), ((0, 0), (0, PAGE - ss), (0, 0)))
    pnew_t = jnp.pad(jnp.swapaxes(kpe_s.reshape(bs, ss, QK_ROPE), 1, 2), ((0, 0), (0, 0), (0, PAGE - ss)))
    o_s = _decode_attn(page_table, pool_c, pool_pt, layer, cnew, pnew_t, qabs, qpe, lw).reshape(bs * ss, MLA_WIDTH)
    hp = _outproj_even(y_p, o_p, proj_p, hp, lw["w_out"])
    hs = _outproj_even(y_s, o_s, proj_s, hs, lw["w_out"])
    outs = (ckv_p.reshape(bp, sp, KV_LORA), kpe_p.reshape(bp, sp, QK_ROPE), ckv_s.reshape(bs, ss, KV_LORA),
            kpe_s.reshape(bs, ss, QK_ROPE), ssm_p.reshape(bp, SSM_HEADS, SSM_HEAD_DIM, SSM_STATE),
            ssm_s.reshape(bs, SSM_HEADS, SSM_HEAD_DIM, SSM_STATE), conv_p, conv_s)
    return hp, hs, outs


def _odd_layer(hp, hs, norm_w, lw, s5_re, s5_im, bp, sp, bs, ss):
    proj_p = _inproj(hp, norm_w, lw["w_in"])
    proj_s = _inproj(hs, norm_w, lw["w_in"])
    y_p, r_p, i_p = _s5_seq(proj_p, lw, batch=bp, seq=sp)
    u_t = jnp.transpose(proj_s[:, :S5_WIDTH].reshape(bs, ss, S5_WIDTH), (1, 0, 2))
    y_t, r_s, i_s = _s5_step(u_t, s5_re.reshape(bs, -1), s5_im.reshape(bs, -1), lw)
    y_s = jnp.transpose(y_t, (1, 0, 2)).reshape(bs * ss, S5_WIDTH)
    hp = _odd_tail(y_p, proj_p, hp, lw)
    hs = _odd_tail(y_s, proj_s, hs, lw)
    shp = lambda a, b: a.reshape(b, S5_GROUPS, S5_STATE)
    return hp, hs, (shp(r_p, bp), shp(i_p, bp), shp(r_s, bs), shp(i_s, bs))


def kernel(x_prompt, x_sample, cache_ckv, cache_kpe, page_table, state_ssm, state_conv, state_s5_re, state_s5_im, norm_w, w_in_even, conv_w, conv_b, dt_bias, a_log, d_ssm, ssm_norm_w, q_a_norm_w, w_qb, kv_a_norm_w, w_kvb, q_norm_w, k_norm_w, w_out_even, w_in_odd, s5_a_re, s5_a_im, s5_b_re, s5_b_im, s5_c_re, s5_c_im, s5_d, s5_log_step, w_glu, b_glu, w_out_odd):
    bp, sp, _ = x_prompt.shape
    bs, ss, _ = x_sample.shape
    past_len = page_table.shape[1] * PAGE
    depth = norm_w.shape[0]
    pos_p = jnp.tile(jnp.arange(sp, dtype=F32), bp)
    pos_s = jnp.tile(past_len + jnp.arange(ss, dtype=F32), bs)
    tabs = _rope_tables(pos_p) + _rope_tables(pos_s)
    cache_kpe_t = jnp.swapaxes(cache_kpe, 2, 3)
    hp = x_prompt.reshape(bp * sp, D_MODEL)
    hs = x_sample.reshape(bs * ss, D_MODEL)
    even_out, odd_out = [], []
    for i in range(depth):
        j = i // 2
        if i % 2 == 0:
            lw = _even_weights(w_in_even[j], conv_w[j], conv_b[j], dt_bias[j], a_log[j], d_ssm[j], ssm_norm_w[j],
                               q_a_norm_w[j], w_qb[j], kv_a_norm_w[j], w_kvb[j], q_norm_w[j], k_norm_w[j],
                               w_out_even[j])
            hp, hs, o = _even_layer(hp, hs, norm_w[i], lw, tabs, state_ssm, state_conv[j], cache_ckv,
                                    cache_kpe_t, j, page_table, bp, sp, bs, ss)
            even_out.append(o)
        else:
            lw = _odd_weights(w_in_odd[j], s5_a_re[j], s5_a_im[j], s5_b_re[j], s5_b_im[j], s5_c_re[j], s5_c_im[j],
                              s5_d[j], s5_log_step[j], w_glu[j], b_glu[j], w_out_odd[j])
            hp, hs, o = _odd_layer(hp, hs, norm_w[i], lw, state_s5_re[j], state_s5_im[j], bp, sp, bs, ss)
            odd_out.append(o)
    ev = [jnp.stack([o[k] for o in even_out]) for k in range(8)]
    od = [jnp.stack([o[k] for o in odd_out]) for k in range(4)]
    return (hp.reshape(bp, sp, D_MODEL), hs.reshape(bs, ss, D_MODEL),
            ev[0], ev[1], ev[2], ev[3], ev[4], ev[5], ev[6], ev[7], od[0], od[1], od[2], od[3])
```

```python
import functools
import math

import jax
import jax.numpy as jnp
from jax import lax
from jax.experimental import pallas as pl
from jax.experimental.pallas import tpu as pltpu

F32 = jnp.float32
BF16 = jnp.bfloat16
EPS = 1e-6

D_MODEL = 1024
LANES = 128
SUBLANES = 8
VMEM_LIMIT = 48 * 1024 * 1024

SSM_D_INNER = 1024
SSM_HEAD_DIM = 64
SSM_HEADS = 16
SSM_GROUPS = 4
SSM_STATE = 128
SSM_CONV = 4
SSM_CHUNK = 128
SSM_GN = SSM_GROUPS * SSM_STATE
SSM_CONV_CH = SSM_D_INNER + 2 * SSM_GN

MLA_HEADS = 16
Q_LORA = 256
KV_LORA = 256
QK_NOPE = 64
QK_ROPE = 32
QK_DIM = QK_NOPE + QK_ROPE
V_DIM = 64
MLA_WIDTH = MLA_HEADS * V_DIM
ROPE_THETA = 10000.0
PAGE = 128
FLASH_TK = 512
FLASH_TN = 256
FLASH_DEPTH = 6
FLASH_LROWS = 16

S5_WIDTH = 1024
S5_GROUP_CH = 16
S5_GROUPS = 64
S5_STATE = 64
S5_SLABS = S5_WIDTH // LANES
S5_SLAB_STATE = (LANES // S5_GROUP_CH) * S5_STATE

C_XBC, C_Z, C_G, C_QL, C_KVL, C_KPE, C_KPESW, C_DT, N_EVEN_PAD = 0, 2048, 3072, 4096, 4352, 4608, 4736, 4864, 5120


def _cparams(*sem):
    return pltpu.CompilerParams(dimension_semantics=sem, vmem_limit_bytes=VMEM_LIMIT)


def _silu(x):
    return x * (1.0 / (1.0 + jnp.exp(-x)))


def _dot(a, b):
    return jnp.dot(a, b, preferred_element_type=F32)


def _dot_nt(a, b):
    return lax.dot_general(a, b, (((1,), (1,)), ((), ())), preferred_element_type=F32)


def _split3(v):
    hi = v.astype(BF16)
    r1 = v - hi.astype(F32)
    mid = r1.astype(BF16)
    lo = (r1 - mid.astype(F32)).astype(BF16)
    return hi, mid, lo


def _dot_exact_lhs(m_bf16, v):
    hi, mid, lo = _split3(v)
    return _dot(m_bf16, hi) + _dot(m_bf16, mid) + _dot(m_bf16, lo)


def _dot_exact_rhs(v, m_bf16):
    hi, mid, lo = _split3(v)
    return _dot(hi, m_bf16) + _dot(mid, m_bf16) + _dot(lo, m_bf16)


def _inproj_body(x_ref, nw_ref, w_ref, o_ref, xn_ref):
    @pl.when(pl.program_id(1) == 0)
    def _():
        x = x_ref[...]
        ms = jnp.mean(x * x, axis=-1, keepdims=True)
        xn_ref[...] = (x * lax.rsqrt(ms + EPS) * nw_ref[...]).astype(BF16)

    o_ref[...] = _dot(xn_ref[...], w_ref[...])


def _inproj(x, norm_w, w, tm=512, tn=1024):
    m, k = x.shape
    n = w.shape[1]
    tm = min(tm, m)
    return pl.pallas_call(
        _inproj_body,
        grid=(m // tm, n // tn),
        in_specs=[
            pl.BlockSpec((tm, k), lambda i, j: (i, 0)),
            pl.BlockSpec((1, k), lambda i, j: (0, 0)),
            pl.BlockSpec((k, tn), lambda i, j: (0, j)),
        ],
        out_specs=pl.BlockSpec((tm, tn), lambda i, j: (i, j)),
        out_shape=jax.ShapeDtypeStruct((m, n), F32),
        scratch_shapes=[pltpu.VMEM((tm, k), BF16)],
        compiler_params=_cparams("parallel", "arbitrary"),
        name="inproj",
    )(x, norm_w.reshape(1, k), w)


def _ssd_body(xbc_ref, z_ref, dt_ref, h0_ref, convw_ref, convb_ref, dtb_ref, alog_ref, dexp_ref, nw_ref, e_ref,
              y_ref, hout_ref,
              xp_ref, ysc_ref, xwt_ref, b_ref, c_ref, cse_ref, tott_ref,
              *, L, P, n_hist, n_real, nseq, nchunks, carry):
    blk = pl.program_id(0)
    s = pl.program_id(1)
    chunk = blk % nchunks
    HIST = SUBLANES

    @pl.when(s == 0)
    def _intra():
        if carry:
            @pl.when(chunk == 0)
            def _():
                xp_ref[0:HIST, :] = jnp.zeros((HIST, SSM_CONV_CH), F32)
        else:
            xp_ref[0:HIST, :] = jnp.zeros((HIST, SSM_CONV_CH), F32)
        xp_ref[HIST:HIST + L, :] = xbc_ref[...]
        conv = convb_ref[...] + convw_ref[3:4, :] * xp_ref[HIST:HIST + L, :]
        for k in range(SSM_CONV - 1):
            off = HIST - (SSM_CONV - 1) + k
            conv = conv + convw_ref[k:k + 1, :] * xp_ref[off:off + L, :]
        if carry:
            xp_ref[0:HIST, :] = xp_ref[L:L + HIST, :]
        xc = _silu(conv)
        xs = xc[:, :SSM_D_INNER]
        b_ref[...] = xc[:, SSM_D_INNER:SSM_D_INNER + SSM_GN].astype(BF16)
        c_ref[...] = xc[:, SSM_D_INNER + SSM_GN:].astype(BF16)

        raw = dt_ref[...] + dtb_ref[...]
        dt = jnp.maximum(raw, 0.0) + jnp.log1p(jnp.exp(-jnp.abs(raw)))
        ri = lax.broadcasted_iota(jnp.int32, (L, L), 0)
        ci = lax.broadcasted_iota(jnp.int32, (L, L), 1)
        if P < L:
            rp = lax.broadcasted_iota(jnp.int32, (L, LANES), 0) % P
            dt = jnp.where((rp >= n_hist) & (rp < n_hist + n_real), dt, 0.0)
            same = (ri // P) == (ci // P)
            causal = same & (ci <= ri)
        else:
            same = ri >= 0
            causal = ci <= ri
        a_neg = -jnp.exp(alog_ref[...])
        da = dt * a_neg
        m_cum = jnp.where(causal, 1.0, 0.0).astype(BF16)
        m_tot = jnp.where(same, 1.0, 0.0).astype(BF16)
        cs = _dot_exact_lhs(m_cum, da)
        tot = _dot_exact_lhs(m_tot, da)
        cst = cs.T
        e = e_ref[...]
        dt_e = _dot_exact_rhs(dt, e)
        cs_e = _dot_exact_rhs(cs, e)
        tot_e = _dot_exact_rhs(tot, e)
        cse_ref[...] = cs_e
        tott_ref[...] = tot_e.T
        xdt = xs * dt_e
        xw = xdt * jnp.exp(tot_e - cs_e)
        xwt_ref[...] = xw.T.astype(BF16)
        xdt16 = xdt.astype(BF16)
        lane = lax.broadcasted_iota(jnp.int32, (L, LANES), 1)
        for g in range(SSM_GROUPS):
            cb = _dot_nt(c_ref[:, g * SSM_STATE:(g + 1) * SSM_STATE], b_ref[:, g * SSM_STATE:(g + 1) * SSM_STATE])
            for pr in range(2):
                h0i = 4 * g + 2 * pr
                col = h0i // 2
                xpair = xdt16[:, col * LANES:(col + 1) * LANES]
                ys = []
                for hh in (h0i, h0i + 1):
                    dec = jnp.exp(jnp.where(causal, cs[:, hh:hh + 1] - cst[hh:hh + 1, :], -1e30))
                    ys.append(_dot((cb * dec).astype(BF16), xpair))
                ypair = jnp.where(lane < SSM_HEAD_DIM, ys[0], ys[1])
                ysc_ref[:, col * LANES:(col + 1) * LANES] = (
                    ypair + dexp_ref[:, col * LANES:(col + 1) * LANES] * xs[:, col * LANES:(col + 1) * LANES])

    if carry:
        @pl.when(chunk == 0)
        def _():
            hout_ref[...] = h0_ref[...]
    else:
        hout_ref[...] = h0_ref[...]

    if nseq > 1:
        rmask = (lax.broadcasted_iota(jnp.int32, (L, LANES), 0) // P) == s
        cmask = (lax.broadcasted_iota(jnp.int32, (LANES, L), 1) // P) == s
        onehot = lax.broadcasted_iota(jnp.int32, (LANES, L), 1) == s * P
    for col in range(SSM_HEADS // 2):
        g = col // 2
        sp = hout_ref[col * LANES:(col + 1) * LANES, :]
        yoff = _dot_nt(c_ref[:, g * SSM_STATE:(g + 1) * SSM_STATE], sp.astype(BF16))
        yoff = yoff * jnp.exp(cse_ref[:, col * LANES:(col + 1) * LANES])
        xwt = xwt_ref[col * LANES:(col + 1) * LANES, :]
        tott = tott_ref[col * LANES:(col + 1) * LANES, :]
        if nseq > 1:
            yoff = jnp.where(rmask, yoff, 0.0)
            xwt = jnp.where(cmask, xwt, jnp.zeros_like(xwt))
            deccol = jnp.exp(jnp.sum(jnp.where(onehot, tott, 0.0), axis=1, keepdims=True))
        else:
            deccol = jnp.exp(tott[:, 0:1])
        ysc_ref[:, col * LANES:(col + 1) * LANES] += yoff
        hout_ref[col * LANES:(col + 1) * LANES, :] = sp * deccol + _dot(xwt, b_ref[:, g * SSM_STATE:(g + 1) * SSM_STATE])

    @pl.when(s == nseq - 1)
    def _epilogue():
        gw = SSM_D_INNER // SSM_GROUPS
        for g in range(SSM_GROUPS):
            y = ysc_ref[:, g * gw:(g + 1) * gw] * _silu(z_ref[:, g * gw:(g + 1) * gw])
            ms = jnp.mean(y * y, axis=-1, keepdims=True)
            y_ref[:, g * gw:(g + 1) * gw] = y * lax.rsqrt(ms + EPS) * nw_ref[:, g * gw:(g + 1) * gw]


def _ssd(xbc_src, z_src, dt_src, h0, lw, *, nblk, nchunks, nseq, P, n_hist, n_real, carry,
         xbc_col, z_col, dt_col, h0_offset=0):
    L = SSM_CHUNK
    if carry:
        nstate = nblk // nchunks
        state_idx = lambda b, s: (b // nchunks, 0, 0)
    else:
        nstate = nblk * nseq
        state_idx = lambda b, s: (b * nseq + s, 0, 0)
    h0_idx = lambda b, s: (h0_offset + state_idx(b, s)[0], 0, 0)
    full = lambda shape: pl.BlockSpec(shape, lambda b, s: (0,) * len(shape))
    body = functools.partial(_ssd_body, L=L, P=P, n_hist=n_hist, n_real=n_real, nseq=nseq, nchunks=nchunks,
                             carry=carry)
    return pl.pallas_call(
        body,
        grid=(nblk, nseq),
        in_specs=[
            pl.BlockSpec((L, SSM_CONV_CH), lambda b, s: (b, xbc_col)),
            pl.BlockSpec((L, SSM_D_INNER), lambda b, s: (b, z_col)),
            pl.BlockSpec((L, LANES), lambda b, s: (b, dt_col)),
            pl.BlockSpec((None, SSM_D_INNER, SSM_STATE), h0_idx),
            full((SSM_CONV, SSM_CONV_CH)),
            full((1, SSM_CONV_CH)),
            full((1, LANES)),
            full((1, LANES)),
            full((1, SSM_D_INNER)),
            full((1, SSM_D_INNER)),
            full((LANES, SSM_D_INNER)),
        ],
        out_specs=[
            pl.BlockSpec((L, SSM_D_INNER), lambda b, s: (b, 0)),
            pl.BlockSpec((None, SSM_D_INNER, SSM_STATE), state_idx),
        ],
        out_shape=[
            jax.ShapeDtypeStruct((nblk * L, SSM_D_INNER), F32),
            jax.ShapeDtypeStruct((nstate, SSM_D_INNER, SSM_STATE), F32),
        ],
        scratch_shapes=[
            pltpu.VMEM((L + SUBLANES, SSM_CONV_CH), F32),
            pltpu.VMEM((L, SSM_D_INNER), F32),
            pltpu.VMEM((SSM_D_INNER, L), BF16),
            pltpu.VMEM((L, SSM_GN), BF16),
            pltpu.VMEM((L, SSM_GN), BF16),
            pltpu.VMEM((L, SSM_D_INNER), F32),
            pltpu.VMEM((SSM_D_INNER, L), F32),
        ],
        compiler_params=_cparams("arbitrary", "arbitrary"),
        name="ssd",
    )(xbc_src, z_src, dt_src, h0, lw["conv_w"], lw["conv_b"], lw["dt_bias"], lw["a_log"], lw["d_exp"],
      lw["ssm_norm_w"], lw["head_expand"])


def _mla_proj_body(ql_ref, kvl_ref, kpe_ref, kpesw_ref, cos_ref, sin_ref, qanw_ref, wq_ref, wqsw_ref, qnw_ref,
                   kvanw_ref, wk_ref, wv_ref, knw_ref,
                   q_ref, ckv_ref, kper_ref, *kv_refs, with_kv):
    cos = cos_ref[...]
    sin = sin_ref[...]
    ql = ql_ref[...]
    qn = (ql * lax.rsqrt(jnp.mean(ql * ql, axis=-1, keepdims=True) + EPS) * qanw_ref[...]).astype(BF16)
    q0 = _dot(qn, wq_ref[...])
    q1 = _dot(qn, wqsw_ref[...])
    scale = QK_DIM ** -0.5 * math.log2(math.e)
    for h in range(MLA_HEADS):
        sl = slice(h * LANES, (h + 1) * LANES)
        qh = q0[:, sl] * cos + q1[:, sl] * sin
        ms = jnp.sum(qh * qh, axis=-1, keepdims=True) * (1.0 / QK_DIM)
        q_ref[:, sl] = (qh * lax.rsqrt(ms + EPS) * (qnw_ref[...] * scale)).astype(BF16)
    kvl = kvl_ref[...]
    ckv = kvl * lax.rsqrt(jnp.mean(kvl * kvl, axis=-1, keepdims=True) + EPS) * kvanw_ref[...]
    ckv_ref[...] = ckv
    kper = kpe_ref[...] * cos + kpesw_ref[...] * sin
    kper_ref[...] = kper
    if with_kv:
        k_ref, v_ref = kv_refs
        c16 = ckv.astype(BF16)
        kn = _dot(c16, wk_ref[...])
        for h in range(MLA_HEADS):
            sl = slice(h * LANES, (h + 1) * LANES)
            kh = kn[:, sl] + kper
            ms = jnp.sum(kh * kh, axis=-1, keepdims=True) * (1.0 / QK_DIM)
            k_ref[:, sl] = (kh * lax.rsqrt(ms + EPS) * knw_ref[...]).astype(BF16)
        v_ref[...] = _dot_nt(wv_ref[...], c16).astype(BF16)


def _mla_proj(proj, cos_t, sin_t, lw, *, with_kv, tm=256):
    m = proj.shape[0]
    tm = min(tm, m)
    hp = MLA_HEADS * LANES
    full = lambda shape: pl.BlockSpec(shape, lambda i: (0,) * len(shape))
    row = lambda w, c: pl.BlockSpec((tm, w), lambda i: (i, c))
    out_specs = [row(hp, 0), row(KV_LORA, 0), row(LANES, 0)]
    out_shape = [jax.ShapeDtypeStruct((m, hp), BF16), jax.ShapeDtypeStruct((m, KV_LORA), F32),
                 jax.ShapeDtypeStruct((m, LANES), F32)]
    if with_kv:
        out_specs += [row(hp, 0), pl.BlockSpec((None, MLA_WIDTH, tm), lambda i: (i, 0, 0))]
        out_shape += [jax.ShapeDtypeStruct((m, hp), BF16), jax.ShapeDtypeStruct((m // tm, MLA_WIDTH, tm), BF16)]
    return pl.pallas_call(
        functools.partial(_mla_proj_body, with_kv=with_kv),
        grid=(m // tm,),
        in_specs=[
            row(Q_LORA, C_QL // Q_LORA), row(KV_LORA, C_KVL // KV_LORA), row(LANES, C_KPE // LANES),
            row(LANES, C_KPESW // LANES), row(LANES, 0), row(LANES, 0),
            full((1, Q_LORA)), full((Q_LORA, hp)), full((Q_LORA, hp)), full((1, LANES)),
            full((1, KV_LORA)), full((KV_LORA, hp)), full((MLA_WIDTH, KV_LORA)), full((1, LANES)),
        ],
        out_specs=out_specs,
        out_shape=out_shape,
        compiler_params=_cparams("parallel"),
        name="mla_proj",
    )(proj, proj, proj, proj, cos_t, sin_t, lw["q_a_norm_w"], lw["w_q"], lw["w_qsw"], lw["q_norm_pat"],
      lw["kv_a_norm_w"], lw["w_k"], lw["w_v_t"], lw["k_norm_pat"])


def _flash_body(q_ref, k_ref, vt_ref, o_ref, m_ref, acc_ref, *, tq, tk):
    qi = pl.program_id(2)
    nfull = qi * (tq // tk)
    m_ref[...] = jnp.full(m_ref.shape, -jnp.inf, F32)
    acc_ref[...] = jnp.zeros(acc_ref.shape, F32)
    ones = jnp.ones((FLASH_LROWS, tk), BF16)

    tn = min(FLASH_TN, tq)

    def steps(kis, masked):
        chains = [(ki, nt, j) for ki in kis for nt in range(tq // tn) for j in range(2)]

        def qk(ch):
            ki, nt, j = ch
            koff = pl.multiple_of(ki * tk, tk)
            q = q_ref[nt * tn:(nt + 1) * tn, j * LANES:(j + 1) * LANES]
            k = k_ref[pl.ds(koff, tk), j * LANES:(j + 1) * LANES]
            return _dot_nt(k, q)

        def softmax(ch, st):
            ki, nt, j = ch
            cols = slice(nt * tn, (nt + 1) * tn)
            if masked:
                kpos = ki * tk + lax.broadcasted_iota(jnp.int32, (tk, tn), 0)
                qpos = qi * tq + nt * tn + lax.broadcasted_iota(jnp.int32, (tk, tn), 1)
                st = jnp.where(kpos <= qpos, st, -jnp.inf)
            m_old = m_ref[j, :, cols]
            m_new = jnp.maximum(m_old, jnp.max(st, axis=0, keepdims=True))
            p = jnp.exp2(st - m_new)
            alpha = jnp.exp2(m_old - m_new)
            m_ref[j, :, cols] = m_new
            lhs = jnp.concatenate([vt_ref[ki, j * V_DIM:(j + 1) * V_DIM, :], ones], axis=0)
            return alpha, _dot(lhs, p.astype(BF16))

        def fold(ch, alpha, pv):
            _, nt, j = ch
            cols = slice(nt * tn, (nt + 1) * tn)
            acc_ref[j, :, cols] = alpha * acc_ref[j, :, cols] + pv

        sts = [qk(c) for c in chains[:FLASH_DEPTH]]
        pending = None
        for i, ch in enumerate(chains):
            st = sts.pop(0)
            if i + FLASH_DEPTH < len(chains):
                sts.append(qk(chains[i + FLASH_DEPTH]))
            alpha, pv = softmax(ch, st)
            if pending is not None:
                fold(*pending)
            pending = (ch, alpha, pv)
        fold(*pending)

    unroll = tq // tk

    def full_steps(it, c):
        steps([it * unroll + u for u in range(unroll)], False)
        return c

    lax.fori_loop(0, qi, full_steps, 0)
    steps([nfull + d for d in range(tq // tk)], True)
    ot = jnp.concatenate([acc_ref[j, 0:V_DIM, :] / acc_ref[j, V_DIM:V_DIM + 1, :] for j in range(2)], axis=0)
    o_ref[...] = ot.T


def _flash(q, k, vt, *, batch, seq, tq=1024):
    tk = vt.shape[2]
    tq = min(tq, seq)
    nq = seq // tq
    nk = seq // tk
    return pl.pallas_call(
        functools.partial(_flash_body, tq=tq, tk=tk),
        grid=(batch, MLA_HEADS // 2, nq),
        in_specs=[
            pl.BlockSpec((tq, 2 * LANES), lambda b, h, i: (b * nq + i, h)),
            pl.BlockSpec((seq, 2 * LANES), lambda b, h, i: (b, h)),
            pl.BlockSpec((nk, 2 * V_DIM, tk), lambda b, h, i: (b, h, 0)),
        ],
        out_specs=pl.BlockSpec((tq, 2 * V_DIM), lambda b, h, i: (b * nq + i, h)),
        out_shape=jax.ShapeDtypeStruct((batch * seq, MLA_WIDTH), F32),
        scratch_shapes=[pltpu.VMEM((2, 1, tq), F32), pltpu.VMEM((2, V_DIM + FLASH_LROWS, tq), F32)],
        compiler_params=_cparams("parallel", "parallel", "arbitrary"),
        name="flash",
    )(q, k, vt)


def _qabs_body(q_ref, knw_ref, wabs_ref, epe_ref, qabs_ref, qpe_ref):
    qh = (q_ref[...].astype(F32) * knw_ref[...]).astype(BF16)
    qabs_ref[...] = _dot(qh, wabs_ref[...]).astype(BF16)
    qpe_ref[...] = _dot(qh, epe_ref[...]).astype(BF16)


def _qabs(q, lw):
    m = q.shape[0]
    return pl.pallas_call(
        _qabs_body,
        grid=(MLA_HEADS,),
        in_specs=[
            pl.BlockSpec((m, LANES), lambda h: (0, h)),
            pl.BlockSpec((1, LANES), lambda h: (0, 0)),
            pl.BlockSpec((None, LANES, KV_LORA), lambda h: (h, 0, 0)),
            pl.BlockSpec((LANES, LANES), lambda h: (0, 0)),
        ],
        out_specs=[pl.BlockSpec((m, KV_LORA), lambda h: (0, h)), pl.BlockSpec((m, LANES), lambda h: (0, h))],
        out_shape=[jax.ShapeDtypeStruct((m, MLA_HEADS * KV_LORA), BF16),
                   jax.ShapeDtypeStruct((m, MLA_HEADS * LANES), BF16)],
        compiler_params=_cparams("parallel"),
        name="qabs",
    )(q, lw["k_norm_pat"], lw["w_abs"], lw["e_pe"])


def _decode_body(pt_ref, *refs, pg, ngroups, nq):
    del pt_ref
    c_refs = refs[:pg]
    p_refs = refs[pg:2 * pg]
    (cnew_ref, pnew_ref, wt_ref, qabs_ref, qpe_ref, wv_ref, o_ref,
     m_ref, l_ref, acc_ref, lhs_ref) = refs[2 * pg:]
    g = pl.program_id(1)
    rows = nq * MLA_HEADS
    nk = MLA_HEADS * QK_NOPE

    @pl.when(g == 0)
    def _():
        m_ref[...] = jnp.full((rows, 1), -jnp.inf, F32)
        l_ref[...] = jnp.zeros((rows, 1), F32)
        acc_ref[...] = jnp.zeros((rows, KV_LORA), F32)
        lhs_ref[0:nk, :] = wt_ref[...]
        lhs_ref[nk:nk + rows, :] = qabs_ref[...]

    def project(c16):
        return _dot_nt(lhs_ref[...], c16)

    def scores(both, kpt):
        t = both.shape[1]
        kt = both[0:nk]
        ssq = jnp.sum((kt * kt).reshape(MLA_HEADS, QK_NOPE, t), axis=1)
        ssq_pe = jnp.sum(kpt * kpt, axis=0, keepdims=True)
        r = lax.rsqrt((ssq + ssq_pe) * (1.0 / QK_DIM) + EPS)
        st = both[nk:nk + rows] + _dot(qpe_ref[...], kpt.astype(BF16))
        return (st.reshape(nq, MLA_HEADS, t) * r[None]).reshape(rows, t)

    def update(st, c16s):
        m_old = m_ref[...]
        m_new = jnp.maximum(m_old, jnp.max(st, axis=-1, keepdims=True))
        p = jnp.exp2(st - m_new)
        alpha = jnp.exp2(m_old - m_new)
        l_ref[...] = alpha * l_ref[...] + jnp.sum(p, axis=-1, keepdims=True)
        p16 = p.astype(BF16)
        pv = None
        off = 0
        for c16 in c16s:
            t = c16.shape[0]
            d = _dot(p16[:, off:off + t], c16)
            pv = d if pv is None else pv + d
            off += t
        acc_ref[...] = alpha * acc_ref[...] + pv
        m_ref[...] = m_new

    def latent(k):
        return jnp.concatenate([c_refs[k][...], c_refs[k + 1][...]], axis=0).astype(BF16)

    subs = list(range(0, pg, 2))
    c16s = [latent(subs[0])]
    boths = [project(c16s[0])]
    sts = []
    for i, k in enumerate(subs):
        if i + 1 < len(subs):
            c16s.append(latent(subs[i + 1]))
            boths.append(project(c16s[-1]))
        kpt = jnp.concatenate([p_refs[k][...], p_refs[k + 1][...]], axis=1)
        sts.append(scores(boths[i], kpt))
    update(jnp.concatenate(sts, axis=1), c16s)

    @pl.when(g == ngroups - 1)
    def _():
        qrow = lax.broadcasted_iota(jnp.int32, (rows, PAGE), 0) // MLA_HEADS
        tok = lax.broadcasted_iota(jnp.int32, (rows, PAGE), 1)
        c16 = cnew_ref[...].astype(BF16)
        st = scores(project(c16), pnew_ref[...])
        update(jnp.where(tok <= qrow, st, -jnp.inf), [c16])
        olat = (acc_ref[...] / l_ref[...]).astype(BF16)
        of = _dot(olat, wv_ref[...])
        rh = lax.broadcasted_iota(jnp.int32, (rows, MLA_WIDTH), 0) % MLA_HEADS
        ch = lax.broadcasted_iota(jnp.int32, (rows, MLA_WIDTH), 1) // V_DIM
        of = jnp.where(rh == ch, of, 0.0)
        o_ref[...] = jnp.sum(of.reshape(nq, MLA_HEADS, MLA_WIDTH), axis=1)


def _decode_attn(page_table, pool_c, pool_pt, layer, cnew, pnew_t, qabs, qpe, lw, *, pg=16):
    nb, npages = page_table.shape
    nq = qabs.shape[1] // MLA_HEADS
    rows = nq * MLA_HEADS
    pg = min(pg, npages)
    ngroups = npages // pg

    def page_spec(shape, k):
        return pl.BlockSpec((None, None) + shape, lambda b, g, pt: (layer, pt[b, g * pg + k], 0, 0))

    per_seq = lambda shape: pl.BlockSpec((None,) + shape, lambda b, g, pt: (b, 0, 0))
    full = lambda shape: pl.BlockSpec(shape, lambda b, g, pt: (0,) * len(shape))
    in_specs = ([page_spec((PAGE, KV_LORA), k) for k in range(pg)]
                + [page_spec((QK_ROPE, PAGE), k) for k in range(pg)]
                + [per_seq((PAGE, KV_LORA)), per_seq((QK_ROPE, PAGE)), full((MLA_HEADS * QK_NOPE, KV_LORA)),
                   per_seq((rows, KV_LORA)), per_seq((rows, QK_ROPE)), full((KV_LORA, MLA_WIDTH))])
    grid_spec = pltpu.PrefetchScalarGridSpec(
        num_scalar_prefetch=1,
        grid=(nb, ngroups),
        in_specs=in_specs,
        out_specs=per_seq((nq, MLA_WIDTH)),
        scratch_shapes=[pltpu.VMEM((rows, 1), F32), pltpu.VMEM((rows, 1), F32), pltpu.VMEM((rows, KV_LORA), F32),
                        pltpu.VMEM((MLA_HEADS * QK_NOPE + rows, KV_LORA), BF16)],
    )
    return pl.pallas_call(
        functools.partial(_decode_body, pg=pg, ngroups=ngroups, nq=nq),
        grid_spec=grid_spec,
        out_shape=jax.ShapeDtypeStruct((nb, nq, MLA_WIDTH), F32),
        compiler_params=_cparams("parallel", "arbitrary"),
        name="decode_attn",
    )(page_table, *([pool_c] * pg), *([pool_pt] * pg), cnew, pnew_t, lw["w_nope_t"], qabs, qpe, lw["w_v"])


def _outproj_even_body(y_ref, o_ref, g_ref, h_ref, wy_ref, wo_ref, out_ref):
    og = (o_ref[...] * _silu(g_ref[...])).astype(BF16)
    out_ref[...] = h_ref[...] + _dot(y_ref[...].astype(BF16), wy_ref[...]) + _dot(og, wo_ref[...])


def _outproj_even(y, o, proj, h, w_out, tm=512):
    m = h.shape[0]
    tm = min(tm, m)
    row = lambda c: pl.BlockSpec((tm, D_MODEL), lambda i: (i, c))
    return pl.pallas_call(
        _outproj_even_body,
        grid=(m // tm,),
        in_specs=[row(0), row(0), row(C_G // D_MODEL), row(0),
                  pl.BlockSpec((SSM_D_INNER, D_MODEL), lambda i: (0, 0)),
                  pl.BlockSpec((MLA_WIDTH, D_MODEL), lambda i: (1, 0))],
        out_specs=row(0),
        out_shape=jax.ShapeDtypeStruct((m, D_MODEL), F32),
        compiler_params=_cparams("parallel"),
        name="outproj_even",
    )(y, o, proj, h, w_out, w_out)


def _gelu(x):
    return 0.5 * x * (1.0 + jnp.tanh(math.sqrt(2.0 / math.pi) * (x + 0.044715 * (x * x * x))))


def _s5_seq_body(u_ref, kst_ref, bst_ref, mr_ref, d_ref, are_ref, aim_ref, pre_ref, pim_ref,
                 y_ref, hre_ref, him_ref, xre_ref, xim_ref, ys_ref, *, lc):
    c = pl.program_id(2)
    tile = SUBLANES
    nt = lc // tile
    ns = S5_SLAB_STATE

    @pl.when(c == 0)
    def _():
        xre_ref[0:tile, :] = jnp.zeros((tile, ns), F32)
        xim_ref[0:tile, :] = jnp.zeros((tile, ns), F32)

    u = u_ref[...]
    row = lax.broadcasted_iota(jnp.int32, (lc, LANES), 0) % tile
    parts = [u.astype(BF16)]
    for j in range(1, tile):
        parts.append(jnp.where(row >= j, pltpu.roll(u, j, 0), 0.0).astype(BF16))
    y_local = _dot(jnp.concatenate(parts, axis=1), kst_ref[0])

    tstack = jnp.concatenate(
        [u_ref[pl.ds(tile - 1 - j, nt, stride=tile), :].astype(BF16) for j in range(tile)], axis=1)
    v = _dot(tstack, bst_ref[0])
    re, im = v[:, :ns], v[:, ns:]
    for si in range(3):
        sh = 1 << si
        sre = pltpu.roll(re, sh, 0).reshape(nt // tile, tile, ns)
        sim = pltpu.roll(im, sh, 0).reshape(nt // tile, tile, ns)
        ar = are_ref[0, si][None]
        ai = aim_ref[0, si][None]
        re3 = re.reshape(nt // tile, tile, ns) + ar * sre - ai * sim
        im3 = im.reshape(nt // tile, tile, ns) + ar * sim + ai * sre
        re = re3.reshape(nt, ns)
        im = im3.reshape(nt, ns)
    cr = jnp.broadcast_to(xre_ref[tile - 1:tile, :], (tile, ns))
    ci = jnp.broadcast_to(xim_ref[tile - 1:tile, :], (tile, ns))
    pr = pre_ref[0]
    pi = pim_ref[0]
    for g in range(nt // tile):
        xr = re[g * tile:(g + 1) * tile] + pr * cr - pi * ci
        xi = im[g * tile:(g + 1) * tile] + pr * ci + pi * cr
        xre_ref[(g + 1) * tile:(g + 2) * tile, :] = xr
        xim_ref[(g + 1) * tile:(g + 2) * tile, :] = xi
        cr = jnp.broadcast_to(xr[tile - 1:tile, :], (tile, ns))
        ci = jnp.broadcast_to(xi[tile - 1:tile, :], (tile, ns))
    hre_ref[...] = cr[0:1, :]
    him_ref[...] = ci[0:1, :]

    xp = jnp.concatenate([xre_ref[tile - 1:tile - 1 + nt, :], xim_ref[tile - 1:tile - 1 + nt, :]],
                         axis=1).astype(BF16)
    for r in range(tile):
        ys_ref[pl.ds(r, nt, stride=tile), :] = _dot(xp, mr_ref[0, r])
    xre_ref[0:tile, :] = xre_ref[nt:nt + tile, :]
    xim_ref[0:tile, :] = xim_ref[nt:nt + tile, :]
    y_ref[...] = _gelu(y_local + ys_ref[...] + d_ref[...] * u)


def _s5_seq(proj, lw, *, batch, seq, lc=1024):
    lc = min(lc, seq)
    nc = seq // lc
    ns = S5_SLAB_STATE
    nst = S5_GROUPS * S5_STATE
    kdim = SUBLANES * LANES
    slab = lambda *shape: pl.BlockSpec((1,) + shape, lambda b, j, c: (j,) + (0,) * len(shape))
    return pl.pallas_call(
        functools.partial(_s5_seq_body, lc=lc),
        grid=(batch, S5_SLABS, nc),
        in_specs=[
            pl.BlockSpec((lc, LANES), lambda b, j, c: (b * nc + c, j)),
            slab(kdim, LANES), slab(kdim, 2 * ns), slab(SUBLANES, 2 * ns, LANES),
            pl.BlockSpec((1, LANES), lambda b, j, c: (0, j)),
            slab(3, SUBLANES, ns), slab(3, SUBLANES, ns), slab(SUBLANES, ns), slab(SUBLANES, ns),
        ],
        out_specs=[
            pl.BlockSpec((lc, LANES), lambda b, j, c: (b * nc + c, j)),
            pl.BlockSpec((None, 1, ns), lambda b, j, c: (b, 0, j)),
            pl.BlockSpec((None, 1, ns), lambda b, j, c: (b, 0, j)),
        ],
        out_shape=[
            jax.ShapeDtypeStruct((batch * seq, S5_WIDTH), F32),
            jax.ShapeDtypeStruct((batch, 1, nst), F32),
            jax.ShapeDtypeStruct((batch, 1, nst), F32),
        ],
        scratch_shapes=[pltpu.VMEM((lc // SUBLANES + SUBLANES, ns), F32),
                        pltpu.VMEM((lc // SUBLANES + SUBLANES, ns), F32), pltpu.VMEM((lc, LANES), F32)],
        compiler_params=_cparams("parallel", "arbitrary", "arbitrary"),
        name="s5_seq",
    )(proj, lw["k_stack"], lw["b_stack"], lw["m_rows"], lw["d"], lw["a8_re_hs"], lw["a8_im_hs"],
      lw["p8_re"], lw["p8_im"])


def _s5_step_body(u_ref, h0re_ref, h0im_ref, bre_ref, bim_ref, brel_ref, biml_ref, cre_ref, cim_ref, d_ref,
                  are_ref, aim_ref, y_ref, hre_ref, him_ref, *, nt):
    xr = h0re_ref[...]
    xi = h0im_ref[...]
    ar = are_ref[0]
    ai = aim_ref[0]
    for t in range(nt):
        u = u_ref[t]
        uh = u.astype(BF16)
        ul = (u - uh.astype(F32)).astype(BF16)
        bur = _dot(uh, bre_ref[0]) + _dot(ul, bre_ref[0]) + _dot(uh, brel_ref[0])
        bui = _dot(uh, bim_ref[0]) + _dot(ul, bim_ref[0]) + _dot(uh, biml_ref[0])
        xr, xi = ar * xr - ai * xi + bur, ar * xi + ai * xr + bui
        y = _dot(xr.astype(BF16), cre_ref[0]) + _dot(xi.astype(BF16), cim_ref[0])
        y_ref[t] = _gelu(y + d_ref[...] * u)
    hre_ref[...] = xr
    him_ref[...] = xi


def _s5_step(u_t, h0re, h0im, lw):
    nt, nb, _ = u_t.shape
    ns = S5_SLAB_STATE
    slab3 = lambda a, b: pl.BlockSpec((1, a, b), lambda j: (j, 0, 0))
    return pl.pallas_call(
        functools.partial(_s5_step_body, nt=nt),
        grid=(S5_SLABS,),
        in_specs=[
            pl.BlockSpec((nt, nb, LANES), lambda j: (0, 0, j)),
            pl.BlockSpec((nb, ns), lambda j: (0, j)), pl.BlockSpec((nb, ns), lambda j: (0, j)),
            slab3(LANES, ns), slab3(LANES, ns), slab3(LANES, ns), slab3(LANES, ns),
            slab3(ns, LANES), slab3(ns, LANES),
            pl.BlockSpec((1, LANES), lambda j: (0, j)),
            slab3(1, ns), slab3(1, ns),
        ],
        out_specs=[
            pl.BlockSpec((nt, nb, LANES), lambda j: (0, 0, j)),
            pl.BlockSpec((nb, ns), lambda j: (0, j)), pl.BlockSpec((nb, ns), lambda j: (0, j)),
        ],
        out_shape=[
            jax.ShapeDtypeStruct((nt, nb, S5_WIDTH), F32),
            jax.ShapeDtypeStruct((nb, S5_GROUPS * S5_STATE), F32),
            jax.ShapeDtypeStruct((nb, S5_GROUPS * S5_STATE), F32),
        ],
        compiler_params=_cparams("parallel"),
        name="s5_step",
    )(u_t, h0re, h0im, lw["b_re"], lw["b_im"], lw["b_re_lo"], lw["b_im_lo"], lw["c_re"], lw["c_im"], lw["d"],
      lw["a_re1"], lw["a_im1"])


def _odd_tail_body(y_ref, z_ref, h_ref, wg_ref, bg_ref, wo_ref, out_ref):
    y = y_ref[...]
    gl = _dot(y.astype(BF16), wg_ref[...]) + bg_ref[...]
    y = y * (1.0 / (1.0 + jnp.exp(-gl)))
    y = y * _silu(z_ref[...])
    out_ref[...] = h_ref[...] + _dot(y.astype(BF16), wo_ref[...])


def _odd_tail(y, proj, h, lw, tm=512):
    m = h.shape[0]
    tm = min(tm, m)
    row = lambda c: pl.BlockSpec((tm, D_MODEL), lambda i: (i, c))
    full = lambda shape: pl.BlockSpec(shape, lambda i: (0,) * len(shape))
    return pl.pallas_call(
        _odd_tail_body,
        grid=(m // tm,),
        in_specs=[row(0), row(1), row(0), full((S5_WIDTH, S5_WIDTH)), full((1, S5_WIDTH)),
                  full((S5_WIDTH, D_MODEL))],
        out_specs=row(0),
        out_shape=jax.ShapeDtypeStruct((m, D_MODEL), F32),
        compiler_params=_cparams("parallel"),
        name="odd_tail",
    )(y, proj, h, lw["w_glu"], lw["b_glu"], lw["w_out"])


def _rot_half_cols(w):
    half = QK_ROPE // 2
    return jnp.concatenate([-w[..., half:], w[..., :half]], axis=-1)


def _head_pad(x_nope, x_rope):
    z = jnp.zeros(x_nope.shape[:-1] + (LANES - QK_DIM,), x_nope.dtype)
    out = jnp.concatenate([x_nope, x_rope, z], axis=-1)
    return out.reshape(out.shape[:-2] + (MLA_HEADS * LANES,))


def _even_weights(w_in, conv_w, conv_b, dt_bias, a_log, d_ssm, ssm_norm_w, q_a_norm_w, w_qb, kv_a_norm_w, w_kvb,
                  q_norm_w, k_norm_w, w_out):
    k = w_in.shape[0]
    o = 0
    parts = {}
    for name, sz in (("z", SSM_D_INNER), ("xbc", SSM_CONV_CH), ("dt", SSM_HEADS), ("ql", Q_LORA), ("kvl", KV_LORA),
                     ("kpe", QK_ROPE), ("g", MLA_WIDTH)):
        parts[name] = w_in[:, o:o + sz]
        o += sz
    zc = lambda n: jnp.zeros((k, n), F32)
    kpe_blk = jnp.concatenate([zc(QK_NOPE), parts["kpe"], zc(LANES - QK_DIM)], axis=1)
    kpesw_blk = jnp.concatenate([zc(QK_NOPE), _rot_half_cols(parts["kpe"]), zc(LANES - QK_DIM)], axis=1)
    dt_blk = jnp.concatenate([parts["dt"], zc(LANES - SSM_HEADS)], axis=1)
    w_in_p = jnp.concatenate([parts["xbc"], parts["z"], parts["g"], parts["ql"], parts["kvl"], kpe_blk, kpesw_blk,
                              dt_blk, zc(N_EVEN_PAD - C_DT - LANES)], axis=1).astype(BF16)
    wq = w_qb.reshape(Q_LORA, MLA_HEADS, QK_DIM)
    zq = jnp.zeros((Q_LORA, MLA_HEADS, QK_NOPE), F32)
    wkv = w_kvb.reshape(KV_LORA, MLA_HEADS, QK_NOPE + V_DIM)
    w_nope = wkv[..., :QK_NOPE]
    pad1 = lambda v, n: jnp.concatenate([v, jnp.zeros((n - v.shape[0],), F32)]).reshape(1, n)
    norm_pat = lambda w: jnp.concatenate([w, jnp.zeros((LANES - QK_DIM,), F32)]).reshape(1, LANES)
    w_abs = jnp.concatenate([jnp.transpose(w_nope, (1, 2, 0)),
                             jnp.zeros((MLA_HEADS, LANES - QK_NOPE, KV_LORA), F32)], axis=1)
    e_pe = jnp.zeros((LANES, LANES), F32).at[QK_NOPE + jnp.arange(QK_ROPE), jnp.arange(QK_ROPE)].set(1.0)
    head_expand = (jnp.arange(LANES)[:, None] == (jnp.arange(SSM_D_INNER)[None, :] // SSM_HEAD_DIM))
    return {
        "w_in": w_in_p,
        "conv_w": conv_w, "conv_b": conv_b.reshape(1, -1),
        "dt_bias": pad1(dt_bias, LANES), "a_log": pad1(a_log, LANES),
        "d_exp": jnp.repeat(d_ssm, SSM_HEAD_DIM).reshape(1, -1),
        "ssm_norm_w": ssm_norm_w.reshape(1, -1),
        "head_expand": head_expand.astype(BF16),
        "q_a_norm_w": q_a_norm_w.reshape(1, -1),
        "w_q": _head_pad(wq[..., :QK_NOPE], wq[..., QK_NOPE:]).astype(BF16),
        "w_qsw": _head_pad(zq, _rot_half_cols(wq[..., QK_NOPE:])).astype(BF16),
        "q_norm_pat": norm_pat(q_norm_w),
        "kv_a_norm_w": kv_a_norm_w.reshape(1, -1),
        "w_k": _head_pad(w_nope, jnp.zeros((KV_LORA, MLA_HEADS, QK_ROPE), F32)).astype(BF16),
        "w_v": wkv[..., QK_NOPE:].reshape(KV_LORA, MLA_WIDTH).astype(BF16),
        "w_v_t": wkv[..., QK_NOPE:].reshape(KV_LORA, MLA_WIDTH).T.astype(BF16),
        "k_norm_pat": norm_pat(k_norm_w),
        "w_abs": w_abs.astype(BF16),
        "e_pe": e_pe.astype(BF16),
        "w_nope_t": jnp.transpose(w_nope, (1, 2, 0)).reshape(MLA_HEADS * QK_NOPE, KV_LORA).astype(BF16),
        "w_out": w_out.astype(BF16),
    }


def _rope_tables(pos):
    half = QK_ROPE // 2
    inv_freq = jnp.power(ROPE_THETA, -jnp.arange(half, dtype=F32) / half)
    ang = pos[:, None] * inv_freq[None, :]
    c, s = jnp.cos(ang), jnp.sin(ang)
    n = pos.shape[0]
    cos_t = jnp.concatenate([jnp.ones((n, QK_NOPE), F32), c, c, jnp.ones((n, LANES - QK_DIM), F32)], axis=1)
    sin_t = jnp.concatenate([jnp.zeros((n, QK_NOPE), F32), s, s, jnp.zeros((n, LANES - QK_DIM), F32)], axis=1)
    return cos_t, sin_t


def _odd_weights(w_in, a_re, a_im, b_re, b_im, c_re, c_im, d, log_step, w_glu, b_glu, w_out):
    ar, ai = a_re.astype(F32), a_im.astype(F32)
    step = jnp.exp(log_step.astype(F32))[:, None]
    mag = jnp.exp(ar * step)
    ab_re, ab_im = mag * jnp.cos(ai * step), mag * jnp.sin(ai * step)
    den = ar * ar + ai * ai
    nr, ni = ab_re - 1.0, ab_im
    f_re = (nr * ar + ni * ai) / den
    f_im = (ni * ar - nr * ai) / den
    bb_re = f_re[..., None] * b_re - f_im[..., None] * b_im
    bb_im = f_re[..., None] * b_im + f_im[..., None] * b_re
    gl = LANES // S5_GROUP_CH

    eye = jnp.eye(gl, dtype=F32)

    def b_blocks(bb):
        x = bb.reshape(-1, S5_SLABS, gl, S5_STATE, S5_GROUP_CH)
        return jnp.einsum("jsgnc,gh->sjgchn", x, eye).reshape(S5_SLABS, -1, S5_SLAB_STATE)

    def c_blocks(cc):
        x = cc.reshape(-1, S5_SLABS, gl, S5_GROUP_CH, S5_STATE)
        return jnp.einsum("jsgcn,gh->sjgnhc", x, eye).reshape(S5_SLABS, -1, S5_SLAB_STATE, LANES)

    def k_blocks(kk):
        x = kk.reshape(-1, S5_SLABS, gl, S5_GROUP_CH, S5_GROUP_CH)
        return jnp.einsum("jsgcd,gh->sjgdhc", x, eye).reshape(S5_SLABS, -1, LANES)

    def powers(base_re, base_im, n):
        pr, pi = jnp.ones_like(base_re), jnp.zeros_like(base_im)
        out = []
        for _ in range(n):
            pr, pi = pr * base_re - pi * base_im, pr * base_im + pi * base_re
            out.append((pr, pi))
        return out

    pw = powers(ab_re, ab_im, SUBLANES)
    p_re = jnp.stack([jnp.ones_like(ab_re)] + [p[0] for p in pw])
    p_im = jnp.stack([jnp.zeros_like(ab_im)] + [p[1] for p in pw])
    cr, ci = c_re.astype(F32), c_im.astype(F32)
    t_re, t_im = p_re[:SUBLANES, :, :, None], p_im[:SUBLANES, :, :, None]
    bj_re = t_re * bb_re - t_im * bb_im
    bj_im = t_re * bb_im + t_im * bb_re
    b_stack = jnp.concatenate([b_blocks(bj_re), b_blocks(bj_im)], axis=2)
    hi = lax.Precision.HIGHEST
    k_stack = k_blocks(jnp.einsum("gcn,jgnd->jgcd", cr, bj_re, precision=hi)
                       - jnp.einsum("gcn,jgnd->jgcd", ci, bj_im, precision=hi))
    q_re, q_im = p_re[1:, :, None, :], p_im[1:, :, None, :]
    m_rows = jnp.concatenate([c_blocks(cr * q_re - ci * q_im), c_blocks(-(cr * q_im + ci * q_re))], axis=2)
    pw8 = powers(pw[-1][0], pw[-1][1], SUBLANES)
    slab = lambda x: x.reshape(S5_SLABS, S5_SLAB_STATE)
    rows = jnp.arange(SUBLANES)[None, :, None]
    hs8_re = jnp.stack([jnp.where(rows >= sh, slab(pw8[sh - 1][0])[:, None, :], 0.0) for sh in (1, 2, 4)], axis=1)
    hs8_im = jnp.stack([jnp.where(rows >= sh, slab(pw8[sh - 1][1])[:, None, :], 0.0) for sh in (1, 2, 4)], axis=1)
    p8_re = jnp.stack([slab(p[0]) for p in pw8], axis=1)
    p8_im = jnp.stack([slab(p[1]) for p in pw8], axis=1)
    bre_f, bim_f = b_blocks(bb_re), b_blocks(bb_im)
    bre16, bim16 = bre_f.astype(BF16), bim_f.astype(BF16)
    return {
        "w_in": w_in.astype(BF16),
        "b_re": bre16, "b_im": bim16,
        "b_re_lo": (bre_f - bre16.astype(F32)).astype(BF16), "b_im_lo": (bim_f - bim16.astype(F32)).astype(BF16),
        "c_re": c_blocks(cr)[:, 0].astype(BF16), "c_im": c_blocks(-ci)[:, 0].astype(BF16),
        "d": d.reshape(1, -1),
        "k_stack": k_stack.astype(BF16), "b_stack": b_stack.astype(BF16), "m_rows": m_rows.astype(BF16),
        "a8_re_hs": hs8_re, "a8_im_hs": hs8_im, "p8_re": p8_re, "p8_im": p8_im,
        "a_re1": slab(ab_re).reshape(S5_SLABS, 1, S5_SLAB_STATE),
        "a_im1": slab(ab_im).reshape(S5_SLABS, 1, S5_SLAB_STATE),
        "w_glu": w_glu.astype(BF16), "b_glu": b_glu.reshape(1, -1), "w_out": w_out.astype(BF16),
    }


SAMPLE_ROWS = 8


def _even_layer(hp, hs, norm_w, lw, tabs, state_ssm, state_conv, pool_c, pool_pt, layer, page_table, bp, sp, bs, ss):
    cos_p, sin_p, cos_s, sin_s = tabs
    hist = SSM_CONV - 1
    proj_p = _inproj(hp, norm_w, lw["w_in"])
    proj_s = _inproj(hs, norm_w, lw["w_in"])
    nchunks = sp // SSM_CHUNK
    zero_state = jnp.zeros((bp, SSM_D_INNER, SSM_STATE), F32)
    y_p, ssm_p = _ssd(proj_p, proj_p, proj_p, zero_state, lw, nblk=bp * nchunks, nchunks=nchunks, nseq=1,
                      P=SSM_CHUNK, n_hist=0, n_real=SSM_CHUNK, carry=True,
                      xbc_col=0, z_col=C_Z // SSM_D_INNER, dt_col=C_DT // LANES)
    conv_p = proj_p.reshape(bp, sp, N_EVEN_PAD)[:, sp - hist:, C_XBC:C_XBC + SSM_CONV_CH]
    xbc_s = proj_s[:, C_XBC:C_XBC + SSM_CONV_CH].reshape(bs, ss, SSM_CONV_CH)
    xbc_full = jnp.concatenate([state_conv, xbc_s], axis=1)
    pad_rows = SAMPLE_ROWS - hist - ss
    pad3 = lambda a: jnp.pad(a, ((0, 0), (hist, pad_rows), (0, 0)))
    xbc_pad = jnp.pad(xbc_full, ((0, 0), (0, pad_rows), (0, 0))).reshape(bs * SAMPLE_ROWS, SSM_CONV_CH)
    z_pad = pad3(proj_s[:, C_Z:C_Z + SSM_D_INNER].reshape(bs, ss, -1)).reshape(bs * SAMPLE_ROWS, -1)
    dt_pad = pad3(proj_s[:, C_DT:C_DT + LANES].reshape(bs, ss, -1)).reshape(bs * SAMPLE_ROWS, -1)
    nseq = SSM_CHUNK // SAMPLE_ROWS
    y_s_pad, ssm_s = _ssd(xbc_pad, z_pad, dt_pad, state_ssm.reshape(-1, SSM_D_INNER, SSM_STATE), lw,
                          nblk=bs // nseq, nchunks=1, nseq=nseq, P=SAMPLE_ROWS, n_hist=hist, n_real=ss,
                          carry=False, xbc_col=0, z_col=0, dt_col=0, h0_offset=layer * bs)
    y_s = y_s_pad.reshape(bs, SAMPLE_ROWS, SSM_D_INNER)[:, hist:hist + ss].reshape(bs * ss, SSM_D_INNER)
    conv_s = xbc_full[:, ss:]
    q_p, ckv_p, kper_p, k_p, v_p = _mla_proj(proj_p, cos_p, sin_p, lw, with_kv=True, tm=FLASH_TK)
    q_s, ckv_s, kper_s = _mla_proj(proj_s, cos_s, sin_s, lw, with_kv=False)
    o_p = _flash(q_p, k_p, v_p, batch=bp, seq=sp)
    kpe_p = kper_p[:, QK_NOPE:QK_DIM]
    kpe_s = kper_s[:, QK_NOPE:QK_DIM]
    qabs, qpe = _qabs(q_s, lw)
    qabs = qabs.reshape(bs, ss * MLA_HEADS, KV_LORA)
    qpe = qpe.reshape(bs, ss * MLA_HEADS, LANES)[:, :, :QK_ROPE]
    cnew = jnp.pad(ckv_s.reshape(bs, ss, KV_LORA), ((0, 0), (0, PAGE - ss), (0, 0)))
    pnew_t = jnp.pad(jnp.swapaxes(kpe_s.reshape(bs, ss, QK_ROPE), 1, 2), ((0, 0), (0, 0), (0, PAGE - ss)))
    o_s = _decode_attn(page_table, pool_c, pool_pt, layer, cnew, pnew_t, qabs, qpe, lw).reshape(bs * ss, MLA_WIDTH)
    hp = _outproj_even(y_p, o_p, proj_p, hp, lw["w_out"])
    hs = _outproj_even(y_s, o_s, proj_s, hs, lw["w_out"])
    outs = (ckv_p.reshape(bp, sp, KV_LORA), kpe_p.reshape(bp, sp, QK_ROPE), ckv_s.reshape(bs, ss, KV_LORA),
            kpe_s.reshape(bs, ss, QK_ROPE), ssm_p.reshape(bp, SSM_HEADS, SSM_HEAD_DIM, SSM_STATE),
            ssm_s.reshape(bs, SSM_HEADS, SSM_HEAD_DIM, SSM_STATE), conv_p, conv_s)
    return hp, hs, outs


def _odd_layer(hp, hs, norm_w, lw, s5_re, s5_im, bp, sp, bs, ss):
    proj_p = _inproj(hp, norm_w, lw["w_in"])
    proj_s = _inproj(hs, norm_w, lw["w_in"])
    y_p, r_p, i_p = _s5_seq(proj_p, lw, batch=bp, seq=sp)
    u_t = jnp.transpose(proj_s[:, :S5_WIDTH].reshape(bs, ss, S5_WIDTH), (1, 0, 2))
    y_t, r_s, i_s = _s5_step(u_t, s5_re.reshape(bs, -1), s5_im.reshape(bs, -1), lw)
    y_s = jnp.transpose(y_t, (1, 0, 2)).reshape(bs * ss, S5_WIDTH)
    hp = _odd_tail(y_p, proj_p, hp, lw)
    hs = _odd_tail(y_s, proj_s, hs, lw)
    shp = lambda a, b: a.reshape(b, S5_GROUPS, S5_STATE)
    return hp, hs, (shp(r_p, bp), shp(i_p, bp), shp(r_s, bs), shp(i_s, bs))


def kernel(x_prompt, x_sample, cache_ckv, cache_kpe, page_table, state_ssm, state_conv, state_s5_re, state_s5_im, norm_w, w_in_even, conv_w, conv_b, dt_bias, a_log, d_ssm, ssm_norm_w, q_a_norm_w, w_qb, kv_a_norm_w, w_kvb, q_norm_w, k_norm_w, w_out_even, w_in_odd, s5_a_re, s5_a_im, s5_b_re, s5_b_im, s5_c_re, s5_c_im, s5_d, s5_log_step, w_glu, b_glu, w_out_odd):
    bp, sp, _ = x_prompt.shape
    bs, ss, _ = x_sample.shape
    past_len = page_table.shape[1] * PAGE
    depth = norm_w.shape[0]
    pos_p = jnp.tile(jnp.arange(sp, dtype=F32), bp)
    pos_s = jnp.tile(past_len + jnp.arange(ss, dtype=F32), bs)
    tabs = _rope_tables(pos_p) + _rope_tables(pos_s)
    cache_kpe_t = jnp.swapaxes(cache_kpe, 2, 3)
    hp = x_prompt.reshape(bp * sp, D_MODEL)
    hs = x_sample.reshape(bs * ss, D_MODEL)
    even_out, odd_out = [], []
    for i in range(depth):
        j = i // 2
        if i % 2 == 0:
            lw = _even_weights(w_in_even[j], conv_w[j], conv_b[j], dt_bias[j], a_log[j], d_ssm[j], ssm_norm_w[j],
                               q_a_norm_w[j], w_qb[j], kv_a_norm_w[j], w_kvb[j], q_norm_w[j], k_norm_w[j],
                               w_out_even[j])
            hp, hs, o = _even_layer(hp, hs, norm_w[i], lw, tabs, state_ssm, state_conv[j], cache_ckv,
                                    cache_kpe_t, j, page_table, bp, sp, bs, ss)
            even_out.append(o)
        else:
            lw = _odd_weights(w_in_odd[j], s5_a_re[j], s5_a_im[j], s5_b_re[j], s5_b_im[j], s5_c_re[j], s5_c_im[j],
                              s5_d[j], s5_log_step[j], w_glu[j], b_glu[j], w_out_odd[j])
            hp, hs, o = _odd_layer(hp, hs, norm_w[i], lw, state_s5_re[j], state_s5_im[j], bp, sp, bs, ss)
            odd_out.append(o)
    ev = [jnp.stack([o[k] for o in even_out]) for k in range(8)]
    od = [jnp.stack([o[k] for o in odd_out]) for k in range(4)]
    return (hp.reshape(bp, sp, D_MODEL), hs.reshape(bs, ss, D_MODEL),
            ev[0], ev[1], ev[2], ev[3], ev[4], ev[5], ev[6], ev[7], od[0], od[1], od[2], od[3])
```

```python
import functools
import math

import jax
import jax.numpy as jnp
from jax import lax
from jax.experimental import pallas as pl
from jax.experimental.pallas import tpu as pltpu

F32 = jnp.float32
BF16 = jnp.bfloat16
EPS = 1e-6

D_MODEL = 1024
LANES = 128
SUBLANES = 8
VMEM_LIMIT = 48 * 1024 * 1024

SSM_D_INNER = 1024
SSM_HEAD_DIM = 64
SSM_HEADS = 16
SSM_GROUPS = 4
SSM_STATE = 128
SSM_CONV = 4
SSM_CHUNK = 128
SSM_GN = SSM_GROUPS * SSM_STATE
SSM_CONV_CH = SSM_D_INNER + 2 * SSM_GN

MLA_HEADS = 16
Q_LORA = 256
KV_LORA = 256
QK_NOPE = 64
QK_ROPE = 32
QK_DIM = QK_NOPE + QK_ROPE
V_DIM = 64
MLA_WIDTH = MLA_HEADS * V_DIM
ROPE_THETA = 10000.0
PAGE = 128
FLASH_TK = 512
FLASH_TN = 256
FLASH_DEPTH = 6
FLASH_LROWS = 16

S5_WIDTH = 1024
S5_GROUP_CH = 16
S5_GROUPS = 64
S5_STATE = 64
S5_SLABS = S5_WIDTH // LANES
S5_SLAB_STATE = (LANES // S5_GROUP_CH) * S5_STATE

C_XBC, C_Z, C_G, C_QL, C_KVL, C_KPE, C_KPESW, C_DT, N_EVEN_PAD = 0, 2048, 3072, 4096, 4352, 4608, 4736, 4864, 5120


def _cparams(*sem):
    return pltpu.CompilerParams(dimension_semantics=sem, vmem_limit_bytes=VMEM_LIMIT)


def _silu(x):
    return x * (1.0 / (1.0 + jnp.exp(-x)))


def _dot(a, b):
    return jnp.dot(a, b, preferred_element_type=F32)


def _dot_nt(a, b):
    return lax.dot_general(a, b, (((1,), (1,)), ((), ())), preferred_element_type=F32)


def _split3(v):
    hi = v.astype(BF16)
    r1 = v - hi.astype(F32)
    mid = r1.astype(BF16)
    lo = (r1 - mid.astype(F32)).astype(BF16)
    return hi, mid, lo


def _dot_exact_lhs(m_bf16, v):
    hi, mid, lo = _split3(v)
    return _dot(m_bf16, hi) + _dot(m_bf16, mid) + _dot(m_bf16, lo)


def _dot_exact_rhs(v, m_bf16):
    hi, mid, lo = _split3(v)
    return _dot(hi, m_bf16) + _dot(mid, m_bf16) + _dot(lo, m_bf16)


def _inproj_body(x_ref, nw_ref, w_ref, o_ref, xn_ref):
    @pl.when(pl.program_id(1) == 0)
    def _():
        x = x_ref[...]
        ms = jnp.mean(x * x, axis=-1, keepdims=True)
        xn_ref[...] = (x * lax.rsqrt(ms + EPS) * nw_ref[...]).astype(BF16)

    o_ref[...] = _dot(xn_ref[...], w_ref[...])


def _inproj(x, norm_w, w, tm=512, tn=1024):
    m, k = x.shape
    n = w.shape[1]
    tm = min(tm, m)
    return pl.pallas_call(
        _inproj_body,
        grid=(m // tm, n // tn),
        in_specs=[
            pl.BlockSpec((tm, k), lambda i, j: (i, 0)),
            pl.BlockSpec((1, k), lambda i, j: (0, 0)),
            pl.BlockSpec((k, tn), lambda i, j: (0, j)),
        ],
        out_specs=pl.BlockSpec((tm, tn), lambda i, j: (i, j)),
        out_shape=jax.ShapeDtypeStruct((m, n), F32),
        scratch_shapes=[pltpu.VMEM((tm, k), BF16)],
        compiler_params=_cparams("parallel", "arbitrary"),
        name="inproj",
    )(x, norm_w.reshape(1, k), w)


def _ssd_body(xbc_ref, z_ref, dt_ref, h0_ref, convw_ref, convb_ref, dtb_ref, alog_ref, dexp_ref, nw_ref, e_ref,
              y_ref, hout_ref,
              xp_ref, ysc_ref, xwt_ref, b_ref, c_ref, cse_ref, tott_ref,
              *, L, P, n_hist, n_real, nseq, nchunks, carry):
    blk = pl.program_id(0)
    s = pl.program_id(1)
    chunk = blk % nchunks
    HIST = SUBLANES

    @pl.when(s == 0)
    def _intra():
        if carry:
            @pl.when(chunk == 0)
            def _():
                xp_ref[0:HIST, :] = jnp.zeros((HIST, SSM_CONV_CH), F32)
        else:
            xp_ref[0:HIST, :] = jnp.zeros((HIST, SSM_CONV_CH), F32)
        xp_ref[HIST:HIST + L, :] = xbc_ref[...]
        conv = convb_ref[...] + convw_ref[3:4, :] * xp_ref[HIST:HIST + L, :]
        for k in range(SSM_CONV - 1):
            off = HIST - (SSM_CONV - 1) + k
            conv = conv + convw_ref[k:k + 1, :] * xp_ref[off:off + L, :]
        if carry:
            xp_ref[0:HIST, :] = xp_ref[L:L + HIST, :]
        xc = _silu(conv)
        xs = xc[:, :SSM_D_INNER]
        b_ref[...] = xc[:, SSM_D_INNER:SSM_D_INNER + SSM_GN].astype(BF16)
        c_ref[...] = xc[:, SSM_D_INNER + SSM_GN:].astype(BF16)

        raw = dt_ref[...] + dtb_ref[...]
        dt = jnp.maximum(raw, 0.0) + jnp.log1p(jnp.exp(-jnp.abs(raw)))
        ri = lax.broadcasted_iota(jnp.int32, (L, L), 0)
        ci = lax.broadcasted_iota(jnp.int32, (L, L), 1)
        if P < L:
            rp = lax.broadcasted_iota(jnp.int32, (L, LANES), 0) % P
            dt = jnp.where((rp >= n_hist) & (rp < n_hist + n_real), dt, 0.0)
            same = (ri // P) == (ci // P)
            causal = same & (ci <= ri)
        else:
            same = ri >= 0
            causal = ci <= ri
        a_neg = -jnp.exp(alog_ref[...])
        da = dt * a_neg
        m_cum = jnp.where(causal, 1.0, 0.0).astype(BF16)
        m_tot = jnp.where(same, 1.0, 0.0).astype(BF16)
        cs = _dot_exact_lhs(m_cum, da)
        tot = _dot_exact_lhs(m_tot, da)
        cst = cs.T
        e = e_ref[...]
        dt_e = _dot_exact_rhs(dt, e)
        cs_e = _dot_exact_rhs(cs, e)
        tot_e = _dot_exact_rhs(tot, e)
        cse_ref[...] = cs_e
        tott_ref[...] = tot_e.T
        xdt = xs * dt_e
        xw = xdt * jnp.exp(tot_e - cs_e)
        xwt_ref[...] = xw.T.astype(BF16)
        xdt16 = xdt.astype(BF16)
        lane = lax.broadcasted_iota(jnp.int32, (L, LANES), 1)
        for g in range(SSM_GROUPS):
            cb = _dot_nt(c_ref[:, g * SSM_STATE:(g + 1) * SSM_STATE], b_ref[:, g * SSM_STATE:(g + 1) * SSM_STATE])
            for pr in range(2):
                h0i = 4 * g + 2 * pr
                col = h0i // 2
                xpair = xdt16[:, col * LANES:(col + 1) * LANES]
                ys = []
                for hh in (h0i, h0i + 1):
                    dec = jnp.exp(jnp.where(causal, cs[:, hh:hh + 1] - cst[hh:hh + 1, :], -1e30))
                    ys.append(_dot((cb * dec).astype(BF16), xpair))
                ypair = jnp.where(lane < SSM_HEAD_DIM, ys[0], ys[1])
                ysc_ref[:, col * LANES:(col + 1) * LANES] = (
                    ypair + dexp_ref[:, col * LANES:(col + 1) * LANES] * xs[:, col * LANES:(col + 1) * LANES])

    if carry:
        @pl.when(chunk == 0)
        def _():
            hout_ref[...] = h0_ref[...]
    else:
        hout_ref[...] = h0_ref[...]

    if nseq > 1:
        rmask = (lax.broadcasted_iota(jnp.int32, (L, LANES), 0) // P) == s
        cmask = (lax.broadcasted_iota(jnp.int32, (LANES, L), 1) // P) == s
        onehot = lax.broadcasted_iota(jnp.int32, (LANES, L), 1) == s * P
    for col in range(SSM_HEADS // 2):
        g = col // 2
        sp = hout_ref[col * LANES:(col + 1) * LANES, :]
        yoff = _dot_nt(c_ref[:, g * SSM_STATE:(g + 1) * SSM_STATE], sp.astype(BF16))
        yoff = yoff * jnp.exp(cse_ref[:, col * LANES:(col + 1) * LANES])
        xwt = xwt_ref[col * LANES:(col + 1) * LANES, :]
        tott = tott_ref[col * LANES:(col + 1) * LANES, :]
        if nseq > 1:
            yoff = jnp.where(rmask, yoff, 0.0)
            xwt = jnp.where(cmask, xwt, jnp.zeros_like(xwt))
            deccol = jnp.exp(jnp.sum(jnp.where(onehot, tott, 0.0), axis=1, keepdims=True))
        else:
            deccol = jnp.exp(tott[:, 0:1])
        ysc_ref[:, col * LANES:(col + 1) * LANES] += yoff
        hout_ref[col * LANES:(col + 1) * LANES, :] = sp * deccol + _dot(xwt, b_ref[:, g * SSM_STATE:(g + 1) * SSM_STATE])

    @pl.when(s == nseq - 1)
    def _epilogue():
        gw = SSM_D_INNER // SSM_GROUPS
        for g in range(SSM_GROUPS):
            y = ysc_ref[:, g * gw:(g + 1) * gw] * _silu(z_ref[:, g * gw:(g + 1) * gw])
            ms = jnp.mean(y * y, axis=-1, keepdims=True)
            y_ref[:, g * gw:(g + 1) * gw] = y * lax.rsqrt(ms + EPS) * nw_ref[:, g * gw:(g + 1) * gw]


def _ssd(xbc_src, z_src, dt_src, h0, lw, *, nblk, nchunks, nseq, P, n_hist, n_real, carry,
         xbc_col, z_col, dt_col, h0_offset=0):
    L = SSM_CHUNK
    if carry:
        nstate = nblk // nchunks
        state_idx = lambda b, s: (b // nchunks, 0, 0)
    else:
        nstate = nblk * nseq
        state_idx = lambda b, s: (b * nseq + s, 0, 0)
    h0_idx = lambda b, s: (h0_offset + state_idx(b, s)[0], 0, 0)
    full = lambda shape: pl.BlockSpec(shape, lambda b, s: (0,) * len(shape))
    body = functools.partial(_ssd_body, L=L, P=P, n_hist=n_hist, n_real=n_real, nseq=nseq, nchunks=nchunks,
                             carry=carry)
    return pl.pallas_call(
        body,
        grid=(nblk, nseq),
        in_specs=[
            pl.BlockSpec((L, SSM_CONV_CH), lambda b, s: (b, xbc_col)),
            pl.BlockSpec((L, SSM_D_INNER), lambda b, s: (b, z_col)),
            pl.BlockSpec((L, LANES), lambda b, s: (b, dt_col)),
            pl.BlockSpec((None, SSM_D_INNER, SSM_STATE), h0_idx),
            full((SSM_CONV, SSM_CONV_CH)),
            full((1, SSM_CONV_CH)),
            full((1, LANES)),
            full((1, LANES)),
            full((1, SSM_D_INNER)),
            full((1, SSM_D_INNER)),
            full((LANES, SSM_D_INNER)),
        ],
        out_specs=[
            pl.BlockSpec((L, SSM_D_INNER), lambda b, s: (b, 0)),
            pl.BlockSpec((None, SSM_D_INNER, SSM_STATE), state_idx),
        ],
        out_shape=[
            jax.ShapeDtypeStruct((nblk * L, SSM_D_INNER), F32),
            jax.ShapeDtypeStruct((nstate, SSM_D_INNER, SSM_STATE), F32),
        ],
        scratch_shapes=[
            pltpu.VMEM((L + SUBLANES, SSM_CONV_CH), F32),
            pltpu.VMEM((L, SSM_D_INNER), F32),
            pltpu.VMEM((SSM_D_INNER, L), BF16),
            pltpu.VMEM((L, SSM_GN), BF16),
            pltpu.VMEM((L, SSM_GN), BF16),
            pltpu.VMEM((L, SSM_D_INNER), F32),
            pltpu.VMEM((SSM_D_INNER, L), F32),
        ],
        compiler_params=_cparams("arbitrary", "arbitrary"),
        name="ssd",
    )(xbc_src, z_src, dt_src, h0, lw["conv_w"], lw["conv_b"], lw["dt_bias"], lw["a_log"], lw["d_exp"],
      lw["ssm_norm_w"], lw["head_expand"])


def _mla_proj_body(ql_ref, kvl_ref, kpe_ref, kpesw_ref, cos_ref, sin_ref, qanw_ref, wq_ref, wqsw_ref, qnw_ref,
                   kvanw_ref, wk_ref, wv_ref, knw_ref,
                   q_ref, ckv_ref, kper_ref, *kv_refs, with_kv):
    cos = cos_ref[...]
    sin = sin_ref[...]
    ql = ql_ref[...]
    qn = (ql * lax.rsqrt(jnp.mean(ql * ql, axis=-1, keepdims=True) + EPS) * qanw_ref[...]).astype(BF16)
    q0 = _dot(qn, wq_ref[...])
    q1 = _dot(qn, wqsw_ref[...])
    scale = QK_DIM ** -0.5 * math.log2(math.e)
    for h in range(MLA_HEADS):
        sl = slice(h * LANES, (h + 1) * LANES)
        qh = q0[:, sl] * cos + q1[:, sl] * sin
        ms = jnp.sum(qh * qh, axis=-1, keepdims=True) * (1.0 / QK_DIM)
        q_ref[:, sl] = (qh * lax.rsqrt(ms + EPS) * (qnw_ref[...] * scale)).astype(BF16)
    kvl = kvl_ref[...]
    ckv = kvl * lax.rsqrt(jnp.mean(kvl * kvl, axis=-1, keepdims=True) + EPS) * kvanw_ref[...]
    ckv_ref[...] = ckv
    kper = kpe_ref[...] * cos + kpesw_ref[...] * sin
    kper_ref[...] = kper
    if with_kv:
        k_ref, v_ref = kv_refs
        c16 = ckv.astype(BF16)
        kn = _dot(c16, wk_ref[...])
        for h in range(MLA_HEADS):
            sl = slice(h * LANES, (h + 1) * LANES)
            kh = kn[:, sl] + kper
            ms = jnp.sum(kh * kh, axis=-1, keepdims=True) * (1.0 / QK_DIM)
            k_ref[:, sl] = (kh * lax.rsqrt(ms + EPS) * knw_ref[...]).astype(BF16)
        v_ref[...] = _dot_nt(wv_ref[...], c16).astype(BF16)


def _mla_proj(proj, cos_t, sin_t, lw, *, with_kv, tm=256):
    m = proj.shape[0]
    tm = min(tm, m)
    hp = MLA_HEADS * LANES
    full = lambda shape: pl.BlockSpec(shape, lambda i: (0,) * len(shape))
    row = lambda w, c: pl.BlockSpec((tm, w), lambda i: (i, c))
    out_specs = [row(hp, 0), row(KV_LORA, 0), row(LANES, 0)]
    out_shape = [jax.ShapeDtypeStruct((m, hp), BF16), jax.ShapeDtypeStruct((m, KV_LORA), F32),
                 jax.ShapeDtypeStruct((m, LANES), F32)]
    if with_kv:
        out_specs += [row(hp, 0), pl.BlockSpec((None, MLA_WIDTH, tm), lambda i: (i, 0, 0))]
        out_shape += [jax.ShapeDtypeStruct((m, hp), BF16), jax.ShapeDtypeStruct((m // tm, MLA_WIDTH, tm), BF16)]
    return pl.pallas_call(
        functools.partial(_mla_proj_body, with_kv=with_kv),
        grid=(m // tm,),
        in_specs=[
            row(Q_LORA, C_QL // Q_LORA), row(KV_LORA, C_KVL // KV_LORA), row(LANES, C_KPE // LANES),
            row(LANES, C_KPESW // LANES), row(LANES, 0), row(LANES, 0),
            full((1, Q_LORA)), full((Q_LORA, hp)), full((Q_LORA, hp)), full((1, LANES)),
            full((1, KV_LORA)), full((KV_LORA, hp)), full((MLA_WIDTH, KV_LORA)), full((1, LANES)),
        ],
        out_specs=out_specs,
        out_shape=out_shape,
        compiler_params=_cparams("parallel"),
        name="mla_proj",
    )(proj, proj, proj, proj, cos_t, sin_t, lw["q_a_norm_w"], lw["w_q"], lw["w_qsw"], lw["q_norm_pat"],
      lw["kv_a_norm_w"], lw["w_k"], lw["w_v_t"], lw["k_norm_pat"])


def _flash_body(q_ref, k_ref, vt_ref, o_ref, m_ref, acc_ref, *, tq, tk):
    qi = pl.program_id(2)
    nfull = qi * (tq // tk)
    m_ref[...] = jnp.full(m_ref.shape, -jnp.inf, F32)
    acc_ref[...] = jnp.zeros(acc_ref.shape, F32)
    ones = jnp.ones((FLASH_LROWS, tk), BF16)

    tn = min(FLASH_TN, tq)

    def steps(kis, masked):
        chains = [(ki, nt, j) for ki in kis for nt in range(tq // tn) for j in range(2)]

        def qk(ch):
            ki, nt, j = ch
            koff = pl.multiple_of(ki * tk, tk)
            q = q_ref[nt * tn:(nt + 1) * tn, j * LANES:(j + 1) * LANES]
            k = k_ref[pl.ds(koff, tk), j * LANES:(j + 1) * LANES]
            return _dot_nt(k, q)

        def softmax(ch, st):
            ki, nt, j = ch
            cols = slice(nt * tn, (nt + 1) * tn)
            if masked:
                kpos = ki * tk + lax.broadcasted_iota(jnp.int32, (tk, tn), 0)
                qpos = qi * tq + nt * tn + lax.broadcasted_iota(jnp.int32, (tk, tn), 1)
                st = jnp.where(kpos <= qpos, st, -jnp.inf)
            m_old = m_ref[j, :, cols]
            m_new = jnp.maximum(m_old, jnp.max(st, axis=0, keepdims=True))
            p = jnp.exp2(st - m_new)
            alpha = jnp.exp2(m_old - m_new)
            m_ref[j, :, cols] = m_new
            lhs = jnp.concatenate([vt_ref[ki, j * V_DIM:(j + 1) * V_DIM, :], ones], axis=0)
            return alpha, _dot(lhs, p.astype(BF16))

        def fold(ch, alpha, pv):
            _, nt, j = ch
            cols = slice(nt * tn, (nt + 1) * tn)
            acc_ref[j, :, cols] = alpha * acc_ref[j, :, cols] + pv

        sts = [qk(c) for c in chains[:FLASH_DEPTH]]
        pending = None
        for i, ch in enumerate(chains):
            st = sts.pop(0)
            if i + FLASH_DEPTH < len(chains):
                sts.append(qk(chains[i + FLASH_DEPTH]))
            alpha, pv = softmax(ch, st)
            if pending is not None:
                fold(*pending)
            pending = (ch, alpha, pv)
        fold(*pending)

    unroll = tq // tk

    def full_steps(it, c):
        steps([it * unroll + u for u in range(unroll)], False)
        return c

    lax.fori_loop(0, qi, full_steps, 0)
    steps([nfull + d for d in range(tq // tk)], True)
    ot = jnp.concatenate([acc_ref[j, 0:V_DIM, :] / acc_ref[j, V_DIM:V_DIM + 1, :] for j in range(2)], axis=0)
    o_ref[...] = ot.T


def _flash(q, k, vt, *, batch, seq, tq=1024):
    tk = vt.shape[2]
    tq = min(tq, seq)
    nq = seq // tq
    nk = seq // tk
    return pl.pallas_call(
        functools.partial(_flash_body, tq=tq, tk=tk),
        grid=(batch, MLA_HEADS // 2, nq),
        in_specs=[
            pl.BlockSpec((tq, 2 * LANES), lambda b, h, i: (b * nq + i, h)),
            pl.BlockSpec((seq, 2 * LANES), lambda b, h, i: (b, h)),
            pl.BlockSpec((nk, 2 * V_DIM, tk), lambda b, h, i: (b, h, 0)),
        ],
        out_specs=pl.BlockSpec((tq, 2 * V_DIM), lambda b, h, i: (b * nq + i, h)),
        out_shape=jax.ShapeDtypeStruct((batch * seq, MLA_WIDTH), F32),
        scratch_shapes=[pltpu.VMEM((2, 1, tq), F32), pltpu.VMEM((2, V_DIM + FLASH_LROWS, tq), F32)],
        compiler_params=_cparams("parallel", "parallel", "arbitrary"),
        name="flash",
    )(q, k, vt)


def _qabs_body(q_ref, knw_ref, wabs_ref, epe_ref, qabs_ref, qpe_ref):
    qh = (q_ref[...].astype(F32) * knw_ref[...]).astype(BF16)
    qabs_ref[...] = _dot(qh, wabs_ref[...]).astype(BF16)
    qpe_ref[...] = _dot(qh, epe_ref[...]).astype(BF16)


def _qabs(q, lw):
    m = q.shape[0]
    return pl.pallas_call(
        _qabs_body,
        grid=(MLA_HEADS,),
        in_specs=[
            pl.BlockSpec((m, LANES), lambda h: (0, h)),
            pl.BlockSpec((1, LANES), lambda h: (0, 0)),
            pl.BlockSpec((None, LANES, KV_LORA), lambda h: (h, 0, 0)),
            pl.BlockSpec((LANES, LANES), lambda h: (0, 0)),
        ],
        out_specs=[pl.BlockSpec((m, KV_LORA), lambda h: (0, h)), pl.BlockSpec((m, LANES), lambda h: (0, h))],
        out_shape=[jax.ShapeDtypeStruct((m, MLA_HEADS * KV_LORA), BF16),
                   jax.ShapeDtypeStruct((m, MLA_HEADS * LANES), BF16)],
        compiler_params=_cparams("parallel"),
        name="qabs",
    )(q, lw["k_norm_pat"], lw["w_abs"], lw["e_pe"])


def _decode_body(pt_ref, *refs, pg, ngroups, nq):
    del pt_ref
    c_refs = refs[:pg]
    p_refs = refs[pg:2 * pg]
    (cnew_ref, pnew_ref, wt_ref, qabs_ref, qpe_ref, wv_ref, o_ref,
     m_ref, l_ref, acc_ref, lhs_ref) = refs[2 * pg:]
    g = pl.program_id(1)
    rows = nq * MLA_HEADS
    nk = MLA_HEADS * QK_NOPE

    @pl.when(g == 0)
    def _():
        m_ref[...] = jnp.full((rows, 1), -jnp.inf, F32)
        l_ref[...] = jnp.zeros((rows, 1), F32)
        acc_ref[...] = jnp.zeros((rows, KV_LORA), F32)
        lhs_ref[0:nk, :] = wt_ref[...]
        lhs_ref[nk:nk + rows, :] = qabs_ref[...]

    def project(c16):
        return _dot_nt(lhs_ref[...], c16)

    def scores(both, kpt):
        t = both.shape[1]
        kt = both[0:nk]
        ssq = jnp.sum((kt * kt).reshape(MLA_HEADS, QK_NOPE, t), axis=1)
        ssq_pe = jnp.sum(kpt * kpt, axis=0, keepdims=True)
        r = lax.rsqrt((ssq + ssq_pe) * (1.0 / QK_DIM) + EPS)
        st = both[nk:nk + rows] + _dot(qpe_ref[...], kpt.astype(BF16))
        return (st.reshape(nq, MLA_HEADS, t) * r[None]).reshape(rows, t)

    def update(st, c16s):
        m_old = m_ref[...]
        m_new = jnp.maximum(m_old, jnp.max(st, axis=-1, keepdims=True))
        p = jnp.exp2(st - m_new)
        alpha = jnp.exp2(m_old - m_new)
        l_ref[...] = alpha * l_ref[...] + jnp.sum(p, axis=-1, keepdims=True)
        p16 = p.astype(BF16)
        pv = None
        off = 0
        for c16 in c16s:
            t = c16.shape[0]
            d = _dot(p16[:, off:off + t], c16)
            pv = d if pv is None else pv + d
            off += t
        acc_ref[...] = alpha * acc_ref[...] + pv
        m_ref[...] = m_new

    def latent(k):
        return jnp.concatenate([c_refs[k][...], c_refs[k + 1][...]], axis=0).astype(BF16)

    subs = list(range(0, pg, 2))
    c16s = [latent(subs[0])]
    boths = [project(c16s[0])]
    sts = []
    for i, k in enumerate(subs):
        if i + 1 < len(subs):
            c16s.append(latent(subs[i + 1]))
            boths.append(project(c16s[-1]))
        kpt = jnp.concatenate([p_refs[k][...], p_refs[k + 1][...]], axis=1)
        sts.append(scores(boths[i], kpt))
    update(jnp.concatenate(sts, axis=1), c16s)

    @pl.when(g == ngroups - 1)
    def _():
        qrow = lax.broadcasted_iota(jnp.int32, (rows, PAGE), 0) // MLA_HEADS
        tok = lax.broadcasted_iota(jnp.int32, (rows, PAGE), 1)
        c16 = cnew_ref[...].astype(BF16)
        st = scores(project(c16), pnew_ref[...])
        update(jnp.where(tok <= qrow, st, -jnp.inf), [c16])
        olat = (acc_ref[...] / l_ref[...]).astype(BF16)
        of = _dot(olat, wv_ref[...])
        rh = lax.broadcasted_iota(jnp.int32, (rows, MLA_WIDTH), 0) % MLA_HEADS
        ch = lax.broadcasted_iota(jnp.int32, (rows, MLA_WIDTH), 1) // V_DIM
        of = jnp.where(rh == ch, of, 0.0)
        o_ref[...] = jnp.sum(of.reshape(nq, MLA_HEADS, MLA_WIDTH), axis=1)


def _decode_attn(page_table, pool_c, pool_pt, layer, cnew, pnew_t, qabs, qpe, lw, *, pg=16):
    nb, npages = page_table.shape
    nq = qabs.shape[1] // MLA_HEADS
    rows = nq * MLA_HEADS
    pg = min(pg, npages)
    ngroups = npages // pg

    def page_spec(shape, k):
        return pl.BlockSpec((None, None) + shape, lambda b, g, pt: (layer, pt[b, g * pg + k], 0, 0))

    per_seq = lambda shape: pl.BlockSpec((None,) + shape, lambda b, g, pt: (b, 0, 0))
    full = lambda shape: pl.BlockSpec(shape, lambda b, g, pt: (0,) * len(shape))
    in_specs = ([page_spec((PAGE, KV_LORA), k) for k in range(pg)]
                + [page_spec((QK_ROPE, PAGE), k) for k in range(pg)]
                + [per_seq((PAGE, KV_LORA)), per_seq((QK_ROPE, PAGE)), full((MLA_HEADS * QK_NOPE, KV_LORA)),
                   per_seq((rows, KV_LORA)), per_seq((rows, QK_ROPE)), full((KV_LORA, MLA_WIDTH))])
    grid_spec = pltpu.PrefetchScalarGridSpec(
        num_scalar_prefetch=1,
        grid=(nb, ngroups),
        in_specs=in_specs,
        out_specs=per_seq((nq, MLA_WIDTH)),
        scratch_shapes=[pltpu.VMEM((rows, 1), F32), pltpu.VMEM((rows, 1), F32), pltpu.VMEM((rows, KV_LORA), F32),
                        pltpu.VMEM((MLA_HEADS * QK_NOPE + rows, KV_LORA), BF16)],
    )
    return pl.pallas_call(
        functools.partial(_decode_body, pg=pg, ngroups=ngroups, nq=nq),
        grid_spec=grid_spec,
        out_shape=jax.ShapeDtypeStruct((nb, nq, MLA_WIDTH), F32),
        compiler_params=_cparams("parallel", "arbitrary"),
        name="decode_attn",
    )(page_table, *([pool_c] * pg), *([pool_pt] * pg), cnew, pnew_t, lw["w_nope_t"], qabs, qpe, lw["w_v"])


def _outproj_even_body(y_ref, o_ref, g_ref, h_ref, wy_ref, wo_ref, out_ref):
    og = (o_ref[...] * _silu(g_ref[...])).astype(BF16)
    out_ref[...] = h_ref[...] + _dot(y_ref[...].astype(BF16), wy_ref[...]) + _dot(og, wo_ref[...])


def _outproj_even(y, o, proj, h, w_out, tm=512):
    m = h.shape[0]
    tm = min(tm, m)
    row = lambda c: pl.BlockSpec((tm, D_MODEL), lambda i: (i, c))
    return pl.pallas_call(
        _outproj_even_body,
        grid=(m // tm,),
        in_specs=[row(0), row(0), row(C_G // D_MODEL), row(0),
                  pl.BlockSpec((SSM_D_INNER, D_MODEL), lambda i: (0, 0)),
                  pl.BlockSpec((MLA_WIDTH, D_MODEL), lambda i: (1, 0))],
        out_specs=row(0),
        out_shape=jax.ShapeDtypeStruct((m, D_MODEL), F32),
        compiler_params=_cparams("parallel"),
        name="outproj_even",
    )(y, o, proj, h, w_out, w_out)


def _gelu(x):
    return 0.5 * x * (1.0 + jnp.tanh(math.sqrt(2.0 / math.pi) * (x + 0.044715 * (x * x * x))))


def _s5_seq_body(u_ref, kst_ref, bst_ref, mr_ref, d_ref, are_ref, aim_ref, pre_ref, pim_ref,
                 y_ref, hre_ref, him_ref, xre_ref, xim_ref, ys_ref, *, lc):
    c = pl.program_id(2)
    tile = SUBLANES
    nt = lc // tile
    ns = S5_SLAB_STATE

    @pl.when(c == 0)
    def _():
        xre_ref[0:tile, :] = jnp.zeros((tile, ns), F32)
        xim_ref[0:tile, :] = jnp.zeros((tile, ns), F32)

    u = u_ref[...]
    row = lax.broadcasted_iota(jnp.int32, (lc, LANES), 0) % tile
    parts = [u.astype(BF16)]
    for j in range(1, tile):
        parts.append(jnp.where(row >= j, pltpu.roll(u, j, 0), 0.0).astype(BF16))
    y_local = _dot(jnp.concatenate(parts, axis=1), kst_ref[0])

    tstack = jnp.concatenate(
        [u_ref[pl.ds(tile - 1 - j, nt, stride=tile), :].astype(BF16) for j in range(tile)], axis=1)
    v = _dot(tstack, bst_ref[0])
    re, im = v[:, :ns], v[:, ns:]
    for si in range(3):
        sh = 1 << si
        sre = pltpu.roll(re, sh, 0).reshape(nt // tile, tile, ns)
        sim = pltpu.roll(im, sh, 0).reshape(nt // tile, tile, ns)
        ar = are_ref[0, si][None]
        ai = aim_ref[0, si][None]
        re3 = re.reshape(nt // tile, tile, ns) + ar * sre - ai * sim
        im3 = im.reshape(nt // tile, tile, ns) + ar * sim + ai * sre
        re = re3.reshape(nt, ns)
        im = im3.reshape(nt, ns)
    cr = jnp.broadcast_to(xre_ref[tile - 1:tile, :], (tile, ns))
    ci = jnp.broadcast_to(xim_ref[tile - 1:tile, :], (tile, ns))
    pr = pre_ref[0]
    pi = pim_ref[0]
    for g in range(nt // tile):
        xr = re[g * tile:(g + 1) * tile] + pr * cr - pi * ci
        xi = im[g * tile:(g + 1) * tile] + pr * ci + pi * cr
        xre_ref[(g + 1) * tile:(g + 2) * tile, :] = xr
        xim_ref[(g + 1) * tile:(g + 2) * tile, :] = xi
        cr = jnp.broadcast_to(xr[tile - 1:tile, :], (tile, ns))
        ci = jnp.broadcast_to(xi[tile - 1:tile, :], (tile, ns))
    hre_ref[...] = cr[0:1, :]
    him_ref[...] = ci[0:1, :]

    xp = jnp.concatenate([xre_ref[tile - 1:tile - 1 + nt, :], xim_ref[tile - 1:tile - 1 + nt, :]],
                         axis=1).astype(BF16)
    for r in range(tile):
        ys_ref[pl.ds(r, nt, stride=tile), :] = _dot(xp, mr_ref[0, r])
    xre_ref[0:tile, :] = xre_ref[nt:nt + tile, :]
    xim_ref[0:tile, :] = xim_ref[nt:nt + tile, :]
    y_ref[...] = _gelu(y_local + ys_ref[...] + d_ref[...] * u)


def _s5_seq(proj, lw, *, batch, seq, lc=1024):
    lc = min(lc, seq)
    nc = seq // lc
    ns = S5_SLAB_STATE
    nst = S5_GROUPS * S5_STATE
    kdim = SUBLANES * LANES
    slab = lambda *shape: pl.BlockSpec((1,) + shape, lambda b, j, c: (j,) + (0,) * len(shape))
    return pl.pallas_call(
        functools.partial(_s5_seq_body, lc=lc),
        grid=(batch, S5_SLABS, nc),
        in_specs=[
            pl.BlockSpec((lc, LANES), lambda b, j, c: (b * nc + c, j)),
            slab(kdim, LANES), slab(kdim, 2 * ns), slab(SUBLANES, 2 * ns, LANES),
            pl.BlockSpec((1, LANES), lambda b, j, c: (0, j)),
            slab(3, SUBLANES, ns), slab(3, SUBLANES, ns), slab(SUBLANES, ns), slab(SUBLANES, ns),
        ],
        out_specs=[
            pl.BlockSpec((lc, LANES), lambda b, j, c: (b * nc + c, j)),
            pl.BlockSpec((None, 1, ns), lambda b, j, c: (b, 0, j)),
            pl.BlockSpec((None, 1, ns), lambda b, j, c: (b, 0, j)),
        ],
        out_shape=[
            jax.ShapeDtypeStruct((batch * seq, S5_WIDTH), F32),
            jax.ShapeDtypeStruct((batch, 1, nst), F32),
            jax.ShapeDtypeStruct((batch, 1, nst), F32),
        ],
        scratch_shapes=[pltpu.VMEM((lc // SUBLANES + SUBLANES, ns), F32),
                        pltpu.VMEM((lc // SUBLANES + SUBLANES, ns), F32), pltpu.VMEM((lc, LANES), F32)],
        compiler_params=_cparams("parallel", "arbitrary", "arbitrary"),
        name="s5_seq",
    )(proj, lw["k_stack"], lw["b_stack"], lw["m_rows"], lw["d"], lw["a8_re_hs"], lw["a8_im_hs"],
      lw["p8_re"], lw["p8_im"])


def _s5_step_body(u_ref, h0re_ref, h0im_ref, bre_ref, bim_ref, brel_ref, biml_ref, cre_ref, cim_ref, d_ref,
                  are_ref, aim_ref, y_ref, hre_ref, him_ref, *, nt):
    xr = h0re_ref[...]
    xi = h0im_ref[...]
    ar = are_ref[0]
    ai = aim_ref[0]
    for t in range(nt):
        u = u_ref[t]
        uh = u.astype(BF16)
        ul = (u - uh.astype(F32)).astype(BF16)
        bur = _dot(uh, bre_ref[0]) + _dot(ul, bre_ref[0]) + _dot(uh, brel_ref[0])
        bui = _dot(uh, bim_ref[0]) + _dot(ul, bim_ref[0]) + _dot(uh, biml_ref[0])
        xr, xi = ar * xr - ai * xi + bur, ar * xi + ai * xr + bui
        y = _dot(xr.astype(BF16), cre_ref[0]) + _dot(xi.astype(BF16), cim_ref[0])
        y_ref[t] = _gelu(y + d_ref[...] * u)
    hre_ref[...] = xr
    him_ref[...] = xi


def _s5_step(u_t, h0re, h0im, lw):
    nt, nb, _ = u_t.shape
    ns = S5_SLAB_STATE
    slab3 = lambda a, b: pl.BlockSpec((1, a, b), lambda j: (j, 0, 0))
    return pl.pallas_call(
        functools.partial(_s5_step_body, nt=nt),
        grid=(S5_SLABS,),
        in_specs=[
            pl.BlockSpec((nt, nb, LANES), lambda j: (0, 0, j)),
            pl.BlockSpec((nb, ns), lambda j: (0, j)), pl.BlockSpec((nb, ns), lambda j: (0, j)),
            slab3(LANES, ns), slab3(LANES, ns), slab3(LANES, ns), slab3(LANES, ns),
            slab3(ns, LANES), slab3(ns, LANES),
            pl.BlockSpec((1, LANES), lambda j: (0, j)),
            slab3(1, ns), slab3(1, ns),
        ],
        out_specs=[
            pl.BlockSpec((nt, nb, LANES), lambda j: (0, 0, j)),
            pl.BlockSpec((nb, ns), lambda j: (0, j)), pl.BlockSpec((nb, ns), lambda j: (0, j)),
        ],
        out_shape=[
            jax.ShapeDtypeStruct((nt, nb, S5_WIDTH), F32),
            jax.ShapeDtypeStruct((nb, S5_GROUPS * S5_STATE), F32),
            jax.ShapeDtypeStruct((nb, S5_GROUPS * S5_STATE), F32),
        ],
        compiler_params=_cparams("parallel"),
        name="s5_step",
    )(u_t, h0re, h0im, lw["b_re"], lw["b_im"], lw["b_re_lo"], lw["b_im_lo"], lw["c_re"], lw["c_im"], lw["d"],
      lw["a_re1"], lw["a_im1"])


def _odd_tail_body(y_ref, z_ref, h_ref, wg_ref, bg_ref, wo_ref, out_ref):
    y = y_ref[...]
    gl = _dot(y.astype(BF16), wg_ref[...]) + bg_ref[...]
    y = y * (1.0 / (1.0 + jnp.exp(-gl)))
    y = y * _silu(z_ref[...])
    out_ref[...] = h_ref[...] + _dot(y.astype(BF16), wo_ref[...])


def _odd_tail(y, proj, h, lw, tm=512):
    m = h.shape[0]
    tm = min(tm, m)
    row = lambda c: pl.BlockSpec((tm, D_MODEL), lambda i: (i, c))
    full = lambda shape: pl.BlockSpec(shape, lambda i: (0,) * len(shape))
    return pl.pallas_call(
        _odd_tail_body,
        grid=(m // tm,),
        in_specs=[row(0), row(1), row(0), full((S5_WIDTH, S5_WIDTH)), full((1, S5_WIDTH)),
                  full((S5_WIDTH, D_MODEL))],
        out_specs=row(0),
        out_shape=jax.ShapeDtypeStruct((m, D_MODEL), F32),
        compiler_params=_cparams("parallel"),
        name="odd_tail",
    )(y, proj, h, lw["w_glu"], lw["b_glu"], lw["w_out"])


def _rot_half_cols(w):
    half = QK_ROPE // 2
    return jnp.concatenate([-w[..., half:], w[..., :half]], axis=-1)


def _head_pad(x_nope, x_rope):
    z = jnp.zeros(x_nope.shape[:-1] + (LANES - QK_DIM,), x_nope.dtype)
    out = jnp.concatenate([x_nope, x_rope, z], axis=-1)
    return out.reshape(out.shape[:-2] + (MLA_HEADS * LANES,))


def _even_weights(w_in, conv_w, conv_b, dt_bias, a_log, d_ssm, ssm_norm_w, q_a_norm_w, w_qb, kv_a_norm_w, w_kvb,
                  q_norm_w, k_norm_w, w_out):
    k = w_in.shape[0]
    o = 0
    parts = {}
    for name, sz in (("z", SSM_D_INNER), ("xbc", SSM_CONV_CH), ("dt", SSM_HEADS), ("ql", Q_LORA), ("kvl", KV_LORA),
                     ("kpe", QK_ROPE), ("g", MLA_WIDTH)):
        parts[name] = w_in[:, o:o + sz]
        o += sz
    zc = lambda n: jnp.zeros((k, n), F32)
    kpe_blk = jnp.concatenate([zc(QK_NOPE), parts["kpe"], zc(LANES - QK_DIM)], axis=1)
    kpesw_blk = jnp.concatenate([zc(QK_NOPE), _rot_half_cols(parts["kpe"]), zc(LANES - QK_DIM)], axis=1)
    dt_blk = jnp.concatenate([parts["dt"], zc(LANES - SSM_HEADS)], axis=1)
    w_in_p = jnp.concatenate([parts["xbc"], parts["z"], parts["g"], parts["ql"], parts["kvl"], kpe_blk, kpesw_blk,
                              dt_blk, zc(N_EVEN_PAD - C_DT - LANES)], axis=1).astype(BF16)
    wq = w_qb.reshape(Q_LORA, MLA_HEADS, QK_DIM)
    zq = jnp.zeros((Q_LORA, MLA_HEADS, QK_NOPE), F32)
    wkv = w_kvb.reshape(KV_LORA, MLA_HEADS, QK_NOPE + V_DIM)
    w_nope = wkv[..., :QK_NOPE]
    pad1 = lambda v, n: jnp.concatenate([v, jnp.zeros((n - v.shape[0],), F32)]).reshape(1, n)
    norm_pat = lambda w: jnp.concatenate([w, jnp.zeros((LANES - QK_DIM,), F32)]).reshape(1, LANES)
    w_abs = jnp.concatenate([jnp.transpose(w_nope, (1, 2, 0)),
                             jnp.zeros((MLA_HEADS, LANES - QK_NOPE, KV_LORA), F32)], axis=1)
    e_pe = jnp.zeros((LANES, LANES), F32).at[QK_NOPE + jnp.arange(QK_ROPE), jnp.arange(QK_ROPE)].set(1.0)
    head_expand = (jnp.arange(LANES)[:, None] == (jnp.arange(SSM_D_INNER)[None, :] // SSM_HEAD_DIM))
    return {
        "w_in": w_in_p,
        "conv_w": conv_w, "conv_b": conv_b.reshape(1, -1),
        "dt_bias": pad1(dt_bias, LANES), "a_log": pad1(a_log, LANES),
        "d_exp": jnp.repeat(d_ssm, SSM_HEAD_DIM).reshape(1, -1),
        "ssm_norm_w": ssm_norm_w.reshape(1, -1),
        "head_expand": head_expand.astype(BF16),
        "q_a_norm_w": q_a_norm_w.reshape(1, -1),
        "w_q": _head_pad(wq[..., :QK_NOPE], wq[..., QK_NOPE:]).astype(BF16),
        "w_qsw": _head_pad(zq, _rot_half_cols(wq[..., QK_NOPE:])).astype(BF16),
        "q_norm_pat": norm_pat(q_norm_w),
        "kv_a_norm_w": kv_a_norm_w.reshape(1, -1),
        "w_k": _head_pad(w_nope, jnp.zeros((KV_LORA, MLA_HEADS, QK_ROPE), F32)).astype(BF16),
        "w_v": wkv[..., QK_NOPE:].reshape(KV_LORA, MLA_WIDTH).astype(BF16),
        "w_v_t": wkv[..., QK_NOPE:].reshape(KV_LORA, MLA_WIDTH).T.astype(BF16),
        "k_norm_pat": norm_pat(k_norm_w),
        "w_abs": w_abs.astype(BF16),
        "e_pe": e_pe.astype(BF16),
        "w_nope_t": jnp.transpose(w_nope, (1, 2, 0)).reshape(MLA_HEADS * QK_NOPE, KV_LORA).astype(BF16),
        "w_out": w_out.astype(BF16),
    }


def _rope_tables(pos):
    half = QK_ROPE // 2
    inv_freq = jnp.power(ROPE_THETA, -jnp.arange(half, dtype=F32) / half)
    ang = pos[:, None] * inv_freq[None, :]
    c, s = jnp.cos(ang), jnp.sin(ang)
    n = pos.shape[0]
    cos_t = jnp.concatenate([jnp.ones((n, QK_NOPE), F32), c, c, jnp.ones((n, LANES - QK_DIM), F32)], axis=1)
    sin_t = jnp.concatenate([jnp.zeros((n, QK_NOPE), F32), s, s, jnp.zeros((n, LANES - QK_DIM), F32)], axis=1)
    return cos_t, sin_t


def _odd_weights(w_in, a_re, a_im, b_re, b_im, c_re, c_im, d, log_step, w_glu, b_glu, w_out):
    ar, ai = a_re.astype(F32), a_im.astype(F32)
    step = jnp.exp(log_step.astype(F32))[:, None]
    mag = jnp.exp(ar * step)
    ab_re, ab_im = mag * jnp.cos(ai * step), mag * jnp.sin(ai * step)
    den = ar * ar + ai * ai
    nr, ni = ab_re - 1.0, ab_im
    f_re = (nr * ar + ni * ai) / den
    f_im = (ni * ar - nr * ai) / den
    bb_re = f_re[..., None] * b_re - f_im[..., None] * b_im
    bb_im = f_re[..., None] * b_im + f_im[..., None] * b_re
    gl = LANES // S5_GROUP_CH

    eye = jnp.eye(gl, dtype=F32)

    def b_blocks(bb):
        x = bb.reshape(-1, S5_SLABS, gl, S5_STATE, S5_GROUP_CH)
        return jnp.einsum("jsgnc,gh->sjgchn", x, eye).reshape(S5_SLABS, -1, S5_SLAB_STATE)

    def c_blocks(cc):
        x = cc.reshape(-1, S5_SLABS, gl, S5_GROUP_CH, S5_STATE)
        return jnp.einsum("jsgcn,gh->sjgnhc", x, eye).reshape(S5_SLABS, -1, S5_SLAB_STATE, LANES)

    def k_blocks(kk):
        x = kk.reshape(-1, S5_SLABS, gl, S5_GROUP_CH, S5_GROUP_CH)
        return jnp.einsum("jsgcd,gh->sjgdhc", x, eye).reshape(S5_SLABS, -1, LANES)

    def powers(base_re, base_im, n):
        pr, pi = jnp.ones_like(base_re), jnp.zeros_like(base_im)
        out = []
        for _ in range(n):
            pr, pi = pr * base_re - pi * base_im, pr * base_im + pi * base_re
            out.append((pr, pi))
        return out

    pw = powers(ab_re, ab_im, SUBLANES)
    p_re = jnp.stack([jnp.ones_like(ab_re)] + [p[0] for p in pw])
    p_im = jnp.stack([jnp.zeros_like(ab_im)] + [p[1] for p in pw])
    cr, ci = c_re.astype(F32), c_im.astype(F32)
    t_re, t_im = p_re[:SUBLANES, :, :, None], p_im[:SUBLANES, :, :, None]
    bj_re = t_re * bb_re - t_im * bb_im
    bj_im = t_re * bb_im + t_im * bb_re
    b_stack = jnp.concatenate([b_blocks(bj_re), b_blocks(bj_im)], axis=2)
    hi = lax.Precision.HIGHEST
    k_stack = k_blocks(jnp.einsum("gcn,jgnd->jgcd", cr, bj_re, precision=hi)
                       - jnp.einsum("gcn,jgnd->jgcd", ci, bj_im, precision=hi))
    q_re, q_im = p_re[1:, :, None, :], p_im[1:, :, None, :]
    m_rows = jnp.concatenate([c_blocks(cr * q_re - ci * q_im), c_blocks(-(cr * q_im + ci * q_re))], axis=2)
    pw8 = powers(pw[-1][0], pw[-1][1], SUBLANES)
    slab = lambda x: x.reshape(S5_SLABS, S5_SLAB_STATE)
    rows = jnp.arange(SUBLANES)[None, :, None]
    hs8_re = jnp.stack([jnp.where(rows >= sh, slab(pw8[sh - 1][0])[:, None, :], 0.0) for sh in (1, 2, 4)], axis=1)
    hs8_im = jnp.stack([jnp.where(rows >= sh, slab(pw8[sh - 1][1])[:, None, :], 0.0) for sh in (1, 2, 4)], axis=1)
    p8_re = jnp.stack([slab(p[0]) for p in pw8], axis=1)
    p8_im = jnp.stack([slab(p[1]) for p in pw8], axis=1)
    bre_f, bim_f = b_blocks(bb_re), b_blocks(bb_im)
    bre16, bim16 = bre_f.astype(BF16), bim_f.astype(BF16)
    return {
        "w_in": w_in.astype(BF16),
        "b_re": bre16, "b_im": bim16,
        "b_re_lo": (bre_f - bre16.astype(F32)).astype(BF16), "b_im_lo": (bim_f - bim16.astype(F32)).astype(BF16),
        "c_re": c_blocks(cr)[:, 0].astype(BF16), "c_im": c_blocks(-ci)[:, 0].astype(BF16),
        "d": d.reshape(1, -1),
        "k_stack": k_stack.astype(BF16), "b_stack": b_stack.astype(BF16), "m_rows": m_rows.astype(BF16),
        "a8_re_hs": hs8_re, "a8_im_hs": hs8_im, "p8_re": p8_re, "p8_im": p8_im,
        "a_re1": slab(ab_re).reshape(S5_SLABS, 1, S5_SLAB_STATE),
        "a_im1": slab(ab_im).reshape(S5_SLABS, 1, S5_SLAB_STATE),
        "w_glu": w_glu.astype(BF16), "b_glu": b_glu.reshape(1, -1), "w_out": w_out.astype(BF16),
    }


SAMPLE_ROWS = 8


def _even_layer(hp, hs, norm_w, lw, tabs, state_ssm, state_conv, pool_c, pool_pt, layer, page_table, bp, sp, bs, ss):
    cos_p, sin_p, cos_s, sin_s = tabs
    hist = SSM_CONV - 1
    proj_p = _inproj(hp, norm_w, lw["w_in"])
    proj_s = _inproj(hs, norm_w, lw["w_in"])
    nchunks = sp // SSM_CHUNK
    zero_state = jnp.zeros((bp, SSM_D_INNER, SSM_STATE), F32)
    y_p, ssm_p = _ssd(proj_p, proj_p, proj_p, zero_state, lw, nblk=bp * nchunks, nchunks=nchunks, nseq=1,
                      P=SSM_CHUNK, n_hist=0, n_real=SSM_CHUNK, carry=True,
                      xbc_col=0, z_col=C_Z // SSM_D_INNER, dt_col=C_DT // LANES)
    conv_p = proj_p.reshape(bp, sp, N_EVEN_PAD)[:, sp - hist:, C_XBC:C_XBC + SSM_CONV_CH]
    xbc_s = proj_s[:, C_XBC:C_XBC + SSM_CONV_CH].reshape(bs, ss, SSM_CONV_CH)
    xbc_full = jnp.concatenate([state_conv, xbc_s], axis=1)
    pad_rows = SAMPLE_ROWS - hist - ss
    pad3 = lambda a: jnp.pad(a, ((0, 0), (hist, pad_rows), (0, 0)))
    xbc_pad = jnp.pad(xbc_full, ((0, 0), (0, pad_rows), (0, 0))).reshape(bs * SAMPLE_ROWS, SSM_CONV_CH)
    z_pad = pad3(proj_s[:, C_Z:C_Z + SSM_D_INNER].reshape(bs, ss, -1)).reshape(bs * SAMPLE_ROWS, -1)
    dt_pad = pad3(proj_s[:, C_DT:C_DT + LANES].reshape(bs, ss, -1)).reshape(bs * SAMPLE_ROWS, -1)
    nseq = SSM_CHUNK // SAMPLE_ROWS
    y_s_pad, ssm_s = _ssd(xbc_pad, z_pad, dt_pad, state_ssm.reshape(-1, SSM_D_INNER, SSM_STATE), lw,
                          nblk=bs // nseq, nchunks=1, nseq=nseq, P=SAMPLE_ROWS, n_hist=hist, n_real=ss,
                          carry=False, xbc_col=0, z_col=0, dt_col=0, h0_offset=layer * bs)
    y_s = y_s_pad.reshape(bs, SAMPLE_ROWS, SSM_D_INNER)[:, hist:hist + ss].reshape(bs * ss, SSM_D_INNER)
    conv_s = xbc_full[:, ss:]
    q_p, ckv_p, kper_p, k_p, v_p = _mla_proj(proj_p, cos_p, sin_p, lw, with_kv=True, tm=FLASH_TK)
    q_s, ckv_s, kper_s = _mla_proj(proj_s, cos_s, sin_s, lw, with_kv=False)
    o_p = _flash(q_p, k_p, v_p, batch=bp, seq=sp)
    kpe_p = kper_p[:, QK_NOPE:QK_DIM]
    kpe_s = kper_s[:, QK_NOPE:QK_DIM]
    qabs, qpe = _qabs(q_s, lw)
    qabs = qabs.reshape(bs, ss * MLA_HEADS, KV_LORA)
    qpe = qpe.reshape(bs, ss * MLA_HEADS, LANES)[:, :, :QK_ROPE]
    cnew = jnp.pad(ckv_s.reshape(bs, ss, KV_LORA), ((0, 0), (0, PAGE - ss), (0, 0)))
    pnew_t = jnp.pad(jnp.swapaxes(kpe_s.reshape(bs, ss, QK_ROPE), 1, 2), ((0, 0), (0, 0), (0, PAGE - ss)))
    o_s = _decode_attn(page_table, pool_c, pool_pt, layer, cnew, pnew_t, qabs, qpe, lw).reshape(bs * ss, MLA_WIDTH)
    hp = _outproj_even(y_p, o_p, proj_p, hp, lw["w_out"])
    hs = _outproj_even(y_s, o_s, proj_s, hs, lw["w_out"])
    outs = (ckv_p.reshape(bp, sp, KV_LORA), kpe_p.reshape(bp, sp, QK_ROPE), ckv_s.reshape(bs, ss, KV_LORA),
            kpe_s.reshape(bs, ss, QK_ROPE), ssm_p.reshape(bp, SSM_HEADS, SSM_HEAD_DIM, SSM_STATE),
            ssm_s.reshape(bs, SSM_HEADS, SSM_HEAD_DIM, SSM_STATE), conv_p, conv_s)
    return hp, hs, outs


def _odd_layer(hp, hs, norm_w, lw, s5_re, s5_im, bp, sp, bs, ss):
    proj_p = _inproj(hp, norm_w, lw["w_in"])
    proj_s = _inproj(hs, norm_w, lw["w_in"])
    y_p, r_p, i_p = _s5_seq(proj_p, lw, batch=bp, seq=sp)
    u_t = jnp.transpose(proj_s[:, :S5_WIDTH].reshape(bs, ss, S5_WIDTH), (1, 0, 2))
    y_t, r_s, i_s = _s5_step(u_t, s5_re.reshape(bs, -1), s5_im.reshape(bs, -1), lw)
    y_s = jnp.transpose(y_t, (1, 0, 2)).reshape(bs * ss, S5_WIDTH)
    hp = _odd_tail(y_p, proj_p, hp, lw)
    hs = _odd_tail(y_s, proj_s, hs, lw)
    shp = lambda a, b: a.reshape(b, S5_GROUPS, S5_STATE)
    return hp, hs, (shp(r_p, bp), shp(i_p, bp), shp(r_s, bs), shp(i_s, bs))


def kernel(x_prompt, x_sample, cache_ckv, cache_kpe, page_table, state_ssm, state_conv, state_s5_re, state_s5_im, norm_w, w_in_even, conv_w, conv_b, dt_bias, a_log, d_ssm, ssm_norm_w, q_a_norm_w, w_qb, kv_a_norm_w, w_kvb, q_norm_w, k_norm_w, w_out_even, w_in_odd, s5_a_re, s5_a_im, s5_b_re, s5_b_im, s5_c_re, s5_c_im, s5_d, s5_log_step, w_glu, b_glu, w_out_odd):
    bp, sp, _ = x_prompt.shape
    bs, ss, _ = x_sample.shape
    past_len = page_table.shape[1] * PAGE
    depth = norm_w.shape[0]
    pos_p = jnp.tile(jnp.arange(sp, dtype=F32), bp)
    pos_s = jnp.tile(past_len + jnp.arange(ss, dtype=F32), bs)
    tabs = _rope_tables(pos_p) + _rope_tables(pos_s)
    cache_kpe_t = jnp.swapaxes(cache_kpe, 2, 3)
    hp = x_prompt.reshape(bp * sp, D_MODEL)
    hs = x_sample.reshape(bs * ss, D_MODEL)
    even_out, odd_out = [], []
    even_w = jax.vmap(_even_weights)(w_in_even, conv_w, conv_b, dt_bias, a_log, d_ssm, ssm_norm_w, q_a_norm_w, w_qb,
                                     kv_a_norm_w, w_kvb, q_norm_w, k_norm_w, w_out_even)
    odd_w = jax.vmap(_odd_weights)(w_in_odd, s5_a_re, s5_a_im, s5_b_re, s5_b_im, s5_c_re, s5_c_im, s5_d,
                                   s5_log_step, w_glu, b_glu, w_out_odd)
    for i in range(depth):
        j = i // 2
        if i % 2 == 0:
            lw = {k: v[j] for k, v in even_w.items()}
            hp, hs, o = _even_layer(hp, hs, norm_w[i], lw, tabs, state_ssm, state_conv[j], cache_ckv,
                                    cache_kpe_t, j, page_table, bp, sp, bs, ss)
            even_out.append(o)
        else:
            lw = {k: v[j] for k, v in odd_w.items()}
            hp, hs, o = _odd_layer(hp, hs, norm_w[i], lw, state_s5_re[j], state_s5_im[j], bp, sp, bs, ss)
            odd_out.append(o)
    ev = [jnp.stack([o[k] for o in even_out]) for k in range(8)]
    od = [jnp.stack([o[k] for o in odd_out]) for k in range(4)]
    return (hp.reshape(bp, sp, D_MODEL), hs.reshape(bs, ss, D_MODEL),
            ev[0], ev[1], ev[2], ev[3], ev[4], ev[5], ev[6], ev[7], od[0], od[1], od[2], od[3])
```

```python
import functools
import math

import jax
import jax.numpy as jnp
from jax import lax
from jax.experimental import pallas as pl
from jax.experimental.pallas import tpu as pltpu

F32 = jnp.float32
BF16 = jnp.bfloat16
EPS = 1e-6

D_MODEL = 1024
LANES = 128
SUBLANES = 8
VMEM_LIMIT = 48 * 1024 * 1024

SSM_D_INNER = 1024
SSM_HEAD_DIM = 64
SSM_HEADS = 16
SSM_GROUPS = 4
SSM_STATE = 128
SSM_CONV = 4
SSM_CHUNK = 128
SSM_GN = SSM_GROUPS * SSM_STATE
SSM_CONV_CH = SSM_D_INNER + 2 * SSM_GN

MLA_HEADS = 16
Q_LORA = 256
KV_LORA = 256
QK_NOPE = 64
QK_ROPE = 32
QK_DIM = QK_NOPE + QK_ROPE
V_DIM = 64
MLA_WIDTH = MLA_HEADS * V_DIM
ROPE_THETA = 10000.0
PAGE = 128
FLASH_TK = 512
FLASH_TN = 256
FLASH_DEPTH = 6
FLASH_LROWS = 16
DECODE_SUB = 8

S5_WIDTH = 1024
S5_GROUP_CH = 16
S5_GROUPS = 64
S5_STATE = 64
S5_SLABS = S5_WIDTH // LANES
S5_SLAB_STATE = (LANES // S5_GROUP_CH) * S5_STATE

C_XBC, C_Z, C_G, C_QL, C_KVL, C_KPE, C_KPESW, C_DT, N_EVEN_PAD = 0, 2048, 3072, 4096, 4352, 4608, 4736, 4864, 5120


def _cparams(*sem):
    return pltpu.CompilerParams(dimension_semantics=sem, vmem_limit_bytes=VMEM_LIMIT)


def _silu(x):
    return x * (1.0 / (1.0 + jnp.exp(-x)))


def _dot(a, b):
    return jnp.dot(a, b, preferred_element_type=F32)


def _dot_nt(a, b):
    return lax.dot_general(a, b, (((1,), (1,)), ((), ())), preferred_element_type=F32)


def _split3(v):
    hi = v.astype(BF16)
    r1 = v - hi.astype(F32)
    mid = r1.astype(BF16)
    lo = (r1 - mid.astype(F32)).astype(BF16)
    return hi, mid, lo


def _dot_exact_lhs(m_bf16, v):
    hi, mid, lo = _split3(v)
    return _dot(m_bf16, hi) + _dot(m_bf16, mid) + _dot(m_bf16, lo)


def _dot_exact_rhs(v, m_bf16):
    hi, mid, lo = _split3(v)
    return _dot(hi, m_bf16) + _dot(mid, m_bf16) + _dot(lo, m_bf16)


def _inproj_body(x_ref, nw_ref, w_ref, o_ref, xn_ref):
    @pl.when(pl.program_id(1) == 0)
    def _():
        x = x_ref[...]
        ms = jnp.mean(x * x, axis=-1, keepdims=True)
        xn_ref[...] = (x * lax.rsqrt(ms + EPS) * nw_ref[...]).astype(BF16)

    o_ref[...] = _dot(xn_ref[...], w_ref[...])


def _inproj(x, norm_w, w, tm=512, tn=1024):
    m, k = x.shape
    n = w.shape[1]
    tm = min(tm, m)
    return pl.pallas_call(
        _inproj_body,
        grid=(m // tm, n // tn),
        in_specs=[
            pl.BlockSpec((tm, k), lambda i, j: (i, 0)),
            pl.BlockSpec((1, k), lambda i, j: (0, 0)),
            pl.BlockSpec((k, tn), lambda i, j: (0, j)),
        ],
        out_specs=pl.BlockSpec((tm, tn), lambda i, j: (i, j)),
        out_shape=jax.ShapeDtypeStruct((m, n), F32),
        scratch_shapes=[pltpu.VMEM((tm, k), BF16)],
        compiler_params=_cparams("parallel", "arbitrary"),
        name="inproj",
    )(x, norm_w.reshape(1, k), w)


def _ssd_body(xbc_ref, z_ref, dt_ref, h0_ref, convw_ref, convb_ref, dtb_ref, alog_ref, dexp_ref, nw_ref, e_ref,
              y_ref, hout_ref,
              xp_ref, ysc_ref, xwt_ref, b_ref, c_ref, cse_ref, tott_ref,
              *, L, P, n_hist, n_real, nseq, nchunks, carry):
    blk = pl.program_id(0)
    s = pl.program_id(1)
    chunk = blk % nchunks
    HIST = SUBLANES

    @pl.when(s == 0)
    def _intra():
        if carry:
            @pl.when(chunk == 0)
            def _():
                xp_ref[0:HIST, :] = jnp.zeros((HIST, SSM_CONV_CH), F32)
        else:
            xp_ref[0:HIST, :] = jnp.zeros((HIST, SSM_CONV_CH), F32)
        xp_ref[HIST:HIST + L, :] = xbc_ref[...]
        conv = convb_ref[...] + convw_ref[3:4, :] * xp_ref[HIST:HIST + L, :]
        for k in range(SSM_CONV - 1):
            off = HIST - (SSM_CONV - 1) + k
            conv = conv + convw_ref[k:k + 1, :] * xp_ref[off:off + L, :]
        if carry:
            xp_ref[0:HIST, :] = xp_ref[L:L + HIST, :]
        xc = _silu(conv)
        xs = xc[:, :SSM_D_INNER]
        b_ref[...] = xc[:, SSM_D_INNER:SSM_D_INNER + SSM_GN].astype(BF16)
        c_ref[...] = xc[:, SSM_D_INNER + SSM_GN:].astype(BF16)

        raw = dt_ref[...] + dtb_ref[...]
        dt = jnp.maximum(raw, 0.0) + jnp.log1p(jnp.exp(-jnp.abs(raw)))
        ri = lax.broadcasted_iota(jnp.int32, (L, L), 0)
        ci = lax.broadcasted_iota(jnp.int32, (L, L), 1)
        if P < L:
            rp = lax.broadcasted_iota(jnp.int32, (L, LANES), 0) % P
            dt = jnp.where((rp >= n_hist) & (rp < n_hist + n_real), dt, 0.0)
            same = (ri // P) == (ci // P)
            causal = same & (ci <= ri)
        else:
            same = ri >= 0
            causal = ci <= ri
        a_neg = -jnp.exp(alog_ref[...])
        da = dt * a_neg
        m_cum = jnp.where(causal, 1.0, 0.0).astype(BF16)
        m_tot = jnp.where(same, 1.0, 0.0).astype(BF16)
        cs = _dot_exact_lhs(m_cum, da)
        tot = _dot_exact_lhs(m_tot, da)
        cst = cs.T
        e = e_ref[...]
        dt_e = _dot_exact_rhs(dt, e)
        cs_e = _dot_exact_rhs(cs, e)
        tot_e = _dot_exact_rhs(tot, e)
        cse_ref[...] = cs_e
        tott_ref[...] = tot_e.T
        xdt = xs * dt_e
        xw = xdt * jnp.exp(tot_e - cs_e)
        xwt_ref[...] = xw.T.astype(BF16)
        xdt16 = xdt.astype(BF16)
        lane = lax.broadcasted_iota(jnp.int32, (L, LANES), 1)
        for g in range(SSM_GROUPS):
            cb = _dot_nt(c_ref[:, g * SSM_STATE:(g + 1) * SSM_STATE], b_ref[:, g * SSM_STATE:(g + 1) * SSM_STATE])
            for pr in range(2):
                h0i = 4 * g + 2 * pr
                col = h0i // 2
                xpair = xdt16[:, col * LANES:(col + 1) * LANES]
                ys = []
                for hh in (h0i, h0i + 1):
                    dec = jnp.exp(jnp.where(causal, cs[:, hh:hh + 1] - cst[hh:hh + 1, :], -1e30))
                    ys.append(_dot((cb * dec).astype(BF16), xpair))
                ypair = jnp.where(lane < SSM_HEAD_DIM, ys[0], ys[1])
                ysc_ref[:, col * LANES:(col + 1) * LANES] = (
                    ypair + dexp_ref[:, col * LANES:(col + 1) * LANES] * xs[:, col * LANES:(col + 1) * LANES])

    if carry:
        @pl.when(chunk == 0)
        def _():
            hout_ref[...] = h0_ref[...]
    else:
        hout_ref[...] = h0_ref[...]

    if nseq > 1:
        rmask = (lax.broadcasted_iota(jnp.int32, (L, LANES), 0) // P) == s
        cmask = (lax.broadcasted_iota(jnp.int32, (LANES, L), 1) // P) == s
        onehot = lax.broadcasted_iota(jnp.int32, (LANES, L), 1) == s * P
    for col in range(SSM_HEADS // 2):
        g = col // 2
        sp = hout_ref[col * LANES:(col + 1) * LANES, :]
        yoff = _dot_nt(c_ref[:, g * SSM_STATE:(g + 1) * SSM_STATE], sp.astype(BF16))
        yoff = yoff * jnp.exp(cse_ref[:, col * LANES:(col + 1) * LANES])
        xwt = xwt_ref[col * LANES:(col + 1) * LANES, :]
        tott = tott_ref[col * LANES:(col + 1) * LANES, :]
        if nseq > 1:
            yoff = jnp.where(rmask, yoff, 0.0)
            xwt = jnp.where(cmask, xwt, jnp.zeros_like(xwt))
            deccol = jnp.exp(jnp.sum(jnp.where(onehot, tott, 0.0), axis=1, keepdims=True))
        else:
            deccol = jnp.exp(tott[:, 0:1])
        ysc_ref[:, col * LANES:(col + 1) * LANES] += yoff
        hout_ref[col * LANES:(col + 1) * LANES, :] = sp * deccol + _dot(xwt, b_ref[:, g * SSM_STATE:(g + 1) * SSM_STATE])

    @pl.when(s == nseq - 1)
    def _epilogue():
        gw = SSM_D_INNER // SSM_GROUPS
        for g in range(SSM_GROUPS):
            y = ysc_ref[:, g * gw:(g + 1) * gw] * _silu(z_ref[:, g * gw:(g + 1) * gw])
            ms = jnp.mean(y * y, axis=-1, keepdims=True)
            y_ref[:, g * gw:(g + 1) * gw] = y * lax.rsqrt(ms + EPS) * nw_ref[:, g * gw:(g + 1) * gw]


def _ssd(xbc_src, z_src, dt_src, h0, lw, *, nblk, nchunks, nseq, P, n_hist, n_real, carry,
         xbc_col, z_col, dt_col, h0_offset=0):
    L = SSM_CHUNK
    if carry:
        nstate = nblk // nchunks
        state_idx = lambda b, s: (b // nchunks, 0, 0)
    else:
        nstate = nblk * nseq
        state_idx = lambda b, s: (b * nseq + s, 0, 0)
    h0_idx = lambda b, s: (h0_offset + state_idx(b, s)[0], 0, 0)
    full = lambda shape: pl.BlockSpec(shape, lambda b, s: (0,) * len(shape))
    body = functools.partial(_ssd_body, L=L, P=P, n_hist=n_hist, n_real=n_real, nseq=nseq, nchunks=nchunks,
                             carry=carry)
    return pl.pallas_call(
        body,
        grid=(nblk, nseq),
        in_specs=[
            pl.BlockSpec((L, SSM_CONV_CH), lambda b, s: (b, xbc_col)),
            pl.BlockSpec((L, SSM_D_INNER), lambda b, s: (b, z_col)),
            pl.BlockSpec((L, LANES), lambda b, s: (b, dt_col)),
            pl.BlockSpec((None, SSM_D_INNER, SSM_STATE), h0_idx),
            full((SSM_CONV, SSM_CONV_CH)),
            full((1, SSM_CONV_CH)),
            full((1, LANES)),
            full((1, LANES)),
            full((1, SSM_D_INNER)),
            full((1, SSM_D_INNER)),
            full((LANES, SSM_D_INNER)),
        ],
        out_specs=[
            pl.BlockSpec((L, SSM_D_INNER), lambda b, s: (b, 0)),
            pl.BlockSpec((None, SSM_D_INNER, SSM_STATE), state_idx),
        ],
        out_shape=[
            jax.ShapeDtypeStruct((nblk * L, SSM_D_INNER), F32),
            jax.ShapeDtypeStruct((nstate, SSM_D_INNER, SSM_STATE), F32),
        ],
        scratch_shapes=[
            pltpu.VMEM((L + SUBLANES, SSM_CONV_CH), F32),
            pltpu.VMEM((L, SSM_D_INNER), F32),
            pltpu.VMEM((SSM_D_INNER, L), BF16),
            pltpu.VMEM((L, SSM_GN), BF16),
            pltpu.VMEM((L, SSM_GN), BF16),
            pltpu.VMEM((L, SSM_D_INNER), F32),
            pltpu.VMEM((SSM_D_INNER, L), F32),
        ],
        compiler_params=_cparams("arbitrary", "arbitrary"),
        name="ssd",
    )(xbc_src, z_src, dt_src, h0, lw["conv_w"], lw["conv_b"], lw["dt_bias"], lw["a_log"], lw["d_exp"],
      lw["ssm_norm_w"], lw["head_expand"])


def _mla_proj_body(ql_ref, kvl_ref, kpe_ref, kpesw_ref, cos_ref, sin_ref, qanw_ref, wq_ref, wqsw_ref, qnw_ref,
                   kvanw_ref, wk_ref, wv_ref, knw_ref,
                   q_ref, ckv_ref, kper_ref, *kv_refs, with_kv):
    cos = cos_ref[...]
    sin = sin_ref[...]
    ql = ql_ref[...]
    qn = (ql * lax.rsqrt(jnp.mean(ql * ql, axis=-1, keepdims=True) + EPS) * qanw_ref[...]).astype(BF16)
    q0 = _dot(qn, wq_ref[...])
    q1 = _dot(qn, wqsw_ref[...])
    scale = QK_DIM ** -0.5 * math.log2(math.e)
    for h in range(MLA_HEADS):
        sl = slice(h * LANES, (h + 1) * LANES)
        qh = q0[:, sl] * cos + q1[:, sl] * sin
        ms = jnp.sum(qh * qh, axis=-1, keepdims=True) * (1.0 / QK_DIM)
        q_ref[:, sl] = (qh * lax.rsqrt(ms + EPS) * (qnw_ref[...] * scale)).astype(BF16)
    kvl = kvl_ref[...]
    ckv = kvl * lax.rsqrt(jnp.mean(kvl * kvl, axis=-1, keepdims=True) + EPS) * kvanw_ref[...]
    ckv_ref[...] = ckv
    kper = kpe_ref[...] * cos + kpesw_ref[...] * sin
    kper_ref[...] = kper
    if with_kv:
        k_ref, v_ref = kv_refs
        c16 = ckv.astype(BF16)
        kn = _dot(c16, wk_ref[...])
        for h in range(MLA_HEADS):
            sl = slice(h * LANES, (h + 1) * LANES)
            kh = kn[:, sl] + kper
            ms = jnp.sum(kh * kh, axis=-1, keepdims=True) * (1.0 / QK_DIM)
            k_ref[:, sl] = (kh * lax.rsqrt(ms + EPS) * knw_ref[...]).astype(BF16)
        v_ref[...] = _dot_nt(wv_ref[...], c16).astype(BF16)


def _mla_proj(proj, cos_t, sin_t, lw, *, with_kv, tm=256):
    m = proj.shape[0]
    tm = min(tm, m)
    hp = MLA_HEADS * LANES
    full = lambda shape: pl.BlockSpec(shape, lambda i: (0,) * len(shape))
    row = lambda w, c: pl.BlockSpec((tm, w), lambda i: (i, c))
    out_specs = [row(hp, 0), row(KV_LORA, 0), row(LANES, 0)]
    out_shape = [jax.ShapeDtypeStruct((m, hp), BF16), jax.ShapeDtypeStruct((m, KV_LORA), F32),
                 jax.ShapeDtypeStruct((m, LANES), F32)]
    if with_kv:
        out_specs += [row(hp, 0), pl.BlockSpec((None, MLA_WIDTH, tm), lambda i: (i, 0, 0))]
        out_shape += [jax.ShapeDtypeStruct((m, hp), BF16), jax.ShapeDtypeStruct((m // tm, MLA_WIDTH, tm), BF16)]
    return pl.pallas_call(
        functools.partial(_mla_proj_body, with_kv=with_kv),
        grid=(m // tm,),
        in_specs=[
            row(Q_LORA, C_QL // Q_LORA), row(KV_LORA, C_KVL // KV_LORA), row(LANES, C_KPE // LANES),
            row(LANES, C_KPESW // LANES), row(LANES, 0), row(LANES, 0),
            full((1, Q_LORA)), full((Q_LORA, hp)), full((Q_LORA, hp)), full((1, LANES)),
            full((1, KV_LORA)), full((KV_LORA, hp)), full((MLA_WIDTH, KV_LORA)), full((1, LANES)),
        ],
        out_specs=out_specs,
        out_shape=out_shape,
        compiler_params=_cparams("parallel"),
        name="mla_proj",
    )(proj, proj, proj, proj, cos_t, sin_t, lw["q_a_norm_w"], lw["w_q"], lw["w_qsw"], lw["q_norm_pat"],
      lw["kv_a_norm_w"], lw["w_k"], lw["w_v_t"], lw["k_norm_pat"])


def _flash_body(q_ref, k_ref, vt_ref, o_ref, m_ref, acc_ref, *, tq, tk):
    qi = pl.program_id(2)
    nfull = qi * (tq // tk)
    m_ref[...] = jnp.full(m_ref.shape, -jnp.inf, F32)
    acc_ref[...] = jnp.zeros(acc_ref.shape, F32)
    ones = jnp.ones((FLASH_LROWS, tk), BF16)

    tn = min(FLASH_TN, tq)

    def steps(kis, masked):
        chains = [(ki, nt, j) for ki in kis for nt in range(tq // tn) for j in range(2)]

        def qk(ch):
            ki, nt, j = ch
            koff = pl.multiple_of(ki * tk, tk)
            q = q_ref[nt * tn:(nt + 1) * tn, j * LANES:(j + 1) * LANES]
            k = k_ref[pl.ds(koff, tk), j * LANES:(j + 1) * LANES]
            return _dot_nt(k, q)

        def softmax(ch, st):
            ki, nt, j = ch
            cols = slice(nt * tn, (nt + 1) * tn)
            if masked:
                kpos = ki * tk + lax.broadcasted_iota(jnp.int32, (tk, tn), 0)
                qpos = qi * tq + nt * tn + lax.broadcasted_iota(jnp.int32, (tk, tn), 1)
                st = jnp.where(kpos <= qpos, st, -jnp.inf)
            m_old = m_ref[j, :, cols]
            m_new = jnp.maximum(m_old, jnp.max(st, axis=0, keepdims=True))
            p = jnp.exp2(st - m_new)
            alpha = jnp.exp2(m_old - m_new)
            m_ref[j, :, cols] = m_new
            lhs = jnp.concatenate([vt_ref[ki, j * V_DIM:(j + 1) * V_DIM, :], ones], axis=0)
            return alpha, _dot(lhs, p.astype(BF16))

        def fold(ch, alpha, pv):
            _, nt, j = ch
            cols = slice(nt * tn, (nt + 1) * tn)
            acc_ref[j, :, cols] = alpha * acc_ref[j, :, cols] + pv

        sts = [qk(c) for c in chains[:FLASH_DEPTH]]
        pending = None
        for i, ch in enumerate(chains):
            st = sts.pop(0)
            if i + FLASH_DEPTH < len(chains):
                sts.append(qk(chains[i + FLASH_DEPTH]))
            alpha, pv = softmax(ch, st)
            if pending is not None:
                fold(*pending)
            pending = (ch, alpha, pv)
        fold(*pending)

    unroll = tq // tk

    def full_steps(it, c):
        steps([it * unroll + u for u in range(unroll)], False)
        return c

    lax.fori_loop(0, qi, full_steps, 0)
    steps([nfull + d for d in range(tq // tk)], True)
    ot = jnp.concatenate([acc_ref[j, 0:V_DIM, :] / acc_ref[j, V_DIM:V_DIM + 1, :] for j in range(2)], axis=0)
    o_ref[...] = ot.T


def _flash(q, k, vt, *, batch, seq, tq=1024):
    tk = vt.shape[2]
    tq = min(tq, seq)
    nq = seq // tq
    nk = seq // tk
    return pl.pallas_call(
        functools.partial(_flash_body, tq=tq, tk=tk),
        grid=(batch, MLA_HEADS // 2, nq),
        in_specs=[
            pl.BlockSpec((tq, 2 * LANES), lambda b, h, i: (b * nq + i, h)),
            pl.BlockSpec((seq, 2 * LANES), lambda b, h, i: (b, h)),
            pl.BlockSpec((nk, 2 * V_DIM, tk), lambda b, h, i: (b, h, 0)),
        ],
        out_specs=pl.BlockSpec((tq, 2 * V_DIM), lambda b, h, i: (b * nq + i, h)),
        out_shape=jax.ShapeDtypeStruct((batch * seq, MLA_WIDTH), F32),
        scratch_shapes=[pltpu.VMEM((2, 1, tq), F32), pltpu.VMEM((2, V_DIM + FLASH_LROWS, tq), F32)],
        compiler_params=_cparams("parallel", "parallel", "arbitrary"),
        name="flash",
    )(q, k, vt)


def _qabs_body(q_ref, knw_ref, wabs_ref, epe_ref, qabs_ref, qpe_ref):
    qh = (q_ref[...].astype(F32) * knw_ref[...]).astype(BF16)
    qabs_ref[...] = _dot(qh, wabs_ref[...]).astype(BF16)
    qpe_ref[...] = _dot(qh, epe_ref[...]).astype(BF16)


def _qabs(q, lw):
    m = q.shape[0]
    return pl.pallas_call(
        _qabs_body,
        grid=(MLA_HEADS,),
        in_specs=[
            pl.BlockSpec((m, LANES), lambda h: (0, h)),
            pl.BlockSpec((1, LANES), lambda h: (0, 0)),
            pl.BlockSpec((None, LANES, KV_LORA), lambda h: (h, 0, 0)),
            pl.BlockSpec((LANES, LANES), lambda h: (0, 0)),
        ],
        out_specs=[pl.BlockSpec((m, KV_LORA), lambda h: (0, h)), pl.BlockSpec((m, LANES), lambda h: (0, h))],
        out_shape=[jax.ShapeDtypeStruct((m, MLA_HEADS * KV_LORA), BF16),
                   jax.ShapeDtypeStruct((m, MLA_HEADS * LANES), BF16)],
        compiler_params=_cparams("parallel"),
        name="qabs",
    )(q, lw["k_norm_pat"], lw["w_abs"], lw["e_pe"])


def _decode_body(pt_ref, *refs, pg, ngroups, nq):
    del pt_ref
    c_refs = refs[:pg]
    p_refs = refs[pg:2 * pg]
    (cnew_ref, pnew_ref, wt_ref, qabs_ref, qpe_ref, wv_ref, o_ref,
     m_ref, l_ref, acc_ref, lhs_ref) = refs[2 * pg:]
    g = pl.program_id(1)
    rows = nq * MLA_HEADS
    nk = MLA_HEADS * QK_NOPE

    @pl.when(g == 0)
    def _():
        m_ref[...] = jnp.full((rows, 1), -jnp.inf, F32)
        l_ref[...] = jnp.zeros((rows, 1), F32)
        acc_ref[...] = jnp.zeros((rows, KV_LORA), F32)
        lhs_ref[0:nk, :] = wt_ref[...]
        lhs_ref[nk:nk + rows, :] = qabs_ref[...]

    def project(c16):
        return _dot_nt(lhs_ref[...], c16)

    def scores(both, kpt):
        t = both.shape[1]
        kt = both[0:nk]
        ssq = jnp.sum((kt * kt).reshape(MLA_HEADS, QK_NOPE, t), axis=1)
        ssq_pe = jnp.sum(kpt * kpt, axis=0, keepdims=True)
        r = lax.rsqrt((ssq + ssq_pe) * (1.0 / QK_DIM) + EPS)
        st = both[nk:nk + rows] + _dot(qpe_ref[...], kpt.astype(BF16))
        return (st.reshape(nq, MLA_HEADS, t) * r[None]).reshape(rows, t)

    def update(st, c16s):
        m_old = m_ref[...]
        m_new = jnp.maximum(m_old, jnp.max(st, axis=-1, keepdims=True))
        p = jnp.exp2(st - m_new)
        alpha = jnp.exp2(m_old - m_new)
        l_ref[...] = alpha * l_ref[...] + jnp.sum(p, axis=-1, keepdims=True)
        p16 = p.astype(BF16)
        pv = None
        off = 0
        for c16 in c16s:
            t = c16.shape[0]
            d = _dot(p16[:, off:off + t], c16)
            pv = d if pv is None else pv + d
            off += t
        acc_ref[...] = alpha * acc_ref[...] + pv
        m_ref[...] = m_new

    def latent(k):
        return jnp.concatenate([c_refs[k + i][...] for i in range(DECODE_SUB)], axis=0).astype(BF16)

    subs = list(range(0, pg, DECODE_SUB))
    c16s = [latent(subs[0])]
    boths = [project(c16s[0])]
    sts = []
    for i, k in enumerate(subs):
        if i + 1 < len(subs):
            c16s.append(latent(subs[i + 1]))
            boths.append(project(c16s[-1]))
        kpt = jnp.concatenate([p_refs[k + i][...] for i in range(DECODE_SUB)], axis=1)
        sts.append(scores(boths[i], kpt))
    update(jnp.concatenate(sts, axis=1), c16s)

    @pl.when(g == ngroups - 1)
    def _():
        qrow = lax.broadcasted_iota(jnp.int32, (rows, PAGE), 0) // MLA_HEADS
        tok = lax.broadcasted_iota(jnp.int32, (rows, PAGE), 1)
        c16 = cnew_ref[...].astype(BF16)
        st = scores(project(c16), pnew_ref[...])
        update(jnp.where(tok <= qrow, st, -jnp.inf), [c16])
        olat = (acc_ref[...] / l_ref[...]).astype(BF16)
        of = _dot(olat, wv_ref[...])
        rh = lax.broadcasted_iota(jnp.int32, (rows, MLA_WIDTH), 0) % MLA_HEADS
        ch = lax.broadcasted_iota(jnp.int32, (rows, MLA_WIDTH), 1) // V_DIM
        of = jnp.where(rh == ch, of, 0.0)
        o_ref[...] = jnp.sum(of.reshape(nq, MLA_HEADS, MLA_WIDTH), axis=1)


def _decode_attn(page_table, pool_c, pool_pt, layer, cnew, pnew_t, qabs, qpe, lw, *, pg=16):
    nb, npages = page_table.shape
    nq = qabs.shape[1] // MLA_HEADS
    rows = nq * MLA_HEADS
    pg = min(pg, npages)
    ngroups = npages // pg

    def page_spec(shape, k):
        return pl.BlockSpec((None, None) + shape, lambda b, g, pt: (layer, pt[b, g * pg + k], 0, 0))

    per_seq = lambda shape: pl.BlockSpec((None,) + shape, lambda b, g, pt: (b, 0, 0))
    full = lambda shape: pl.BlockSpec(shape, lambda b, g, pt: (0,) * len(shape))
    in_specs = ([page_spec((PAGE, KV_LORA), k) for k in range(pg)]
                + [page_spec((QK_ROPE, PAGE), k) for k in range(pg)]
                + [per_seq((PAGE, KV_LORA)), per_seq((QK_ROPE, PAGE)), full((MLA_HEADS * QK_NOPE, KV_LORA)),
                   per_seq((rows, KV_LORA)), per_seq((rows, QK_ROPE)), full((KV_LORA, MLA_WIDTH))])
    grid_spec = pltpu.PrefetchScalarGridSpec(
        num_scalar_prefetch=1,
        grid=(nb, ngroups),
        in_specs=in_specs,
        out_specs=per_seq((nq, MLA_WIDTH)),
        scratch_shapes=[pltpu.VMEM((rows, 1), F32), pltpu.VMEM((rows, 1), F32), pltpu.VMEM((rows, KV_LORA), F32),
                        pltpu.VMEM((MLA_HEADS * QK_NOPE + rows, KV_LORA), BF16)],
    )
    return pl.pallas_call(
        functools.partial(_decode_body, pg=pg, ngroups=ngroups, nq=nq),
        grid_spec=grid_spec,
        out_shape=jax.ShapeDtypeStruct((nb, nq, MLA_WIDTH), F32),
        compiler_params=_cparams("parallel", "arbitrary"),
        name="decode_attn",
    )(page_table, *([pool_c] * pg), *([pool_pt] * pg), cnew, pnew_t, lw["w_nope_t"], qabs, qpe, lw["w_v"])


def _outproj_even_body(y_ref, o_ref, g_ref, h_ref, wy_ref, wo_ref, out_ref):
    og = (o_ref[...] * _silu(g_ref[...])).astype(BF16)
    out_ref[...] = h_ref[...] + _dot(y_ref[...].astype(BF16), wy_ref[...]) + _dot(og, wo_ref[...])


def _outproj_even(y, o, proj, h, w_out, tm=512):
    m = h.shape[0]
    tm = min(tm, m)
    row = lambda c: pl.BlockSpec((tm, D_MODEL), lambda i: (i, c))
    return pl.pallas_call(
        _outproj_even_body,
        grid=(m // tm,),
        in_specs=[row(0), row(0), row(C_G // D_MODEL), row(0),
                  pl.BlockSpec((SSM_D_INNER, D_MODEL), lambda i: (0, 0)),
                  pl.BlockSpec((MLA_WIDTH, D_MODEL), lambda i: (1, 0))],
        out_specs=row(0),
        out_shape=jax.ShapeDtypeStruct((m, D_MODEL), F32),
        compiler_params=_cparams("parallel"),
        name="outproj_even",
    )(y, o, proj, h, w_out, w_out)


def _gelu(x):
    return 0.5 * x * (1.0 + jnp.tanh(math.sqrt(2.0 / math.pi) * (x + 0.044715 * (x * x * x))))


def _s5_seq_body(u_ref, kst_ref, bst_ref, mr_ref, d_ref, are_ref, aim_ref, pre_ref, pim_ref,
                 y_ref, hre_ref, him_ref, xre_ref, xim_ref, ys_ref, *, lc):
    c = pl.program_id(2)
    tile = SUBLANES
    nt = lc // tile
    ns = S5_SLAB_STATE

    @pl.when(c == 0)
    def _():
        xre_ref[0:tile, :] = jnp.zeros((tile, ns), F32)
        xim_ref[0:tile, :] = jnp.zeros((tile, ns), F32)

    u = u_ref[...]
    row = lax.broadcasted_iota(jnp.int32, (lc, LANES), 0) % tile
    parts = [u.astype(BF16)]
    for j in range(1, tile):
        parts.append(jnp.where(row >= j, pltpu.roll(u, j, 0), 0.0).astype(BF16))
    y_local = _dot(jnp.concatenate(parts, axis=1), kst_ref[0])

    tstack = jnp.concatenate(
        [u_ref[pl.ds(tile - 1 - j, nt, stride=tile), :].astype(BF16) for j in range(tile)], axis=1)
    v = _dot(tstack, bst_ref[0])
    re, im = v[:, :ns], v[:, ns:]
    for si in range(3):
        sh = 1 << si
        sre = pltpu.roll(re, sh, 0).reshape(nt // tile, tile, ns)
        sim = pltpu.roll(im, sh, 0).reshape(nt // tile, tile, ns)
        ar = are_ref[0, si][None]
        ai = aim_ref[0, si][None]
        re3 = re.reshape(nt // tile, tile, ns) + ar * sre - ai * sim
        im3 = im.reshape(nt // tile, tile, ns) + ar * sim + ai * sre
        re = re3.reshape(nt, ns)
        im = im3.reshape(nt, ns)
    cr = jnp.broadcast_to(xre_ref[tile - 1:tile, :], (tile, ns))
    ci = jnp.broadcast_to(xim_ref[tile - 1:tile, :], (tile, ns))
    pr = pre_ref[0]
    pi = pim_ref[0]
    for g in range(nt // tile):
        xr = re[g * tile:(g + 1) * tile] + pr * cr - pi * ci
        xi = im[g * tile:(g + 1) * tile] + pr * ci + pi * cr
        xre_ref[(g + 1) * tile:(g + 2) * tile, :] = xr
        xim_ref[(g + 1) * tile:(g + 2) * tile, :] = xi
        cr = jnp.broadcast_to(xr[tile - 1:tile, :], (tile, ns))
        ci = jnp.broadcast_to(xi[tile - 1:tile, :], (tile, ns))
    hre_ref[...] = cr[0:1, :]
    him_ref[...] = ci[0:1, :]

    xp = jnp.concatenate([xre_ref[tile - 1:tile - 1 + nt, :], xim_ref[tile - 1:tile - 1 + nt, :]],
                         axis=1).astype(BF16)
    for r in range(tile):
        ys_ref[pl.ds(r, nt, stride=tile), :] = _dot(xp, mr_ref[0, r])
    xre_ref[0:tile, :] = xre_ref[nt:nt + tile, :]
    xim_ref[0:tile, :] = xim_ref[nt:nt + tile, :]
    y_ref[...] = _gelu(y_local + ys_ref[...] + d_ref[...] * u)


def _s5_seq(proj, lw, *, batch, seq, lc=1024):
    lc = min(lc, seq)
    nc = seq // lc
    ns = S5_SLAB_STATE
    nst = S5_GROUPS * S5_STATE
    kdim = SUBLANES * LANES
    slab = lambda *shape: pl.BlockSpec((1,) + shape, lambda b, j, c: (j,) + (0,) * len(shape))
    return pl.pallas_call(
        functools.partial(_s5_seq_body, lc=lc),
        grid=(batch, S5_SLABS, nc),
        in_specs=[
            pl.BlockSpec((lc, LANES), lambda b, j, c: (b * nc + c, j)),
            slab(kdim, LANES), slab(kdim, 2 * ns), slab(SUBLANES, 2 * ns, LANES),
            pl.BlockSpec((1, LANES), lambda b, j, c: (0, j)),
            slab(3, SUBLANES, ns), slab(3, SUBLANES, ns), slab(SUBLANES, ns), slab(SUBLANES, ns),
        ],
        out_specs=[
            pl.BlockSpec((lc, LANES), lambda b, j, c: (b * nc + c, j)),
            pl.BlockSpec((None, 1, ns), lambda b, j, c: (b, 0, j)),
            pl.BlockSpec((None, 1, ns), lambda b, j, c: (b, 0, j)),
        ],
        out_shape=[
            jax.ShapeDtypeStruct((batch * seq, S5_WIDTH), F32),
            jax.ShapeDtypeStruct((batch, 1, nst), F32),
            jax.ShapeDtypeStruct((batch, 1, nst), F32),
        ],
        scratch_shapes=[pltpu.VMEM((lc // SUBLANES + SUBLANES, ns), F32),
                        pltpu.VMEM((lc // SUBLANES + SUBLANES, ns), F32), pltpu.VMEM((lc, LANES), F32)],
        compiler_params=_cparams("parallel", "arbitrary", "arbitrary"),
        name="s5_seq",
    )(proj, lw["k_stack"], lw["b_stack"], lw["m_rows"], lw["d"], lw["a8_re_hs"], lw["a8_im_hs"],
      lw["p8_re"], lw["p8_im"])


def _s5_step_body(u_ref, h0re_ref, h0im_ref, bre_ref, bim_ref, brel_ref, biml_ref, cre_ref, cim_ref, d_ref,
                  are_ref, aim_ref, y_ref, hre_ref, him_ref, *, nt):
    xr = h0re_ref[...]
    xi = h0im_ref[...]
    ar = are_ref[0]
    ai = aim_ref[0]
    for t in range(nt):
        u = u_ref[t]
        uh = u.astype(BF16)
        ul = (u - uh.astype(F32)).astype(BF16)
        bur = _dot(uh, bre_ref[0]) + _dot(ul, bre_ref[0]) + _dot(uh, brel_ref[0])
        bui = _dot(uh, bim_ref[0]) + _dot(ul, bim_ref[0]) + _dot(uh, biml_ref[0])
        xr, xi = ar * xr - ai * xi + bur, ar * xi + ai * xr + bui
        y = _dot(xr.astype(BF16), cre_ref[0]) + _dot(xi.astype(BF16), cim_ref[0])
        y_ref[t] = _gelu(y + d_ref[...] * u)
    hre_ref[...] = xr
    him_ref[...] = xi


def _s5_step(u_t, h0re, h0im, lw):
    nt, nb, _ = u_t.shape
    ns = S5_SLAB_STATE
    slab3 = lambda a, b: pl.BlockSpec((1, a, b), lambda j: (j, 0, 0))
    return pl.pallas_call(
        functools.partial(_s5_step_body, nt=nt),
        grid=(S5_SLABS,),
        in_specs=[
            pl.BlockSpec((nt, nb, LANES), lambda j: (0, 0, j)),
            pl.BlockSpec((nb, ns), lambda j: (0, j)), pl.BlockSpec((nb, ns), lambda j: (0, j)),
            slab3(LANES, ns), slab3(LANES, ns), slab3(LANES, ns), slab3(LANES, ns),
            slab3(ns, LANES), slab3(ns, LANES),
            pl.BlockSpec((1, LANES), lambda j: (0, j)),
            slab3(1, ns), slab3(1, ns),
        ],
        out_specs=[
            pl.BlockSpec((nt, nb, LANES), lambda j: (0, 0, j)),
            pl.BlockSpec((nb, ns), lambda j: (0, j)), pl.BlockSpec((nb, ns), lambda j: (0, j)),
        ],
        out_shape=[
            jax.ShapeDtypeStruct((nt, nb, S5_WIDTH), F32),
            jax.ShapeDtypeStruct((nb, S5_GROUPS * S5_STATE), F32),
            jax.ShapeDtypeStruct((nb, S5_GROUPS * S5_STATE), F32),
        ],
        compiler_params=_cparams("parallel"),
        name="s5_step",
    )(u_t, h0re, h0im, lw["b_re"], lw["b_im"], lw["b_re_lo"], lw["b_im_lo"], lw["c_re"], lw["c_im"], lw["d"],
      lw["a_re1"], lw["a_im1"])


def _odd_tail_body(y_ref, z_ref, h_ref, wg_ref, bg_ref, wo_ref, out_ref):
    y = y_ref[...]
    gl = _dot(y.astype(BF16), wg_ref[...]) + bg_ref[...]
    y = y * (1.0 / (1.0 + jnp.exp(-gl)))
    y = y * _silu(z_ref[...])
    out_ref[...] = h_ref[...] + _dot(y.astype(BF16), wo_ref[...])


def _odd_tail(y, proj, h, lw, tm=512):
    m = h.shape[0]
    tm = min(tm, m)
    row = lambda c: pl.BlockSpec((tm, D_MODEL), lambda i: (i, c))
    full = lambda shape: pl.BlockSpec(shape, lambda i: (0,) * len(shape))
    return pl.pallas_call(
        _odd_tail_body,
        grid=(m // tm,),
        in_specs=[row(0), row(1), row(0), full((S5_WIDTH, S5_WIDTH)), full((1, S5_WIDTH)),
                  full((S5_WIDTH, D_MODEL))],
        out_specs=row(0),
        out_shape=jax.ShapeDtypeStruct((m, D_MODEL), F32),
        compiler_params=_cparams("parallel"),
        name="odd_tail",
    )(y, proj, h, lw["w_glu"], lw["b_glu"], lw["w_out"])


def _rot_half_cols(w):
    half = QK_ROPE // 2
    return jnp.concatenate([-w[..., half:], w[..., :half]], axis=-1)


def _head_pad(x_nope, x_rope):
    z = jnp.zeros(x_nope.shape[:-1] + (LANES - QK_DIM,), x_nope.dtype)
    out = jnp.concatenate([x_nope, x_rope, z], axis=-1)
    return out.reshape(out.shape[:-2] + (MLA_HEADS * LANES,))


def _even_weights(w_in, conv_w, conv_b, dt_bias, a_log, d_ssm, ssm_norm_w, q_a_norm_w, w_qb, kv_a_norm_w, w_kvb,
                  q_norm_w, k_norm_w, w_out):
    k = w_in.shape[0]
    o = 0
    parts = {}
    for name, sz in (("z", SSM_D_INNER), ("xbc", SSM_CONV_CH), ("dt", SSM_HEADS), ("ql", Q_LORA), ("kvl", KV_LORA),
                     ("kpe", QK_ROPE), ("g", MLA_WIDTH)):
        parts[name] = w_in[:, o:o + sz]
        o += sz
    zc = lambda n: jnp.zeros((k, n), F32)
    kpe_blk = jnp.concatenate([zc(QK_NOPE), parts["kpe"], zc(LANES - QK_DIM)], axis=1)
    kpesw_blk = jnp.concatenate([zc(QK_NOPE), _rot_half_cols(parts["kpe"]), zc(LANES - QK_DIM)], axis=1)
    dt_blk = jnp.concatenate([parts["dt"], zc(LANES - SSM_HEADS)], axis=1)
    w_in_p = jnp.concatenate([parts["xbc"], parts["z"], parts["g"], parts["ql"], parts["kvl"], kpe_blk, kpesw_blk,
                              dt_blk, zc(N_EVEN_PAD - C_DT - LANES)], axis=1).astype(BF16)
    wq = w_qb.reshape(Q_LORA, MLA_HEADS, QK_DIM)
    zq = jnp.zeros((Q_LORA, MLA_HEADS, QK_NOPE), F32)
    wkv = w_kvb.reshape(KV_LORA, MLA_HEADS, QK_NOPE + V_DIM)
    w_nope = wkv[..., :QK_NOPE]
    pad1 = lambda v, n: jnp.concatenate([v, jnp.zeros((n - v.shape[0],), F32)]).reshape(1, n)
    norm_pat = lambda w: jnp.concatenate([w, jnp.zeros((LANES - QK_DIM,), F32)]).reshape(1, LANES)
    w_abs = jnp.concatenate([jnp.transpose(w_nope, (1, 2, 0)),
                             jnp.zeros((MLA_HEADS, LANES - QK_NOPE, KV_LORA), F32)], axis=1)
    e_pe = jnp.zeros((LANES, LANES), F32).at[QK_NOPE + jnp.arange(QK_ROPE), jnp.arange(QK_ROPE)].set(1.0)
    head_expand = (jnp.arange(LANES)[:, None] == (jnp.arange(SSM_D_INNER)[None, :] // SSM_HEAD_DIM))
    return {
        "w_in": w_in_p,
        "conv_w": conv_w, "conv_b": conv_b.reshape(1, -1),
        "dt_bias": pad1(dt_bias, LANES), "a_log": pad1(a_log, LANES),
        "d_exp": jnp.repeat(d_ssm, SSM_HEAD_DIM).reshape(1, -1),
        "ssm_norm_w": ssm_norm_w.reshape(1, -1),
        "head_expand": head_expand.astype(BF16),
        "q_a_norm_w": q_a_norm_w.reshape(1, -1),
        "w_q": _head_pad(wq[..., :QK_NOPE], wq[..., QK_NOPE:]).astype(BF16),
        "w_qsw": _head_pad(zq, _rot_half_cols(wq[..., QK_NOPE:])).astype(BF16),
        "q_norm_pat": norm_pat(q_norm_w),
        "kv_a_norm_w": kv_a_norm_w.reshape(1, -1),
        "w_k": _head_pad(w_nope, jnp.zeros((KV_LORA, MLA_HEADS, QK_ROPE), F32)).astype(BF16),
        "w_v": wkv[..., QK_NOPE:].reshape(KV_LORA, MLA_WIDTH).astype(BF16),
        "w_v_t": wkv[..., QK_NOPE:].reshape(KV_LORA, MLA_WIDTH).T.astype(BF16),
        "k_norm_pat": norm_pat(k_norm_w),
        "w_abs": w_abs.astype(BF16),
        "e_pe": e_pe.astype(BF16),
        "w_nope_t": jnp.transpose(w_nope, (1, 2, 0)).reshape(MLA_HEADS * QK_NOPE, KV_LORA).astype(BF16),
        "w_out": w_out.astype(BF16),
    }


def _rope_tables(pos):
    half = QK_ROPE // 2
    inv_freq = jnp.power(ROPE_THETA, -jnp.arange(half, dtype=F32) / half)
    ang = pos[:, None] * inv_freq[None, :]
    c, s = jnp.cos(ang), jnp.sin(ang)
    n = pos.shape[0]
    cos_t = jnp.concatenate([jnp.ones((n, QK_NOPE), F32), c, c, jnp.ones((n, LANES - QK_DIM), F32)], axis=1)
    sin_t = jnp.concatenate([jnp.zeros((n, QK_NOPE), F32), s, s, jnp.zeros((n, LANES - QK_DIM), F32)], axis=1)
    return cos_t, sin_t


def _odd_weights(w_in, a_re, a_im, b_re, b_im, c_re, c_im, d, log_step, w_glu, b_glu, w_out):
    ar, ai = a_re.astype(F32), a_im.astype(F32)
    step = jnp.exp(log_step.astype(F32))[:, None]
    mag = jnp.exp(ar * step)
    ab_re, ab_im = mag * jnp.cos(ai * step), mag * jnp.sin(ai * step)
    den = ar * ar + ai * ai
    nr, ni = ab_re - 1.0, ab_im
    f_re = (nr * ar + ni * ai) / den
    f_im = (ni * ar - nr * ai) / den
    bb_re = f_re[..., None] * b_re - f_im[..., None] * b_im
    bb_im = f_re[..., None] * b_im + f_im[..., None] * b_re
    gl = LANES // S5_GROUP_CH

    eye = jnp.eye(gl, dtype=F32)

    def b_blocks(bb):
        x = bb.reshape(-1, S5_SLABS, gl, S5_STATE, S5_GROUP_CH)
        return jnp.einsum("jsgnc,gh->sjgchn", x, eye).reshape(S5_SLABS, -1, S5_SLAB_STATE)

    def c_blocks(cc):
        x = cc.reshape(-1, S5_SLABS, gl, S5_GROUP_CH, S5_STATE)
        return jnp.einsum("jsgcn,gh->sjgnhc", x, eye).reshape(S5_SLABS, -1, S5_SLAB_STATE, LANES)

    def k_blocks(kk):
        x = kk.reshape(-1, S5_SLABS, gl, S5_GROUP_CH, S5_GROUP_CH)
        return jnp.einsum("jsgcd,gh->sjgdhc", x, eye).reshape(S5_SLABS, -1, LANES)

    def powers(base_re, base_im, n):
        pr, pi = jnp.ones_like(base_re), jnp.zeros_like(base_im)
        out = []
        for _ in range(n):
            pr, pi = pr * base_re - pi * base_im, pr * base_im + pi * base_re
            out.append((pr, pi))
        return out

    pw = powers(ab_re, ab_im, SUBLANES)
    p_re = jnp.stack([jnp.ones_like(ab_re)] + [p[0] for p in pw])
    p_im = jnp.stack([jnp.zeros_like(ab_im)] + [p[1] for p in pw])
    cr, ci = c_re.astype(F32), c_im.astype(F32)
    t_re, t_im = p_re[:SUBLANES, :, :, None], p_im[:SUBLANES, :, :, None]
    bj_re = t_re * bb_re - t_im * bb_im
    bj_im = t_re * bb_im + t_im * bb_re
    b_stack = jnp.concatenate([b_blocks(bj_re), b_blocks(bj_im)], axis=2)
    hi = lax.Precision.HIGHEST
    k_stack = k_blocks(jnp.einsum("gcn,jgnd->jgcd", cr, bj_re, precision=hi)
                       - jnp.einsum("gcn,jgnd->jgcd", ci, bj_im, precision=hi))
    q_re, q_im = p_re[1:, :, None, :], p_im[1:, :, None, :]
    m_rows = jnp.concatenate([c_blocks(cr * q_re - ci * q_im), c_blocks(-(cr * q_im + ci * q_re))], axis=2)
    pw8 = powers(pw[-1][0], pw[-1][1], SUBLANES)
    slab = lambda x: x.reshape(S5_SLABS, S5_SLAB_STATE)
    rows = jnp.arange(SUBLANES)[None, :, None]
    hs8_re = jnp.stack([jnp.where(rows >= sh, slab(pw8[sh - 1][0])[:, None, :], 0.0) for sh in (1, 2, 4)], axis=1)
    hs8_im = jnp.stack([jnp.where(rows >= sh, slab(pw8[sh - 1][1])[:, None, :], 0.0) for sh in (1, 2, 4)], axis=1)
    p8_re = jnp.stack([slab(p[0]) for p in pw8], axis=1)
    p8_im = jnp.stack([slab(p[1]) for p in pw8], axis=1)
    bre_f, bim_f = b_blocks(bb_re), b_blocks(bb_im)
    bre16, bim16 = bre_f.astype(BF16), bim_f.astype(BF16)
    return {
        "w_in": w_in.astype(BF16),
        "b_re": bre16, "b_im": bim16,
        "b_re_lo": (bre_f - bre16.astype(F32)).astype(BF16), "b_im_lo": (bim_f - bim16.astype(F32)).astype(BF16),
        "c_re": c_blocks(cr)[:, 0].astype(BF16), "c_im": c_blocks(-ci)[:, 0].astype(BF16),
        "d": d.reshape(1, -1),
        "k_stack": k_stack.astype(BF16), "b_stack": b_stack.astype(BF16), "m_rows": m_rows.astype(BF16),
        "a8_re_hs": hs8_re, "a8_im_hs": hs8_im, "p8_re": p8_re, "p8_im": p8_im,
        "a_re1": slab(ab_re).reshape(S5_SLABS, 1, S5_SLAB_STATE),
        "a_im1": slab(ab_im).reshape(S5_SLABS, 1, S5_SLAB_STATE),
        "w_glu": w_glu.astype(BF16), "b_glu": b_glu.reshape(1, -1), "w_out": w_out.astype(BF16),
    }


SAMPLE_ROWS = 8


def _even_layer(hp, hs, norm_w, lw, tabs, state_ssm, state_conv, pool_c, pool_pt, layer, page_table, bp, sp, bs, ss):
    cos_p, sin_p, cos_s, sin_s = tabs
    hist = SSM_CONV - 1
    proj_p = _inproj(hp, norm_w, lw["w_in"])
    proj_s = _inproj(hs, norm_w, lw["w_in"])
    nchunks = sp // SSM_CHUNK
    zero_state = jnp.zeros((bp, SSM_D_INNER, SSM_STATE), F32)
    y_p, ssm_p = _ssd(proj_p, proj_p, proj_p, zero_state, lw, nblk=bp * nchunks, nchunks=nchunks, nseq=1,
                      P=SSM_CHUNK, n_hist=0, n_real=SSM_CHUNK, carry=True,
                      xbc_col=0, z_col=C_Z // SSM_D_INNER, dt_col=C_DT // LANES)
    conv_p = proj_p.reshape(bp, sp, N_EVEN_PAD)[:, sp - hist:, C_XBC:C_XBC + SSM_CONV_CH]
    xbc_s = proj_s[:, C_XBC:C_XBC + SSM_CONV_CH].reshape(bs, ss, SSM_CONV_CH)
    xbc_full = jnp.concatenate([state_conv, xbc_s], axis=1)
    pad_rows = SAMPLE_ROWS - hist - ss
    pad3 = lambda a: jnp.pad(a, ((0, 0), (hist, pad_rows), (0, 0)))
    xbc_pad = jnp.pad(xbc_full, ((0, 0), (0, pad_rows), (0, 0))).reshape(bs * SAMPLE_ROWS, SSM_CONV_CH)
    z_pad = pad3(proj_s[:, C_Z:C_Z + SSM_D_INNER].reshape(bs, ss, -1)).reshape(bs * SAMPLE_ROWS, -1)
    dt_pad = pad3(proj_s[:, C_DT:C_DT + LANES].reshape(bs, ss, -1)).reshape(bs * SAMPLE_ROWS, -1)
    nseq = SSM_CHUNK // SAMPLE_ROWS
    y_s_pad, ssm_s = _ssd(xbc_pad, z_pad, dt_pad, state_ssm.reshape(-1, SSM_D_INNER, SSM_STATE), lw,
                          nblk=bs // nseq, nchunks=1, nseq=nseq, P=SAMPLE_ROWS, n_hist=hist, n_real=ss,
                          carry=False, xbc_col=0, z_col=0, dt_col=0, h0_offset=layer * bs)
    y_s = y_s_pad.reshape(bs, SAMPLE_ROWS, SSM_D_INNER)[:, hist:hist + ss].reshape(bs * ss, SSM_D_INNER)
    conv_s = xbc_full[:, ss:]
    q_p, ckv_p, kper_p, k_p, v_p = _mla_proj(proj_p, cos_p, sin_p, lw, with_kv=True, tm=FLASH_TK)
    q_s, ckv_s, kper_s = _mla_proj(proj_s, cos_s, sin_s, lw, with_kv=False)
    o_p = _flash(q_p, k_p, v_p, batch=bp, seq=sp)
    kpe_p = kper_p[:, QK_NOPE:QK_DIM]
    kpe_s = kper_s[:, QK_NOPE:QK_DIM]
    qabs, qpe = _qabs(q_s, lw)
    qabs = qabs.reshape(bs, ss * MLA_HEADS, KV_LORA)
    qpe = qpe.reshape(bs, ss * MLA_HEADS, LANES)[:, :, :QK_ROPE]
    cnew = jnp.pad(ckv_s.reshape(bs, ss, KV_LORA), ((0, 0), (0, PAGE - ss), (0, 0)))
    pnew_t = jnp.pad(jnp.swapaxes(kpe_s.reshape(bs, ss, QK_ROPE), 1, 2), ((0, 0), (0, 0), (0, PAGE - ss)))
    o_s = _decode_attn(page_table, pool_c, pool_pt, layer, cnew, pnew_t, qabs, qpe, lw).reshape(bs * ss, MLA_WIDTH)
    hp = _outproj_even(y_p, o_p, proj_p, hp, lw["w_out"])
    hs = _outproj_even(y_s, o_s, proj_s, hs, lw["w_out"])
    outs = (ckv_p.reshape(bp, sp, KV_LORA), kpe_p.reshape(bp, sp, QK_ROPE), ckv_s.reshape(bs, ss, KV_LORA),
            kpe_s.reshape(bs, ss, QK_ROPE), ssm_p.reshape(bp, SSM_HEADS, SSM_HEAD_DIM, SSM_STATE),
            ssm_s.reshape(bs, SSM_HEADS, SSM_HEAD_DIM, SSM_STATE), conv_p, conv_s)
    return hp, hs, outs


def _odd_layer(hp, hs, norm_w, lw, s5_re, s5_im, bp, sp, bs, ss):
    proj_p = _inproj(hp, norm_w, lw["w_in"])
    proj_s = _inproj(hs, norm_w, lw["w_in"])
    y_p, r_p, i_p = _s5_seq(proj_p, lw, batch=bp, seq=sp)
    u_t = jnp.transpose(proj_s[:, :S5_WIDTH].reshape(bs, ss, S5_WIDTH), (1, 0, 2))
    y_t, r_s, i_s = _s5_step(u_t, s5_re.reshape(bs, -1), s5_im.reshape(bs, -1), lw)
    y_s = jnp.transpose(y_t, (1, 0, 2)).reshape(bs * ss, S5_WIDTH)
    hp = _odd_tail(y_p, proj_p, hp, lw)
    hs = _odd_tail(y_s, proj_s, hs, lw)
    shp = lambda a, b: a.reshape(b, S5_GROUPS, S5_STATE)
    return hp, hs, (shp(r_p, bp), shp(i_p, bp), shp(r_s, bs), shp(i_s, bs))


def kernel(x_prompt, x_sample, cache_ckv, cache_kpe, page_table, state_ssm, state_conv, state_s5_re, state_s5_im, norm_w, w_in_even, conv_w, conv_b, dt_bias, a_log, d_ssm, ssm_norm_w, q_a_norm_w, w_qb, kv_a_norm_w, w_kvb, q_norm_w, k_norm_w, w_out_even, w_in_odd, s5_a_re, s5_a_im, s5_b_re, s5_b_im, s5_c_re, s5_c_im, s5_d, s5_log_step, w_glu, b_glu, w_out_odd):
    bp, sp, _ = x_prompt.shape
    bs, ss, _ = x_sample.shape
    past_len = page_table.shape[1] * PAGE
    depth = norm_w.shape[0]
    pos_p = jnp.tile(jnp.arange(sp, dtype=F32), bp)
    pos_s = jnp.tile(past_len + jnp.arange(ss, dtype=F32), bs)
    tabs = _rope_tables(pos_p) + _rope_tables(pos_s)
    cache_kpe_t = jnp.swapaxes(cache_kpe, 2, 3)
    hp = x_prompt.reshape(bp * sp, D_MODEL)
    hs = x_sample.reshape(bs * ss, D_MODEL)
    even_out, odd_out = [], []
    for i in range(depth):
        j = i // 2
        if i % 2 == 0:
            lw = _even_weights(w_in_even[j], conv_w[j], conv_b[j], dt_bias[j], a_log[j], d_ssm[j], ssm_norm_w[j],
                               q_a_norm_w[j], w_qb[j], kv_a_norm_w[j], w_kvb[j], q_norm_w[j], k_norm_w[j],
                               w_out_even[j])
            hp, hs, o = _even_layer(hp, hs, norm_w[i], lw, tabs, state_ssm, state_conv[j], cache_ckv,
                                    cache_kpe_t, j, page_table, bp, sp, bs, ss)
            even_out.append(o)
        else:
            lw = _odd_weights(w_in_odd[j], s5_a_re[j], s5_a_im[j], s5_b_re[j], s5_b_im[j], s5_c_re[j], s5_c_im[j],
                              s5_d[j], s5_log_step[j], w_glu[j], b_glu[j], w_out_odd[j])
            hp, hs, o = _odd_layer(hp, hs, norm_w[i], lw, state_s5_re[j], state_s5_im[j], bp, sp, bs, ss)
            odd_out.append(o)
    ev = [jnp.stack([o[k] for o in even_out]) for k in range(8)]
    od = [jnp.stack([o[k] for o in odd_out]) for k in range(4)]
    return (hp.reshape(bp, sp, D_MODEL), hs.reshape(bs, ss, D_MODEL),
            ev[0], ev[1], ev[2], ev[3], ev[4], ev[5], ev[6], ev[7], od[0], od[1], od[2], od[3])
```
